```python
import math
import jax, jax.numpy as jnp
from jax import lax
import numpy as np

D_MODEL = 1024
BATCH = 32
SEQ = 256
DEPTH = 1
DEC_BATCH = 2
DEC_SEQ = 4096
PAST_LEN = 256

GRID_W = 64
D_MIX = D_MODEL
N_HEADS_A = 4
HEAD_DIM_A = 64
V_DIM_A = 2 * HEAD_DIM_A
D_ATTN = N_HEADS_A * V_DIM_A
D_RG = D_MIX - D_ATTN
RG_BLOCKS = 8
RG_BLOCK_W = D_RG // RG_BLOCKS
RG_C = 8.0
CONV_W = 4
CONV_PAD_LEFT = 2
D_FF = 2816
N_MOD = 9
D_IN = 3 * D_ATTN + 2 * D_RG
ROPE_BASE = 10000.0
QBLOCK = 128
EPS = 1e-6

kernel_name = 'hymba_diffattn_rglru_macaron_dit_step'


def _rmsnorm(x, g):
    xf = x.astype(jnp.float32)
    y = xf * lax.rsqrt(jnp.mean(xf * xf, axis=-1, keepdims=True) + EPS)
    return (y * g.astype(jnp.float32)).astype(x.dtype)


def _swiglu(h, w_gate, w_up, w_down):
    return (jax.nn.silu(h @ w_gate) * (h @ w_up)) @ w_down


def _rope_2d(x):
    S = x.shape[1]
    rows = S // GRID_W
    row = jnp.repeat(jnp.arange(rows), GRID_W)
    col = jnp.tile(jnp.arange(GRID_W), rows)
    n_freq = HEAD_DIM_A // 4
    inv_freq = ROPE_BASE ** (-jnp.arange(n_freq, dtype=jnp.float32) / n_freq)

    def rot(xh, pos):
        ang = pos.astype(jnp.float32)[:, None] * inv_freq[None, :]
        cos = jnp.cos(ang)[None, :, None, None, :].astype(x.dtype)
        sin = jnp.sin(ang)[None, :, None, None, :].astype(x.dtype)
        x1, x2 = xh[..., :n_freq], xh[..., n_freq:]
        return jnp.concatenate([x1 * cos - x2 * sin, x1 * sin + x2 * cos], axis=-1)

    half = HEAD_DIM_A // 2
    return jnp.concatenate([rot(x[..., :half], row), rot(x[..., half:], col)], axis=-1)


def _diff_attention(q, k, v, lam):
    B, Sq = q.shape[0], q.shape[1]
    nblk = Sq // QBLOCK
    scale = HEAD_DIM_A ** -0.5
    qb = jnp.moveaxis(q.reshape(B, nblk, QBLOCK, N_HEADS_A, 2, HEAD_DIM_A), 1, 0)

    def block(qi):
        s = jnp.einsum('bqhmd,bkhmd->bhmqk', qi, k, preferred_element_type=jnp.float32) * scale
        p = jax.nn.softmax(s, axis=-1)
        pd = p[:, :, 0] - lam * p[:, :, 1]
        return jnp.einsum('bhqk,bkhv->bqhv', pd.astype(v.dtype), v)

    o = lax.map(block, qb)
    return jnp.moveaxis(o, 0, 1).reshape(B, Sq, N_HEADS_A, V_DIM_A)


def _dwconv(x, w, b):
    S = x.shape[1]
    xp = jnp.pad(x, ((0, 0), (CONV_PAD_LEFT, CONV_W - 1 - CONV_PAD_LEFT), (0, 0)))
    return sum(xp[:, j:j + S] * w[j] for j in range(CONV_W)) + b


def _combine(e1, e2):
    a1, b1 = e1
    a2, b2 = e2
    return a1 * a2, a2 * b1 + b2


def _rglru_dir(xc, w_r, b_r, w_i, b_i, lam, h0, reverse):
    B, S, _ = xc.shape
    xf = xc.astype(jnp.float32)
    xblk = xf.reshape(B, S, RG_BLOCKS, RG_BLOCK_W)
    r = jax.nn.sigmoid(jnp.einsum('bsnc,ncd->bsnd', xblk, w_r.astype(jnp.float32)).reshape(B, S, D_RG) + b_r.astype(jnp.float32))
    i = jax.nn.sigmoid(jnp.einsum('bsnc,ncd->bsnd', xblk, w_i.astype(jnp.float32)).reshape(B, S, D_RG) + b_i.astype(jnp.float32))
    log_a = -RG_C * r * jax.nn.softplus(-lam.astype(jnp.float32))
    a = jnp.exp(log_a)
    bx = jnp.sqrt(-jnp.expm1(2.0 * log_a)) * (i * xf)
    h0 = h0.astype(jnp.float32)
    if reverse:
        bx = bx.at[:, -1].add(a[:, -1] * h0)
    else:
        bx = bx.at[:, 0].add(a[:, 0] * h0)
    _, h = lax.associative_scan(_combine, (a, bx), reverse=reverse, axis=1)
    h_last = h[:, 0] if reverse else h[:, -1]
    return h, h_last


def _rglru(xr, xg, p, l, h0):
    xc = _dwconv(xr, p['conv_w'][l], p['conv_b'][l])
    hf, hf_last = _rglru_dir(xc, p['rg_w_r'][l, 0], p['rg_b_r'][l, 0], p['rg_w_i'][l, 0], p['rg_b_i'][l, 0], p['rg_lambda'][l, 0], h0[:, 0], False)
    hb, hb_last = _rglru_dir(xc, p['rg_w_r'][l, 1], p['rg_b_r'][l, 1], p['rg_w_i'][l, 1], p['rg_b_i'][l, 1], p['rg_lambda'][l, 1], h0[:, 1], True)
    y = ((hf + hb) * jax.nn.gelu(xg.astype(jnp.float32))).astype(xr.dtype)
    return y, jnp.stack([hf_last, hb_last], axis=1).astype(xr.dtype)


def _layer(x, c, p, l, ctx):
    B, S, _ = x.shape
    mod = jax.nn.silu(c) @ p['w_mod'][l] + p['b_mod'][l]
    mod = mod.reshape((-1, 1, N_MOD * D_MODEL))
    sh1, sc1, g1, sh2, sc2, g2, sh3, sc3, g3 = jnp.split(mod, N_MOD, axis=-1)
    ng = p['norm_g'][l]
    h = _rmsnorm(x, ng[0]) * (1.0 + sc1) + sh1
    x = x + g1 * 0.5 * _swiglu(h, p['ffn_w_gate'][l, 0], p['ffn_w_up'][l, 0], p['ffn_w_down'][l, 0])
    h = _rmsnorm(x, ng[1]) * (1.0 + sc2) + sh2
    proj = h @ p['w_in'][l]
    q, k, v, xr, xg = jnp.split(proj, [D_ATTN, 2 * D_ATTN, 3 * D_ATTN, 3 * D_ATTN + D_RG], axis=-1)
    q = q.reshape(B, S, N_HEADS_A, 2, HEAD_DIM_A)
    k = k.reshape(B, S, N_HEADS_A, 2, HEAD_DIM_A)
    v = v.reshape(B, S, N_HEADS_A, V_DIM_A)
    lp = p['diff_lambda'][l].astype(jnp.float32)
    lam_init = 0.8 - 0.6 * math.exp(-0.3 * l)
    lam = jnp.exp(jnp.sum(lp[0] * lp[1])) - jnp.exp(jnp.sum(lp[2] * lp[3])) + lam_init
    if ctx is None:
        k_all, v_all = k, v
        h0 = jnp.zeros((B, 2, D_RG), jnp.float32)
    else:
        k_ctx, v_ctx, h0 = ctx
        q = _rope_2d(q)
        k = _rope_2d(k)
        k_all = jnp.concatenate([k, k_ctx.reshape(B, -1, N_HEADS_A, 2, HEAD_DIM_A).astype(k.dtype)], axis=1)
        v_all = jnp.concatenate([v, v_ctx.astype(v.dtype)], axis=1)
    o = _diff_attention(q, k_all, v_all, lam)
    o = _rmsnorm(o, p['subln_g'][l]) * (1.0 - lam_init)
    rg, h_last = _rglru(xr, xg, p, l, h0)
    mix = jnp.concatenate([o.reshape(B, S, D_ATTN), rg], axis=-1) @ p['w_out'][l]
    x = x + g2 * mix
    h = _rmsnorm(x, ng[2]) * (1.0 + sc3) + sh3
    x = x + g3 * 0.5 * _swiglu(h, p['ffn_w_gate'][l, 1], p['ffn_w_up'][l, 1], p['ffn_w_down'][l, 1])
    return x, k.reshape(B, S, N_HEADS_A, 2 * HEAD_DIM_A), v, h_last


def setup_inputs(seed: int = 0) -> dict:
    key = jax.random.key(seed)
    ks = jax.random.split(key, 32)
    f32 = jnp.float32
    nrm = lambda kk, shape, s: jax.random.normal(kk, shape, f32) * s
    a0 = jax.random.uniform(ks[24], (DEPTH, 2, D_RG), f32, 0.9, 0.999)
    return {
        'x_prompt': nrm(ks[0], (BATCH, SEQ, D_MODEL), 1.0),
        'x_sample': nrm(ks[1], (DEC_BATCH, DEC_SEQ, D_MODEL), 1.0),
        'cache_attn_k': nrm(ks[2], (DEC_BATCH, DEPTH, PAST_LEN, N_HEADS_A, 2 * HEAD_DIM_A), 1.0),
        'cache_attn_v': nrm(ks[3], (DEC_BATCH, DEPTH, PAST_LEN, N_HEADS_A, V_DIM_A), 1.0),
        'state_rglru': nrm(ks[4], (DEC_BATCH, DEPTH, 2, D_RG), 0.5),
        'c': nrm(ks[5], (DEC_BATCH, D_MODEL), 1.0),
        'c_ctx': nrm(ks[6], (D_MODEL,), 1.0),
        'norm_g': 1.0 + nrm(ks[7], (DEPTH, 3, D_MODEL), 0.01),
        'w_mod': nrm(ks[8], (DEPTH, D_MODEL, N_MOD * D_MODEL), D_MODEL ** -0.5),
        'b_mod': nrm(ks[9], (DEPTH, N_MOD * D_MODEL), 0.01),
        'ffn_w_gate': nrm(ks[10], (DEPTH, 2, D_MODEL, D_FF), D_MODEL ** -0.5),
        'ffn_w_up': nrm(ks[11], (DEPTH, 2, D_MODEL, D_FF), D_MODEL ** -0.5),
        'ffn_w_down': nrm(ks[12], (DEPTH, 2, D_FF, D_MODEL), D_FF ** -0.5),
        'w_in': nrm(ks[13], (DEPTH, D_MODEL, D_IN), D_MODEL ** -0.5),
        'w_out': nrm(ks[14], (DEPTH, D_MIX, D_MODEL), D_MIX ** -0.5),
        'diff_lambda': nrm(ks[15], (DEPTH, 4, HEAD_DIM_A), 0.1),
        'subln_g': 1.0 + nrm(ks[16], (DEPTH, V_DIM_A), 0.01),
        'conv_w': nrm(ks[17], (DEPTH, CONV_W, D_RG), CONV_W ** -0.5),
        'conv_b': nrm(ks[18], (DEPTH, D_RG), 0.01),
        'rg_w_r': nrm(ks[19], (DEPTH, 2, RG_BLOCKS, RG_BLOCK_W, RG_BLOCK_W), RG_BLOCK_W ** -0.5),
        'rg_b_r': nrm(ks[20], (DEPTH, 2, D_RG), 0.01),
        'rg_w_i': nrm(ks[21], (DEPTH, 2, RG_BLOCKS, RG_BLOCK_W, RG_BLOCK_W), RG_BLOCK_W ** -0.5),
        'rg_b_i': nrm(ks[22], (DEPTH, 2, D_RG), 0.01),
        'rg_lambda': jnp.log(a0) - jnp.log1p(-a0),
        'final_g': 1.0 + nrm(ks[23], (D_MODEL,), 0.01),
    }


def reference(x_prompt, x_sample, cache_attn_k, cache_attn_v, state_rglru, c, c_ctx,
              norm_g, w_mod, b_mod, ffn_w_gate, ffn_w_up, ffn_w_down, w_in, w_out,
              diff_lambda, subln_g, conv_w, conv_b, rg_w_r, rg_b_r, rg_w_i, rg_b_i,
              rg_lambda, final_g):
    p = dict(norm_g=norm_g, w_mod=w_mod, b_mod=b_mod, ffn_w_gate=ffn_w_gate, ffn_w_up=ffn_w_up,
             ffn_w_down=ffn_w_down, w_in=w_in, w_out=w_out, diff_lambda=diff_lambda,
             subln_g=subln_g, conv_w=conv_w, conv_b=conv_b, rg_w_r=rg_w_r, rg_b_r=rg_b_r,
             rg_w_i=rg_w_i, rg_b_i=rg_b_i, rg_lambda=rg_lambda)
    xp = x_prompt
    k_list, v_list, h_list = [], [], []
    for l in range(DEPTH):
        xp, k_l, v_l, h_l = _layer(xp, c_ctx, p, l, None)
        k_list.append(k_l)
        v_list.append(v_l)
        h_list.append(h_l)
    y_prompt = _rmsnorm(xp, final_g)
    new_cache_attn_k = jnp.stack(k_list, axis=1)
    new_cache_attn_v = jnp.stack(v_list, axis=1)
    new_state_rglru = jnp.stack(h_list, axis=1)
    xs = x_sample
    for l in range(DEPTH):
        xs, _, _, _ = _layer(xs, c, p, l, (cache_attn_k[:, l], cache_attn_v[:, l], state_rglru[:, l]))
    y_sample = _rmsnorm(xs, final_g)
    return (y_prompt, y_sample, new_cache_attn_k, new_cache_attn_v, new_state_rglru)
```

```python
import functools
import math

import jax
import jax.numpy as jnp
from jax import lax
from jax.experimental import pallas as pl
from jax.experimental.pallas import tpu as pltpu

F32 = jnp.float32
BF16 = jnp.bfloat16

D_MODEL = 1024
N_HEADS = 4
HEAD_DIM = 64
V_DIM = 2 * HEAD_DIM
D_ATTN = N_HEADS * V_DIM
D_RG = 512
RG_BLOCK_W = 64
RG_C = 8.0
CONV_W = 4
CONV_PAD_LEFT = 2
D_FF = 2816
N_MOD = 9
GRID_W = 64
ROPE_BASE = 10000.0
EPS = 1e-6
LAM_INIT = 0.8 - 0.6 * math.exp(-0.3 * 0)

LANES = 128
SUBLANES = 8
MXU_N = 256
FF_CHUNK = MXU_N
N_FF_CHUNKS = D_FF // FF_CHUNK
VMEM_LIMIT = 56 * 1024 * 1024


def _sigmoid(x):
    return 1.0 / (1.0 + jnp.exp(-x))


def _rms(x, g):
    ms = jnp.mean(x * x, axis=-1, keepdims=True)
    return x * lax.rsqrt(ms + EPS) * g


def _mod_kernel(c_ref, w_ref, b_ref, o_ref):
    c = c_ref[...]
    s = (c * _sigmoid(c)).astype(BF16)
    o_ref[...] = jnp.dot(s, w_ref[...].astype(BF16), preferred_element_type=F32) + b_ref[...]


def _mod_call(c8, w_mod, b_mod):
    n = w_mod.shape[1]
    tn = 1536
    return pl.pallas_call(
        _mod_kernel,
        grid=(n // tn,),
        in_specs=[
            pl.BlockSpec((SUBLANES, D_MODEL), lambda j: (0, 0)),
            pl.BlockSpec((D_MODEL, tn), lambda j: (0, j)),
            pl.BlockSpec((1, tn), lambda j: (0, j)),
        ],
        out_specs=pl.BlockSpec((SUBLANES, tn), lambda j: (0, j)),
        out_shape=jax.ShapeDtypeStruct((SUBLANES, n), F32),
        compiler_params=pltpu.CompilerParams(
            dimension_semantics=("arbitrary",), vmem_limit_bytes=VMEM_LIMIT),
        name="mod",
    )(c8, w_mod, b_mod)


def _ffn_kernel(*refs, sub, fuse_mix, final_norm):
    it = iter(refs)
    x_ref = next(it)
    if fuse_mix:
        o_ref_in = next(it)
        rg_ref = next(it)
        wout_ref = next(it)
    mod_ref = next(it)
    ng_ref = next(it)
    wgu_ref = next(it)
    wd_ref = next(it)
    fg_ref = next(it) if final_norm else None
    out_ref = next(it)
    acc_ref = next(it)
    h_ref = next(it)

    x = x_ref[...]
    if fuse_mix:
        mix = jnp.dot(o_ref_in[...], wout_ref[0:D_ATTN, :], preferred_element_type=F32)
        mix = mix + jnp.dot(rg_ref[...], wout_ref[D_ATTN:, :], preferred_element_type=F32)
        x = x + mod_ref[0, 5:6, :] * mix
    sh = mod_ref[0, 3 * sub:3 * sub + 1, :]
    sc = mod_ref[0, 3 * sub + 1:3 * sub + 2, :]
    gate = mod_ref[0, 3 * sub + 2:3 * sub + 3, :]
    h_ref[...] = (_rms(x, ng_ref[sub:sub + 1, :]) * (1.0 + sc) + sh).astype(BF16)
    acc_ref[...] = jnp.zeros_like(acc_ref)

    def body(j, carry):
        gu = jnp.dot(h_ref[...], wgu_ref[j], preferred_element_type=F32)
        g = gu[:, :FF_CHUNK]
        u = gu[:, FF_CHUNK:]
        a = (g * _sigmoid(g) * u).astype(BF16)
        acc_ref[...] += jnp.dot(a, wd_ref[j], preferred_element_type=F32)
        return carry

    lax.fori_loop(0, N_FF_CHUNKS, body, 0)
    y = x + (0.5 * gate) * acc_ref[...]
    if final_norm:
        y = _rms(y, fg_ref[...])
    out_ref[...] = y


def _ffn_call(x, mod3, ng, wgu, wd, *, sub, rows_per_mod, mod_base, tm,
              mix=None, final_g=None):
    t = x.shape[0]
    fuse_mix = mix is not None
    final_norm = final_g is not None
    tiles_per_mod = rows_per_mod // tm

    def row_map(i):
        return (i, 0)

    def mod_map(i):
        return (mod_base + i // tiles_per_mod, 0, 0)

    const2 = lambda i: (0, 0)
    const3 = lambda i: (0, 0, 0)
    in_specs = [pl.BlockSpec((tm, D_MODEL), row_map)]
    args = [x]
    if fuse_mix:
        o, rg, wout = mix
        in_specs += [pl.BlockSpec((tm, D_ATTN), row_map),
                     pl.BlockSpec((tm, D_RG), row_map),
                     pl.BlockSpec(wout.shape, const2, pipeline_mode=pl.Buffered(1))]
        args += [o, rg, wout]
    in_specs += [pl.BlockSpec((1, N_MOD, D_MODEL), mod_map),
                 pl.BlockSpec(ng.shape, const2),
                 pl.BlockSpec(wgu.shape, const3, pipeline_mode=pl.Buffered(1)),
                 pl.BlockSpec(wd.shape, const3, pipeline_mode=pl.Buffered(1))]
    args += [mod3, ng, wgu, wd]
    if final_norm:
        in_specs.append(pl.BlockSpec((1, D_MODEL), const2))
        args.append(final_g)
    return pl.pallas_call(
        functools.partial(_ffn_kernel, sub=sub, fuse_mix=fuse_mix, final_norm=final_norm),
        grid=(t // tm,),
        in_specs=in_specs,
        out_specs=pl.BlockSpec((tm, D_MODEL), row_map),
        out_shape=jax.ShapeDtypeStruct((t, D_MODEL), F32),
        scratch_shapes=[pltpu.VMEM((tm, D_MODEL), F32), pltpu.VMEM((tm, D_MODEL), BF16)],
        compiler_params=pltpu.CompilerParams(
            dimension_semantics=("arbitrary",), vmem_limit_bytes=VMEM_LIMIT),
        name="ffn%d" % sub,
    )(*args)


def _rope(x, cos, sin_signed, first_half):
    outs = []
    for cblk in range(x.shape[1] // LANES):
        xs = x[:, cblk * LANES:(cblk + 1) * LANES]
        partner = jnp.where(first_half, pltpu.roll(xs, LANES - 16, axis=1),
                            pltpu.roll(xs, 16, axis=1))
        outs.append(xs * cos + partner * sin_signed)
    return jnp.concatenate(outs, axis=1)


def _proj_kernel(*refs, rope):
    it = iter(refs)
    x_ref = next(it)
    mod_ref = next(it)
    ng_ref = next(it)
    win_ref = next(it)
    if rope:
        cos_ref = next(it)
        sin_ref = next(it)
    q_ref, k_ref, v_ref, xr_ref, xg_ref = it

    x = x_ref[...]
    sh = mod_ref[0, 3:4, :]
    sc = mod_ref[0, 4:5, :]
    h = (_rms(x, ng_ref[1:2, :]) * (1.0 + sc) + sh).astype(BF16)

    def col(j):
        return jnp.dot(h, win_ref[:, j * D_ATTN:(j + 1) * D_ATTN], preferred_element_type=F32)

    q = col(0)
    k = col(1)
    if rope:
        cos = cos_ref[...]
        sin = sin_ref[...]
        lane = lax.broadcasted_iota(jnp.int32, (1, LANES), 1)
        first_half = (lane % 32) < 16
        q = _rope(q, cos, sin, first_half)
        k = _rope(k, cos, sin, first_half)
    q_ref[...] = (q * (HEAD_DIM ** -0.5)).astype(q_ref.dtype)
    k_ref[...] = k.astype(k_ref.dtype)
    v_ref[...] = col(2).astype(v_ref.dtype)
    xr_ref[...] = col(3)
    xg_ref[...] = col(4)


def _proj_call(x, mod3, ng, win, *, rows_per_mod, mod_base, tm, rope_tabs=None, kv_dtype=F32):
    t = x.shape[0]
    rope = rope_tabs is not None
    tiles_per_mod = rows_per_mod // tm
    row_map = lambda i: (i, 0)
    const2 = lambda i: (0, 0)
    in_specs = [pl.BlockSpec((tm, D_MODEL), row_map),
                pl.BlockSpec((1, N_MOD, D_MODEL), lambda i: (mod_base + i // tiles_per_mod, 0, 0)),
                pl.BlockSpec(ng.shape, const2),
                pl.BlockSpec(win.shape, const2, pipeline_mode=pl.Buffered(1))]
    args = [x, mod3, ng, win]
    if rope:
        cos, sin = rope_tabs
        tiles_per_seq = cos.shape[0] // tm
        tab_map = lambda i: (i % tiles_per_seq, 0)
        in_specs += [pl.BlockSpec((tm, LANES), tab_map), pl.BlockSpec((tm, LANES), tab_map)]
        args += [cos, sin]
    half = pl.BlockSpec((tm, D_ATTN), row_map)
    return pl.pallas_call(
        functools.partial(_proj_kernel, rope=rope),
        grid=(t // tm,),
        in_specs=in_specs,
        out_specs=[half] * 5,
        out_shape=[jax.ShapeDtypeStruct((t, D_ATTN), BF16),
                   jax.ShapeDtypeStruct((t, D_ATTN), kv_dtype),
                   jax.ShapeDtypeStruct((t, D_ATTN), kv_dtype),
                   jax.ShapeDtypeStruct((t, D_RG), F32),
                   jax.ShapeDtypeStruct((t, D_RG), F32)],
        compiler_params=pltpu.CompilerParams(
            dimension_semantics=("arbitrary",), vmem_limit_bytes=VMEM_LIMIT),
        name="proj",
    )(*args)


def _attn_kernel(q_ref, k_ref, v_ref, dl_ref, sg_ref, o_ref, s_scr, p_scr, *, tq, rb):
    dl = dl_ref[...]
    lam = (jnp.exp(jnp.sum(dl[0:1] * dl[1:2], axis=-1, keepdims=True))
           - jnp.exp(jnp.sum(dl[2:3] * dl[3:4], axis=-1, keepdims=True)) + LAM_INIT)
    lane = lax.broadcasted_iota(jnp.int32, (1, V_DIM), 1)
    map0 = lane < HEAD_DIM
    for hd in range(N_HEADS):
        cols = slice(hd * V_DIM, (hd + 1) * V_DIM)
        q = q_ref[0, :, cols]
        zero = jnp.zeros_like(q)
        q2 = jnp.concatenate([jnp.where(map0, q, zero), jnp.where(map0, zero, q)], axis=0)
        k = k_ref[0, :, cols].astype(BF16)
        s_scr[...] = lax.dot_general(q2, k, (((1,), (1,)), ((), ())),
                                     preferred_element_type=F32)

        def softmax_rows(r, carry):
            r0 = pl.multiple_of(r * rb, rb)
            s0 = s_scr[pl.ds(r0, rb), :]
            s1 = s_scr[pl.ds(tq + r0, rb), :]
            e0 = jnp.exp(s0 - jnp.max(s0, axis=-1, keepdims=True))
            e1 = jnp.exp(s1 - jnp.max(s1, axis=-1, keepdims=True))
            w0 = 1.0 / jnp.sum(e0, axis=-1, keepdims=True)
            w1 = lam / jnp.sum(e1, axis=-1, keepdims=True)
            p_scr[pl.ds(r0, rb), :] = (e0 * w0 - e1 * w1).astype(BF16)
            return carry

        lax.fori_loop(0, tq // rb, softmax_rows, 0)
        v = v_ref[0, :, cols].astype(BF16)
        o = jnp.dot(p_scr[...], v, preferred_element_type=F32)
        o = _rms(o, sg_ref[...]) * (1.0 - LAM_INIT)
        o_ref[0, :, cols] = o.astype(o_ref.dtype)


def _attn_call(q, k, v, diff_lambda, subln_g, *, tq):
    b, sq, _ = q.shape
    sk = k.shape[1]
    rb = 16
    return pl.pallas_call(
        functools.partial(_attn_kernel, tq=tq, rb=rb),
        grid=(b, sq // tq),
        in_specs=[pl.BlockSpec((1, tq, D_ATTN), lambda bi, qi: (bi, qi, 0)),
                  pl.BlockSpec((1, sk, D_ATTN), lambda bi, qi: (bi, 0, 0)),
                  pl.BlockSpec((1, sk, D_ATTN), lambda bi, qi: (bi, 0, 0)),
                  pl.BlockSpec(diff_lambda.shape, lambda bi, qi: (0, 0)),
                  pl.BlockSpec(subln_g.shape, lambda bi, qi: (0, 0))],
        out_specs=pl.BlockSpec((1, tq, D_ATTN), lambda bi, qi: (bi, qi, 0)),
        out_shape=jax.ShapeDtypeStruct((b, sq, D_ATTN), BF16),
        scratch_shapes=[pltpu.VMEM((2 * tq, sk), F32), pltpu.VMEM((tq, sk), BF16)],
        compiler_params=pltpu.CompilerParams(
            dimension_semantics=("arbitrary", "arbitrary"), vmem_limit_bytes=VMEM_LIMIT),
        name="attn",
    )(q, k, v, diff_lambda, subln_g)


def _scan8(a, b, reverse):
    sub = lax.broadcasted_iota(jnp.int32, a.shape, 1)
    for d in (1, 2, 4):
        if reverse:
            shift, m = SUBLANES - d, sub < SUBLANES - d
        else:
            shift, m = d, sub >= d
        a_s = pltpu.roll(a, shift, axis=1)
        b_s = pltpu.roll(b, shift, axis=1)
        b = jnp.where(m, a * b_s + b, b)
        a = jnp.where(m, a * a_s, a)
    return a, b


def _scan_block(a, b, carry, reverse):
    tb, l = a.shape
    g = tb // SUBLANES
    a3, b3 = _scan8(a.reshape(g, SUBLANES, l), b.reshape(g, SUBLANES, l), reverse)
    hs = [None] * g
    order = range(g - 1, -1, -1) if reverse else range(g)
    for gi in order:
        hg = a3[gi] * carry + b3[gi]
        hs[gi] = hg
        carry = hg[0:1] if reverse else hg[SUBLANES - 1:SUBLANES]
    return jnp.concatenate(hs, axis=0), carry


def _softplus(x):
    return jnp.maximum(x, 0.0) + jnp.log(1.0 + jnp.exp(-jnp.abs(x)))


def _gelu_tanh(x):
    return 0.5 * x * (1.0 + jnp.tanh(math.sqrt(2.0 / math.pi) * (x + 0.044715 * (x * x * x))))


def _neg_expm1(y):
    small = y > -0.25
    series = -y * (1.0 + y * (0.5 + y * (1.0 / 6.0 + y * (1.0 / 24.0 + y * (1.0 / 120.0
             + y * (1.0 / 720.0 + y * (1.0 / 5040.0 + y * (1.0 / 40320.0))))))))
    return jnp.where(small, series, 1.0 - jnp.exp(y))


def _rglru_kernel(xr_ref, xg_ref, h0_ref, cw_ref, cb_ref, wg_ref, bg_ref, lam_ref,
                  y_ref, hl_ref, xpad, ab, bb, hf, *, s, cc, tb):
    nblk = s // tb
    nsub = cc // LANES
    cbase = pl.program_id(1) * nsub
    for j in range(nsub):
        ls = slice(j * LANES, (j + 1) * LANES)
        xpad[0:SUBLANES, ls] = jnp.zeros((SUBLANES, LANES), F32)
        xpad[s + SUBLANES:s + 2 * SUBLANES, ls] = jnp.zeros((SUBLANES, LANES), F32)
        xpad[SUBLANES:s + SUBLANES, ls] = xr_ref[0, :, ls]
        sp_f = _softplus(-lam_ref[0:1, ls])
        sp_b = _softplus(-lam_ref[1:2, ls])
        wg = wg_ref[cbase + j]
        bg = bg_ref[:, ls]
        cw = cw_ref[:, ls]
        cb = cb_ref[:, ls]

        def gate_ab(pre_r, pre_i, sp, xc):
            r = _sigmoid(pre_r)
            i = _sigmoid(pre_i)
            log_a = (-RG_C) * r * sp
            a = jnp.exp(log_a)
            bx = jnp.sqrt(_neg_expm1(2.0 * log_a)) * (i * xc)
            return a, bx

        def fwd_block(t, carry):
            t0 = pl.multiple_of(t * tb, tb)
            xw = xpad[pl.ds(t0, tb + 2 * SUBLANES), ls]
            xc = cb
            for tap in range(CONV_W):
                lo = SUBLANES - CONV_PAD_LEFT + tap
                xc = xc + xw[lo:lo + tb] * cw[tap:tap + 1]
            pre = jnp.dot(xc.astype(BF16), wg, preferred_element_type=F32)
            a_f, b_f = gate_ab(pre[:, 0:128] + bg[0:1], pre[:, 128:256] + bg[1:2], sp_f, xc)
            a_b, b_b = gate_ab(pre[:, 256:384] + bg[2:3], pre[:, 384:512] + bg[3:4], sp_b, xc)
            ab[pl.ds(t0, tb), ls] = a_b
            bb[pl.ds(t0, tb), ls] = b_b
            h, carry = _scan_block(a_f, b_f, carry, False)
            hf[pl.ds(t0, tb), ls] = h
            return carry

        hl_ref[0, 0:1, ls] = lax.fori_loop(0, nblk, fwd_block, h0_ref[0, 0:1, ls])

        def bwd_block(tt, carry):
            t0 = pl.multiple_of((nblk - 1 - tt) * tb, tb)
            h, carry = _scan_block(ab[pl.ds(t0, tb), ls], bb[pl.ds(t0, tb), ls], carry, True)
            y = (hf[pl.ds(t0, tb), ls] + h) * _gelu_tanh(xg_ref[0, pl.ds(t0, tb), ls])
            y_ref[0, pl.ds(t0, tb), ls] = y.astype(y_ref.dtype)
            return carry

        hl_ref[0, 1:2, ls] = lax.fori_loop(0, nblk, bwd_block, h0_ref[0, 1:2, ls])


def _rglru_call(xr, xg, h0, cw, cb, wg, bg, lam, *, cc):
    b, s, _ = xr.shape
    nch = D_RG // cc
    tb = 256
    seq = pl.BlockSpec((1, s, cc), lambda bi, ci: (bi, 0, ci))
    st = pl.BlockSpec((1, 2, cc), lambda bi, ci: (bi, 0, ci))
    return pl.pallas_call(
        functools.partial(_rglru_kernel, s=s, cc=cc, tb=tb),
        grid=(b, nch),
        in_specs=[seq, seq, st,
                  pl.BlockSpec((CONV_W, cc), lambda bi, ci: (0, ci)),
                  pl.BlockSpec((1, cc), lambda bi, ci: (0, ci)),
                  pl.BlockSpec(wg.shape, lambda bi, ci: (0, 0, 0)),
                  pl.BlockSpec((4, cc), lambda bi, ci: (0, ci)),
                  pl.BlockSpec((2, cc), lambda bi, ci: (0, ci))],
        out_specs=[seq, st],
        out_shape=[jax.ShapeDtypeStruct((b, s, D_RG), BF16),
                   jax.ShapeDtypeStruct((b, 2, D_RG), F32)],
        scratch_shapes=[pltpu.VMEM((s + 2 * SUBLANES, cc), F32),
                        pltpu.VMEM((s, cc), F32), pltpu.VMEM((s, cc), F32),
                        pltpu.VMEM((s, cc), F32)],
        compiler_params=pltpu.CompilerParams(
            dimension_semantics=("arbitrary", "arbitrary"), vmem_limit_bytes=VMEM_LIMIT),
        name="rglru",
    )(xr, xg, h0, cw, cb, wg, bg, lam)


def _rope_tables(s):
    pos = jnp.arange(s)
    row = (pos // GRID_W).astype(F32)
    col = (pos % GRID_W).astype(F32)
    n_freq = HEAD_DIM // 4
    inv_freq = ROPE_BASE ** (-jnp.arange(n_freq, dtype=F32) / n_freq)
    lane = jnp.arange(LANES)
    p = lane % HEAD_DIM
    freq = inv_freq[p % n_freq]
    ang = jnp.where((p < HEAD_DIM // 2)[None, :], row[:, None], col[:, None]) * freq[None, :]
    sign = jnp.where((p % (2 * n_freq)) < n_freq, -1.0, 1.0).astype(F32)
    return jnp.cos(ang), jnp.sin(ang) * sign[None, :]


def _gate_weights(w_r, w_i):
    def bd(w):
        w = w.reshape(4, 2, RG_BLOCK_W, RG_BLOCK_W)
        z = jnp.zeros_like(w[:, 0])
        top = jnp.concatenate([w[:, 0], z], axis=2)
        bot = jnp.concatenate([z, w[:, 1]], axis=2)
        return jnp.concatenate([top, bot], axis=1)
    return jnp.concatenate([bd(w_r[0]), bd(w_i[0]), bd(w_r[1]), bd(w_i[1])], axis=2).astype(BF16)


def kernel(x_prompt, x_sample, cache_attn_k, cache_attn_v, state_rglru, c, c_ctx, norm_g, w_mod, b_mod, ffn_w_gate, ffn_w_up, ffn_w_down, w_in, w_out, diff_lambda, subln_g, conv_w, conv_b, rg_w_r, rg_b_r, rg_w_i, rg_b_i, rg_lambda, final_g):
    l = 0
    bsz, seq, _ = x_prompt.shape
    dbsz, dseq, _ = x_sample.shape
    past = cache_attn_k.shape[2]

    ng = norm_g[l]
    wgu = jnp.concatenate(
        [ffn_w_gate[l].reshape(2, D_MODEL, N_FF_CHUNKS, FF_CHUNK),
         ffn_w_up[l].reshape(2, D_MODEL, N_FF_CHUNKS, FF_CHUNK)], axis=-1)
    wgu = wgu.transpose(0, 2, 1, 3).astype(BF16)
    wd = ffn_w_down[l].reshape(2, N_FF_CHUNKS, FF_CHUNK, D_MODEL).astype(BF16)
    win = w_in[l].astype(BF16)
    wout = w_out[l].astype(BF16)
    wg = _gate_weights(rg_w_r[l], rg_w_i[l])
    bg = jnp.stack([rg_b_r[l, 0], rg_b_i[l, 0], rg_b_r[l, 1], rg_b_i[l, 1]])
    cw = conv_w[l]
    cb = conv_b[l][None, :]
    lam = rg_lambda[l]
    dl = diff_lambda[l]
    sg = subln_g[l][None, :]
    fg = final_g[None, :]

    c8 = jnp.concatenate([c_ctx[None, :], c, jnp.zeros((SUBLANES - 1 - dbsz, D_MODEL), F32)], axis=0)
    mod3 = _mod_call(c8, w_mod[l], b_mod[l][None, :]).reshape(SUBLANES, N_MOD, D_MODEL)

    def layer(x, *, rows_per_mod, mod_base, nb, s, k_ctx, v_ctx, h0, cc, tq, rope_tabs, kv_dtype):
        tm = 512
        x1 = _ffn_call(x, mod3, ng, wgu[0], wd[0], sub=0, rows_per_mod=rows_per_mod,
                       mod_base=mod_base, tm=tm)
        q, k, v, xr, xg = _proj_call(x1, mod3, ng, win, rows_per_mod=rows_per_mod,
                                     mod_base=mod_base, tm=tm, rope_tabs=rope_tabs,
                                     kv_dtype=kv_dtype)
        q3 = q.reshape(nb, s, D_ATTN)
        k3 = k.reshape(nb, s, D_ATTN)
        v3 = v.reshape(nb, s, D_ATTN)
        if k_ctx is not None:
            k3 = jnp.concatenate([k3, k_ctx], axis=1)
            v3 = jnp.concatenate([v3, v_ctx], axis=1)
        o = _attn_call(q3, k3, v3, dl, sg, tq=tq)
        rg, h_last = _rglru_call(xr.reshape(nb, s, D_RG), xg.reshape(nb, s, D_RG), h0,
                                 cw, cb, wg, bg, lam, cc=cc)
        y = _ffn_call(x1, mod3, ng, wgu[1], wd[1], sub=2, rows_per_mod=rows_per_mod,
                      mod_base=mod_base, tm=tm,
                      mix=(o.reshape(nb * s, D_ATTN), rg.reshape(nb * s, D_RG), wout),
                      final_g=fg)
        return y, k, v, h_last

    yp, k_new, v_new, h_new = layer(
        x_prompt.reshape(bsz * seq, D_MODEL), rows_per_mod=bsz * seq, mod_base=0, nb=bsz, s=seq,
        k_ctx=None, v_ctx=None, h0=jnp.zeros((bsz, 2, D_RG), F32), cc=D_RG, tq=seq,
        rope_tabs=None, kv_dtype=F32)
    ys, _, _, _ = layer(
        x_sample.reshape(dbsz * dseq, D_MODEL), rows_per_mod=dseq, mod_base=1, nb=dbsz, s=dseq,
        k_ctx=cache_attn_k[:, l].reshape(dbsz, past, D_ATTN).astype(BF16),
        v_ctx=cache_attn_v[:, l].reshape(dbsz, past, D_ATTN).astype(BF16),
        h0=state_rglru[:, l], cc=LANES, tq=256, rope_tabs=_rope_tables(dseq), kv_dtype=BF16)

    return (yp.reshape(bsz, seq, D_MODEL),
            ys.reshape(dbsz, dseq, D_MODEL),
            k_new.reshape(bsz, 1, seq, N_HEADS, V_DIM),
            v_new.reshape(bsz, 1, seq, N_HEADS, V_DIM),
            h_new.reshape(bsz, 1, 2, D_RG))
```

```python
import functools
import math

import jax
import jax.numpy as jnp
from jax import lax
from jax.experimental import pallas as pl
from jax.experimental.pallas import tpu as pltpu

F32 = jnp.float32
BF16 = jnp.bfloat16

D_MODEL = 1024
N_HEADS = 4
HEAD_DIM = 64
V_DIM = 2 * HEAD_DIM
D_ATTN = N_HEADS * V_DIM
D_RG = 512
RG_BLOCK_W = 64
RG_C = 8.0
CONV_W = 4
CONV_PAD_LEFT = 2
D_FF = 2816
N_MOD = 9
GRID_W = 64
ROPE_BASE = 10000.0
EPS = 1e-6
LAM_INIT = 0.8 - 0.6 * math.exp(-0.3 * 0)
LOG2_E = math.log2(math.e)

LANES = 128
SUBLANES = 8
MXU_N = 256
FF_CHUNK = MXU_N
N_FF_CHUNKS = D_FF // FF_CHUNK
VMEM_LIMIT = 56 * 1024 * 1024


def _sigmoid(x):
    return 1.0 / (1.0 + jnp.exp(-x))


def _rms(x, g):
    ms = jnp.mean(x * x, axis=-1, keepdims=True)
    return x * lax.rsqrt(ms + EPS) * g


def _mod_kernel(c_ref, w_ref, b_ref, o_ref):
    c = c_ref[...]
    s = (c * _sigmoid(c)).astype(BF16)
    o_ref[...] = jnp.dot(s, w_ref[...].astype(BF16), preferred_element_type=F32) + b_ref[...]


def _mod_call(c8, w_mod, b_mod):
    n = w_mod.shape[1]
    tn = 1536
    return pl.pallas_call(
        _mod_kernel,
        grid=(n // tn,),
        in_specs=[
            pl.BlockSpec((SUBLANES, D_MODEL), lambda j: (0, 0)),
            pl.BlockSpec((D_MODEL, tn), lambda j: (0, j)),
            pl.BlockSpec((1, tn), lambda j: (0, j)),
        ],
        out_specs=pl.BlockSpec((SUBLANES, tn), lambda j: (0, j)),
        out_shape=jax.ShapeDtypeStruct((SUBLANES, n), F32),
        compiler_params=pltpu.CompilerParams(
            dimension_semantics=("arbitrary",), vmem_limit_bytes=VMEM_LIMIT),
        name="mod",
    )(c8, w_mod, b_mod)


def _ffn_kernel(*refs, sub, fuse_mix, final_norm):
    it = iter(refs)
    x_ref = next(it)
    if fuse_mix:
        o_ref_in = next(it)
        rg_ref = next(it)
        wout_ref = next(it)
    mod_ref = next(it)
    ng_ref = next(it)
    wgu_ref = next(it)
    wd_ref = next(it)
    fg_ref = next(it) if final_norm else None
    out_ref = next(it)
    acc_ref = next(it)
    h_ref = next(it)

    x = x_ref[...]
    if fuse_mix:
        mix = jnp.dot(o_ref_in[...], wout_ref[0:D_ATTN, :], preferred_element_type=F32)
        mix = mix + jnp.dot(rg_ref[...], wout_ref[D_ATTN:, :], preferred_element_type=F32)
        x = x + mod_ref[0, 5:6, :] * mix
    sh = mod_ref[0, 3 * sub:3 * sub + 1, :]
    sc = mod_ref[0, 3 * sub + 1:3 * sub + 2, :]
    gate = mod_ref[0, 3 * sub + 2:3 * sub + 3, :]
    h_ref[...] = (_rms(x, ng_ref[sub:sub + 1, :]) * (1.0 + sc) + sh).astype(BF16)
    acc_ref[...] = jnp.zeros_like(acc_ref)

    def body(j, carry):
        gu = jnp.dot(h_ref[...], wgu_ref[j], preferred_element_type=F32)
        g = gu[:, :FF_CHUNK]
        u = gu[:, FF_CHUNK:]
        a = (g * _sigmoid(g) * u).astype(BF16)
        acc_ref[...] += jnp.dot(a, wd_ref[j], preferred_element_type=F32)
        return carry

    lax.fori_loop(0, N_FF_CHUNKS, body, 0)
    y = x + (0.5 * gate) * acc_ref[...]
    if final_norm:
        y = _rms(y, fg_ref[...])
    out_ref[...] = y


def _ffn_call(x, mod3, ng, wgu, wd, *, sub, rows_per_mod, mod_base, tm,
              mix=None, final_g=None):
    t = x.shape[0]
    fuse_mix = mix is not None
    final_norm = final_g is not None
    tiles_per_mod = rows_per_mod // tm

    def row_map(i):
        return (i, 0)

    def mod_map(i):
        return (mod_base + i // tiles_per_mod, 0, 0)

    const2 = lambda i: (0, 0)
    const3 = lambda i: (0, 0, 0)
    in_specs = [pl.BlockSpec((tm, D_MODEL), row_map)]
    args = [x]
    if fuse_mix:
        o, rg, wout = mix
        in_specs += [pl.BlockSpec((tm, D_ATTN), row_map),
                     pl.BlockSpec((tm, D_RG), row_map),
                     pl.BlockSpec(wout.shape, const2, pipeline_mode=pl.Buffered(1))]
        args += [o, rg, wout]
    in_specs += [pl.BlockSpec((1, N_MOD, D_MODEL), mod_map),
                 pl.BlockSpec(ng.shape, const2),
                 pl.BlockSpec(wgu.shape, const3, pipeline_mode=pl.Buffered(1)),
                 pl.BlockSpec(wd.shape, const3, pipeline_mode=pl.Buffered(1))]
    args += [mod3, ng, wgu, wd]
    if final_norm:
        in_specs.append(pl.BlockSpec((1, D_MODEL), const2))
        args.append(final_g)
    return pl.pallas_call(
        functools.partial(_ffn_kernel, sub=sub, fuse_mix=fuse_mix, final_norm=final_norm),
        grid=(t // tm,),
        in_specs=in_specs,
        out_specs=pl.BlockSpec((tm, D_MODEL), row_map),
        out_shape=jax.ShapeDtypeStruct((t, D_MODEL), F32),
        scratch_shapes=[pltpu.VMEM((tm, D_MODEL), F32), pltpu.VMEM((tm, D_MODEL), BF16)],
        compiler_params=pltpu.CompilerParams(
            dimension_semantics=("arbitrary",), vmem_limit_bytes=VMEM_LIMIT),
        name="ffn%d" % sub,
    )(*args)


def _rope(x, cos, sin_signed, first_half):
    outs = []
    for cblk in range(x.shape[1] // LANES):
        xs = x[:, cblk * LANES:(cblk + 1) * LANES]
        partner = jnp.where(first_half, pltpu.roll(xs, LANES - 16, axis=1),
                            pltpu.roll(xs, 16, axis=1))
        outs.append(xs * cos + partner * sin_signed)
    return jnp.concatenate(outs, axis=1)


def _proj_kernel(*refs, rope):
    it = iter(refs)
    x_ref = next(it)
    mod_ref = next(it)
    ng_ref = next(it)
    win_ref = next(it)
    if rope:
        cos_ref = next(it)
        sin_ref = next(it)
    q_ref, k_ref, v_ref, xr_ref, xg_ref = it

    x = x_ref[...]
    sh = mod_ref[0, 3:4, :]
    sc = mod_ref[0, 4:5, :]
    h = (_rms(x, ng_ref[1:2, :]) * (1.0 + sc) + sh).astype(BF16)

    def col(j):
        return jnp.dot(h, win_ref[:, j * D_ATTN:(j + 1) * D_ATTN], preferred_element_type=F32)

    q = col(0)
    k = col(1)
    if rope:
        cos = cos_ref[...]
        sin = sin_ref[...]
        lane = lax.broadcasted_iota(jnp.int32, (1, LANES), 1)
        first_half = (lane % 32) < 16
        q = _rope(q, cos, sin, first_half)
        k = _rope(k, cos, sin, first_half)
    q_ref[...] = (q * (HEAD_DIM ** -0.5 * LOG2_E)).astype(q_ref.dtype)
    k_ref[...] = k.astype(k_ref.dtype)
    v_ref[...] = col(2).astype(v_ref.dtype)
    xr_ref[...] = col(3)
    xg_ref[...] = col(4)


def _proj_call(x, mod3, ng, win, *, rows_per_mod, mod_base, tm, rope_tabs=None, kv_dtype=F32):
    t = x.shape[0]
    rope = rope_tabs is not None
    tiles_per_mod = rows_per_mod // tm
    row_map = lambda i: (i, 0)
    const2 = lambda i: (0, 0)
    in_specs = [pl.BlockSpec((tm, D_MODEL), row_map),
                pl.BlockSpec((1, N_MOD, D_MODEL), lambda i: (mod_base + i // tiles_per_mod, 0, 0)),
                pl.BlockSpec(ng.shape, const2),
                pl.BlockSpec(win.shape, const2, pipeline_mode=pl.Buffered(1))]
    args = [x, mod3, ng, win]
    if rope:
        cos, sin = rope_tabs
        tiles_per_seq = cos.shape[0] // tm
        tab_map = lambda i: (i % tiles_per_seq, 0)
        in_specs += [pl.BlockSpec((tm, LANES), tab_map), pl.BlockSpec((tm, LANES), tab_map)]
        args += [cos, sin]
    half = pl.BlockSpec((tm, D_ATTN), row_map)
    return pl.pallas_call(
        functools.partial(_proj_kernel, rope=rope),
        grid=(t // tm,),
        in_specs=in_specs,
        out_specs=[half] * 5,
        out_shape=[jax.ShapeDtypeStruct((t, D_ATTN), BF16),
                   jax.ShapeDtypeStruct((t, D_ATTN), kv_dtype),
                   jax.ShapeDtypeStruct((t, D_ATTN), kv_dtype),
                   jax.ShapeDtypeStruct((t, D_RG), F32),
                   jax.ShapeDtypeStruct((t, D_RG), F32)],
        compiler_params=pltpu.CompilerParams(
            dimension_semantics=("arbitrary",), vmem_limit_bytes=VMEM_LIMIT),
        name="proj",
    )(*args)


def _attn_kernel(q_ref, k_ref, v_ref, dl_ref, sg_ref, o_ref, *scratch, tq, rb):
    dl = dl_ref[...]
    lam = (jnp.exp(jnp.sum(dl[0:1] * dl[1:2], axis=-1, keepdims=True))
           - jnp.exp(jnp.sum(dl[2:3] * dl[3:4], axis=-1, keepdims=True)) + LAM_INIT)
    lane = lax.broadcasted_iota(jnp.int32, (1, V_DIM), 1)
    map0 = lane < HEAD_DIM
    for hd in range(N_HEADS):
        cols = slice(hd * V_DIM, (hd + 1) * V_DIM)
        q = q_ref[0, :, cols]
        zero = jnp.zeros_like(q)
        q2 = jnp.concatenate([jnp.where(map0, q, zero), jnp.where(map0, zero, q)], axis=0)
        k = k_ref[0, :, cols].astype(BF16)
        s = lax.dot_general(q2, k, (((1,), (1,)), ((), ())), preferred_element_type=F32)
        m = jnp.max(s, axis=-1, keepdims=True)
        if rb is None:
            e = jnp.exp2(s - m)
            l = jnp.sum(e, axis=-1, keepdims=True)
            p = e.astype(BF16)
        else:
            s_scr, m_scr, l_scr, p_scr = scratch
            s_scr[...] = s
            m_scr[...] = m

            def exp_rows(r, carry):
                rows = pl.ds(pl.multiple_of(r * rb, rb), rb)
                e = jnp.exp2(s_scr[rows, :] - m_scr[rows, :])
                part = e[:, 0:LANES]
                for cblk in range(1, e.shape[1] // LANES):
                    part = part + e[:, cblk * LANES:(cblk + 1) * LANES]
                l_scr[rows, :] = part
                p_scr[rows, :] = e.astype(BF16)
                return carry

            lax.fori_loop(0, 2 * tq // rb, exp_rows, 0, unroll=2)
            l = jnp.sum(l_scr[...], axis=-1, keepdims=True)
            p = p_scr[...]
        v = v_ref[0, :, cols].astype(BF16)
        o2 = jnp.dot(p, v, preferred_element_type=F32)
        o = o2[:tq] * (1.0 / l[:tq]) - o2[tq:] * (lam / l[tq:])
        o = _rms(o, sg_ref[...]) * (1.0 - LAM_INIT)
        o_ref[0, :, cols] = o.astype(o_ref.dtype)


def _attn_call(q, k, v, diff_lambda, subln_g, *, tq, rb):
    b, sq, _ = q.shape
    sk = k.shape[1]
    scratch = []
    if rb is not None:
        scratch = [pltpu.VMEM((2 * tq, sk), F32), pltpu.VMEM((2 * tq, 1), F32),
                   pltpu.VMEM((2 * tq, LANES), F32), pltpu.VMEM((2 * tq, sk), BF16)]
    return pl.pallas_call(
        functools.partial(_attn_kernel, tq=tq, rb=rb),
        grid=(b, sq // tq),
        in_specs=[pl.BlockSpec((1, tq, D_ATTN), lambda bi, qi: (bi, qi, 0)),
                  pl.BlockSpec((1, sk, D_ATTN), lambda bi, qi: (bi, 0, 0)),
                  pl.BlockSpec((1, sk, D_ATTN), lambda bi, qi: (bi, 0, 0)),
                  pl.BlockSpec(diff_lambda.shape, lambda bi, qi: (0, 0)),
                  pl.BlockSpec(subln_g.shape, lambda bi, qi: (0, 0))],
        out_specs=pl.BlockSpec((1, tq, D_ATTN), lambda bi, qi: (bi, qi, 0)),
        out_shape=jax.ShapeDtypeStruct((b, sq, D_ATTN), BF16),
        scratch_shapes=scratch,
        compiler_params=pltpu.CompilerParams(
            dimension_semantics=("arbitrary", "arbitrary"), vmem_limit_bytes=VMEM_LIMIT),
        name="attn",
    )(q, k, v, diff_lambda, subln_g)


def _scan8(a, b, reverse):
    sub = lax.broadcasted_iota(jnp.int32, a.shape, 1)
    for d in (1, 2, 4):
        if reverse:
            shift, m = SUBLANES - d, sub < SUBLANES - d
        else:
            shift, m = d, sub >= d
        a_s = pltpu.roll(a, shift, axis=1)
        b_s = pltpu.roll(b, shift, axis=1)
        b = jnp.where(m, a * b_s + b, b)
        a = jnp.where(m, a * a_s, a)
    return a, b


def _scan_block(a, b, carry, reverse):
    tb, l = a.shape
    g = tb // SUBLANES
    a3, b3 = _scan8(a.reshape(g, SUBLANES, l), b.reshape(g, SUBLANES, l), reverse)
    hs = [None] * g
    order = range(g - 1, -1, -1) if reverse else range(g)
    for gi in order:
        hg = a3[gi] * carry + b3[gi]
        hs[gi] = hg
        carry = hg[0:1] if reverse else hg[SUBLANES - 1:SUBLANES]
    return jnp.concatenate(hs, axis=0), carry


def _softplus(x):
    return jnp.maximum(x, 0.0) + jnp.log(1.0 + jnp.exp(-jnp.abs(x)))


def _gelu_tanh(x):
    return 0.5 * x * (1.0 + jnp.tanh(math.sqrt(2.0 / math.pi) * (x + 0.044715 * (x * x * x))))


def _neg_expm1(y):
    small = y > -0.25
    series = -y * (1.0 + y * (0.5 + y * (1.0 / 6.0 + y * (1.0 / 24.0 + y * (1.0 / 120.0
             + y * (1.0 / 720.0 + y * (1.0 / 5040.0 + y * (1.0 / 40320.0))))))))
    return jnp.where(small, series, 1.0 - jnp.exp(y))


def _rglru_kernel(xr_ref, xg_ref, h0_ref, cw_ref, cb_ref, wg_ref, bg_ref, lam_ref,
                  y_ref, hl_ref, xpad, ab, bb, hf, *, s, cc, tb):
    nblk = s // tb
    nsub = cc // LANES
    cbase = pl.program_id(1) * nsub
    for j in range(nsub):
        ls = slice(j * LANES, (j + 1) * LANES)
        xpad[0:SUBLANES, ls] = jnp.zeros((SUBLANES, LANES), F32)
        xpad[s + SUBLANES:s + 2 * SUBLANES, ls] = jnp.zeros((SUBLANES, LANES), F32)
        xpad[SUBLANES:s + SUBLANES, ls] = xr_ref[0, :, ls]
        sp_f = _softplus(-lam_ref[0:1, ls])
        sp_b = _softplus(-lam_ref[1:2, ls])
        wg = wg_ref[cbase + j]
        bg = bg_ref[:, ls]
        cw = cw_ref[:, ls]
        cb = cb_ref[:, ls]

        def gate_ab(pre_r, pre_i, sp, xc):
            r = _sigmoid(pre_r)
            i = _sigmoid(pre_i)
            log_a = (-RG_C) * r * sp
            a = jnp.exp(log_a)
            bx = jnp.sqrt(_neg_expm1(2.0 * log_a)) * (i * xc)
            return a, bx

        def fwd_block(t, carry):
            t0 = pl.multiple_of(t * tb, tb)
            xw = xpad[pl.ds(t0, tb + 2 * SUBLANES), ls]
            xc = cb
            for tap in range(CONV_W):
                lo = SUBLANES - CONV_PAD_LEFT + tap
                xc = xc + xw[lo:lo + tb] * cw[tap:tap + 1]
            pre = jnp.dot(xc.astype(BF16), wg, preferred_element_type=F32)
            a_f, b_f = gate_ab(pre[:, 0:128] + bg[0:1], pre[:, 128:256] + bg[1:2], sp_f, xc)
            a_b, b_b = gate_ab(pre[:, 256:384] + bg[2:3], pre[:, 384:512] + bg[3:4], sp_b, xc)
            ab[pl.ds(t0, tb), ls] = a_b
            bb[pl.ds(t0, tb), ls] = b_b
            h, carry = _scan_block(a_f, b_f, carry, False)
            hf[pl.ds(t0, tb), ls] = h
            return carry

        hl_ref[0, 0:1, ls] = lax.fori_loop(0, nblk, fwd_block, h0_ref[0, 0:1, ls])

        def bwd_block(tt, carry):
            t0 = pl.multiple_of((nblk - 1 - tt) * tb, tb)
            h, carry = _scan_block(ab[pl.ds(t0, tb), ls], bb[pl.ds(t0, tb), ls], carry, True)
            y = (hf[pl.ds(t0, tb), ls] + h) * _gelu_tanh(xg_ref[0, pl.ds(t0, tb), ls])
            y_ref[0, pl.ds(t0, tb), ls] = y.astype(y_ref.dtype)
            return carry

        hl_ref[0, 1:2, ls] = lax.fori_loop(0, nblk, bwd_block, h0_ref[0, 1:2, ls])


def _rglru_call(xr, xg, h0, cw, cb, wg, bg, lam, *, cc):
    b, s, _ = xr.shape
    nch = D_RG // cc
    tb = 256
    seq = pl.BlockSpec((1, s, cc), lambda bi, ci: (bi, 0, ci))
    st = pl.BlockSpec((1, 2, cc), lambda bi, ci: (bi, 0, ci))
    return pl.pallas_call(
        functools.partial(_rglru_kernel, s=s, cc=cc, tb=tb),
        grid=(b, nch),
        in_specs=[seq, seq, st,
                  pl.BlockSpec((CONV_W, cc), lambda bi, ci: (0, ci)),
                  pl.BlockSpec((1, cc), lambda bi, ci: (0, ci)),
                  pl.BlockSpec(wg.shape, lambda bi, ci: (0, 0, 0)),
                  pl.BlockSpec((4, cc), lambda bi, ci: (0, ci)),
                  pl.BlockSpec((2, cc), lambda bi, ci: (0, ci))],
        out_specs=[seq, st],
        out_shape=[jax.ShapeDtypeStruct((b, s, D_RG), BF16),
                   jax.ShapeDtypeStruct((b, 2, D_RG), F32)],
        scratch_shapes=[pltpu.VMEM((s + 2 * SUBLANES, cc), F32),
                        pltpu.VMEM((s, cc), F32), pltpu.VMEM((s, cc), F32),
                        pltpu.VMEM((s, cc), F32)],
        compiler_params=pltpu.CompilerParams(
            dimension_semantics=("arbitrary", "arbitrary"), vmem_limit_bytes=VMEM_LIMIT),
        name="rglru",
    )(xr, xg, h0, cw, cb, wg, bg, lam)


def _rope_tables(s):
    pos = jnp.arange(s)
    row = (pos // GRID_W).astype(F32)
    col = (pos % GRID_W).astype(F32)
    n_freq = HEAD_DIM // 4
    inv_freq = ROPE_BASE ** (-jnp.arange(n_freq, dtype=F32) / n_freq)
    lane = jnp.arange(LANES)
    p = lane % HEAD_DIM
    freq = inv_freq[p % n_freq]
    ang = jnp.where((p < HEAD_DIM // 2)[None, :], row[:, None], col[:, None]) * freq[None, :]
    sign = jnp.where((p % (2 * n_freq)) < n_freq, -1.0, 1.0).astype(F32)
    return jnp.cos(ang), jnp.sin(ang) * sign[None, :]


def _gate_weights(w_r, w_i):
    def bd(w):
        w = w.reshape(4, 2, RG_BLOCK_W, RG_BLOCK_W)
        z = jnp.zeros_like(w[:, 0])
        top = jnp.concatenate([w[:, 0], z], axis=2)
        bot = jnp.concatenate([z, w[:, 1]], axis=2)
        return jnp.concatenate([top, bot], axis=1)
    return jnp.concatenate([bd(w_r[0]), bd(w_i[0]), bd(w_r[1]), bd(w_i[1])], axis=2).astype(BF16)


def kernel(x_prompt, x_sample, cache_attn_k, cache_attn_v, state_rglru, c, c_ctx, norm_g, w_mod, b_mod, ffn_w_gate, ffn_w_up, ffn_w_down, w_in, w_out, diff_lambda, subln_g, conv_w, conv_b, rg_w_r, rg_b_r, rg_w_i, rg_b_i, rg_lambda, final_g):
    l = 0
    bsz, seq, _ = x_prompt.shape
    dbsz, dseq, _ = x_sample.shape
    past = cache_attn_k.shape[2]

    ng = norm_g[l]
    wgu = jnp.concatenate(
        [ffn_w_gate[l].reshape(2, D_MODEL, N_FF_CHUNKS, FF_CHUNK),
         ffn_w_up[l].reshape(2, D_MODEL, N_FF_CHUNKS, FF_CHUNK)], axis=-1)
    wgu = wgu.transpose(0, 2, 1, 3).astype(BF16)
    wd = ffn_w_down[l].reshape(2, N_FF_CHUNKS, FF_CHUNK, D_MODEL).astype(BF16)
    win = w_in[l].astype(BF16)
    wout = w_out[l].astype(BF16)
    wg = _gate_weights(rg_w_r[l], rg_w_i[l])
    bg = jnp.stack([rg_b_r[l, 0], rg_b_i[l, 0], rg_b_r[l, 1], rg_b_i[l, 1]])
    cw = conv_w[l]
    cb = conv_b[l][None, :]
    lam = rg_lambda[l]
    dl = diff_lambda[l]
    sg = subln_g[l][None, :]
    fg = final_g[None, :]

    c8 = jnp.concatenate([c_ctx[None, :], c, jnp.zeros((SUBLANES - 1 - dbsz, D_MODEL), F32)], axis=0)
    mod3 = _mod_call(c8, w_mod[l], b_mod[l][None, :]).reshape(SUBLANES, N_MOD, D_MODEL)

    def layer(x, *, rows_per_mod, mod_base, nb, s, k_ctx, v_ctx, h0, cc, tq, rb, rope_tabs, kv_dtype):
        tm = 512
        x1 = _ffn_call(x, mod3, ng, wgu[0], wd[0], sub=0, rows_per_mod=rows_per_mod,
                       mod_base=mod_base, tm=tm)
        q, k, v, xr, xg = _proj_call(x1, mod3, ng, win, rows_per_mod=rows_per_mod,
                                     mod_base=mod_base, tm=tm, rope_tabs=rope_tabs,
                                     kv_dtype=kv_dtype)
        q3 = q.reshape(nb, s, D_ATTN)
        k3 = k.reshape(nb, s, D_ATTN)
        v3 = v.reshape(nb, s, D_ATTN)
        if k_ctx is not None:
            k3 = jnp.concatenate([k3, k_ctx], axis=1)
            v3 = jnp.concatenate([v3, v_ctx], axis=1)
        o = _attn_call(q3, k3, v3, dl, sg, tq=tq, rb=rb)
        rg, h_last = _rglru_call(xr.reshape(nb, s, D_RG), xg.reshape(nb, s, D_RG), h0,
                                 cw, cb, wg, bg, lam, cc=cc)
        y = _ffn_call(x1, mod3, ng, wgu[1], wd[1], sub=2, rows_per_mod=rows_per_mod,
                      mod_base=mod_base, tm=tm,
                      mix=(o.reshape(nb * s, D_ATTN), rg.reshape(nb * s, D_RG), wout),
                      final_g=fg)
        return y, k, v, h_last

    yp, k_new, v_new, h_new = layer(
        x_prompt.reshape(bsz * seq, D_MODEL), rows_per_mod=bsz * seq, mod_base=0, nb=bsz, s=seq,
        k_ctx=None, v_ctx=None, h0=jnp.zeros((bsz, 2, D_RG), F32), cc=D_RG, tq=seq, rb=None,
        rope_tabs=None, kv_dtype=F32)
    ys, _, _, _ = layer(
        x_sample.reshape(dbsz * dseq, D_MODEL), rows_per_mod=dseq, mod_base=1, nb=dbsz, s=dseq,
        k_ctx=cache_attn_k[:, l].reshape(dbsz, past, D_ATTN).astype(BF16),
        v_ctx=cache_attn_v[:, l].reshape(dbsz, past, D_ATTN).astype(BF16),
        h0=state_rglru[:, l], cc=LANES, tq=256, rb=32, rope_tabs=_rope_tables(dseq), kv_dtype=BF16)

    return (yp.reshape(bsz, seq, D_MODEL),
            ys.reshape(dbsz, dseq, D_MODEL),
            k_new.reshape(bsz, 1, seq, N_HEADS, V_DIM),
            v_new.reshape(bsz, 1, seq, N_HEADS, V_DIM),
            h_new.reshape(bsz, 1, 2, D_RG))
```

```python
import functools
import math

import jax
import jax.numpy as jnp
from jax import lax
from jax.experimental import pallas as pl
from jax.experimental.pallas import tpu as pltpu

F32 = jnp.float32
BF16 = jnp.bfloat16

D_MODEL = 1024
N_HEADS = 4
HEAD_DIM = 64
V_DIM = 2 * HEAD_DIM
D_ATTN = N_HEADS * V_DIM
D_RG = 512
RG_BLOCK_W = 64
RG_C = 8.0
CONV_W = 4
CONV_PAD_LEFT = 2
D_FF = 2816
N_MOD = 9
GRID_W = 64
ROPE_BASE = 10000.0
EPS = 1e-6
LAM_INIT = 0.8 - 0.6 * math.exp(-0.3 * 0)
LOG2_E = math.log2(math.e)

LANES = 128
SUBLANES = 8
MXU_N = 256
FF_CHUNK = MXU_N
N_FF_CHUNKS = D_FF // FF_CHUNK
VMEM_LIMIT = 56 * 1024 * 1024


def _sigmoid(x):
    return 1.0 / (1.0 + jnp.exp(-x))


def _rms(x, g):
    ms = jnp.mean(x * x, axis=-1, keepdims=True)
    return x * lax.rsqrt(ms + EPS) * g


def _mod_kernel(c_ref, w_ref, b_ref, o_ref):
    c = c_ref[...]
    s = (c * _sigmoid(c)).astype(BF16)
    o_ref[...] = jnp.dot(s, w_ref[...].astype(BF16), preferred_element_type=F32) + b_ref[...]


def _mod_call(c8, w_mod, b_mod):
    n = w_mod.shape[1]
    tn = 1536
    return pl.pallas_call(
        _mod_kernel,
        grid=(n // tn,),
        in_specs=[
            pl.BlockSpec((SUBLANES, D_MODEL), lambda j: (0, 0)),
            pl.BlockSpec((D_MODEL, tn), lambda j: (0, j)),
            pl.BlockSpec((1, tn), lambda j: (0, j)),
        ],
        out_specs=pl.BlockSpec((SUBLANES, tn), lambda j: (0, j)),
        out_shape=jax.ShapeDtypeStruct((SUBLANES, n), F32),
        compiler_params=pltpu.CompilerParams(
            dimension_semantics=("arbitrary",), vmem_limit_bytes=VMEM_LIMIT),
        name="mod",
    )(c8, w_mod, b_mod)


def _ffn_kernel(*refs, sub, fuse_mix, final_norm):
    it = iter(refs)
    x_ref = next(it)
    if fuse_mix:
        o_ref_in = next(it)
        rg_ref = next(it)
        wout_ref = next(it)
    mod_ref = next(it)
    ng_ref = next(it)
    wgu_ref = next(it)
    wd_ref = next(it)
    fg_ref = next(it) if final_norm else None
    out_ref = next(it)
    acc_ref = next(it)
    h_ref = next(it)
    gu0_ref = next(it)
    gu1_ref = next(it)

    x = x_ref[...]
    if fuse_mix:
        mix = jnp.dot(o_ref_in[...], wout_ref[0:D_ATTN, :], preferred_element_type=F32)
        mix = mix + jnp.dot(rg_ref[...], wout_ref[D_ATTN:, :], preferred_element_type=F32)
        x = x + mod_ref[0, 5:6, :] * mix
    sh = mod_ref[0, 3 * sub:3 * sub + 1, :]
    sc = mod_ref[0, 3 * sub + 1:3 * sub + 2, :]
    gate = mod_ref[0, 3 * sub + 2:3 * sub + 3, :]
    h_ref[...] = (_rms(x, ng_ref[sub:sub + 1, :]) * (1.0 + sc) + sh).astype(BF16)
    def gate_up(j):
        return jnp.dot(h_ref[...], wgu_ref[j], preferred_element_type=F32)

    def down(j, gu):
        g = gu[:, :FF_CHUNK]
        u = gu[:, FF_CHUNK:]
        a = (g * _sigmoid(g) * u).astype(BF16)
        return jnp.dot(a, wd_ref[j], preferred_element_type=F32)

    gu0_ref[...] = gate_up(0)
    acc_ref[...] = jnp.zeros_like(acc_ref)

    def body(i, carry):
        gu1_ref[...] = gate_up(2 * i + 1)
        acc_ref[...] += down(2 * i, gu0_ref[...])
        gu0_ref[...] = gate_up(2 * i + 2)
        acc_ref[...] += down(2 * i + 1, gu1_ref[...])
        return carry

    assert N_FF_CHUNKS % 2 == 1
    lax.fori_loop(0, N_FF_CHUNKS // 2, body, 0)
    y = x + (0.5 * gate) * (acc_ref[...] + down(N_FF_CHUNKS - 1, gu0_ref[...]))
    if final_norm:
        y = _rms(y, fg_ref[...])
    out_ref[...] = y


def _ffn_call(x, mod3, ng, wgu, wd, *, sub, rows_per_mod, mod_base, tm,
              mix=None, final_g=None):
    t = x.shape[0]
    fuse_mix = mix is not None
    final_norm = final_g is not None
    tiles_per_mod = rows_per_mod // tm

    def row_map(i):
        return (i, 0)

    def mod_map(i):
        return (mod_base + i // tiles_per_mod, 0, 0)

    const2 = lambda i: (0, 0)
    const3 = lambda i: (0, 0, 0)
    in_specs = [pl.BlockSpec((tm, D_MODEL), row_map)]
    args = [x]
    if fuse_mix:
        o, rg, wout = mix
        in_specs += [pl.BlockSpec((tm, D_ATTN), row_map),
                     pl.BlockSpec((tm, D_RG), row_map),
                     pl.BlockSpec(wout.shape, const2, pipeline_mode=pl.Buffered(1))]
        args += [o, rg, wout]
    in_specs += [pl.BlockSpec((1, N_MOD, D_MODEL), mod_map),
                 pl.BlockSpec(ng.shape, const2),
                 pl.BlockSpec(wgu.shape, const3, pipeline_mode=pl.Buffered(1)),
                 pl.BlockSpec(wd.shape, const3, pipeline_mode=pl.Buffered(1))]
    args += [mod3, ng, wgu, wd]
    if final_norm:
        in_specs.append(pl.BlockSpec((1, D_MODEL), const2))
        args.append(final_g)
    return pl.pallas_call(
        functools.partial(_ffn_kernel, sub=sub, fuse_mix=fuse_mix, final_norm=final_norm),
        grid=(t // tm,),
        in_specs=in_specs,
        out_specs=pl.BlockSpec((tm, D_MODEL), row_map),
        out_shape=jax.ShapeDtypeStruct((t, D_MODEL), F32),
        scratch_shapes=[pltpu.VMEM((tm, D_MODEL), F32), pltpu.VMEM((tm, D_MODEL), BF16),
                        pltpu.VMEM((tm, 2 * FF_CHUNK), F32), pltpu.VMEM((tm, 2 * FF_CHUNK), F32)],
        compiler_params=pltpu.CompilerParams(
            dimension_semantics=("arbitrary",), vmem_limit_bytes=VMEM_LIMIT),
        name="ffn%d" % sub,
    )(*args)


def _rope(x, cos, sin_signed, first_half):
    outs = []
    for cblk in range(x.shape[1] // LANES):
        xs = x[:, cblk * LANES:(cblk + 1) * LANES]
        partner = jnp.where(first_half, pltpu.roll(xs, LANES - 16, axis=1),
                            pltpu.roll(xs, 16, axis=1))
        outs.append(xs * cos + partner * sin_signed)
    return jnp.concatenate(outs, axis=1)


def _proj_kernel(*refs, rope):
    it = iter(refs)
    x_ref = next(it)
    mod_ref = next(it)
    ng_ref = next(it)
    win_ref = next(it)
    if rope:
        cos_ref = next(it)
        sin_ref = next(it)
    q_ref, k_ref, v_ref, xr_ref, xg_ref = it

    x = x_ref[...]
    sh = mod_ref[0, 3:4, :]
    sc = mod_ref[0, 4:5, :]
    h = (_rms(x, ng_ref[1:2, :]) * (1.0 + sc) + sh).astype(BF16)

    def col(j):
        return jnp.dot(h, win_ref[:, j * D_ATTN:(j + 1) * D_ATTN], preferred_element_type=F32)

    q = col(0)
    k = col(1)
    if rope:
        cos = cos_ref[...]
        sin = sin_ref[...]
        lane = lax.broadcasted_iota(jnp.int32, (1, LANES), 1)
        first_half = (lane % 32) < 16
        q = _rope(q, cos, sin, first_half)
        k = _rope(k, cos, sin, first_half)
    q_ref[...] = (q * (HEAD_DIM ** -0.5 * LOG2_E)).astype(q_ref.dtype)
    k_ref[...] = k.astype(k_ref.dtype)
    v_ref[...] = col(2).astype(v_ref.dtype)
    xr_ref[...] = col(3)
    xg_ref[...] = col(4)


def _proj_call(x, mod3, ng, win, *, rows_per_mod, mod_base, tm, rope_tabs=None, kv_dtype=F32):
    t = x.shape[0]
    rope = rope_tabs is not None
    tiles_per_mod = rows_per_mod // tm
    row_map = lambda i: (i, 0)
    const2 = lambda i: (0, 0)
    in_specs = [pl.BlockSpec((tm, D_MODEL), row_map),
                pl.BlockSpec((1, N_MOD, D_MODEL), lambda i: (mod_base + i // tiles_per_mod, 0, 0)),
                pl.BlockSpec(ng.shape, const2),
                pl.BlockSpec(win.shape, const2, pipeline_mode=pl.Buffered(1))]
    args = [x, mod3, ng, win]
    if rope:
        cos, sin = rope_tabs
        tiles_per_seq = cos.shape[0] // tm
        tab_map = lambda i: (i % tiles_per_seq, 0)
        in_specs += [pl.BlockSpec((tm, LANES), tab_map), pl.BlockSpec((tm, LANES), tab_map)]
        args += [cos, sin]
    half = pl.BlockSpec((tm, D_ATTN), row_map)
    return pl.pallas_call(
        functools.partial(_proj_kernel, rope=rope),
        grid=(t // tm,),
        in_specs=in_specs,
        out_specs=[half] * 5,
        out_shape=[jax.ShapeDtypeStruct((t, D_ATTN), BF16),
                   jax.ShapeDtypeStruct((t, D_ATTN), kv_dtype),
                   jax.ShapeDtypeStruct((t, D_ATTN), kv_dtype),
                   jax.ShapeDtypeStruct((t, D_RG), F32),
                   jax.ShapeDtypeStruct((t, D_RG), F32)],
        compiler_params=pltpu.CompilerParams(
            dimension_semantics=("arbitrary",), vmem_limit_bytes=VMEM_LIMIT),
        name="proj",
    )(*args)


def _attn_kernel(q_ref, k_ref, v_ref, dl_ref, sg_ref, o_ref, *scratch, tq, rb):
    dl = dl_ref[...]
    lam = (jnp.exp(jnp.sum(dl[0:1] * dl[1:2], axis=-1, keepdims=True))
           - jnp.exp(jnp.sum(dl[2:3] * dl[3:4], axis=-1, keepdims=True)) + LAM_INIT)
    lane = lax.broadcasted_iota(jnp.int32, (1, V_DIM), 1)
    map0 = lane < HEAD_DIM
    for hd in range(N_HEADS):
        cols = slice(hd * V_DIM, (hd + 1) * V_DIM)
        q = q_ref[0, :, cols]
        zero = jnp.zeros_like(q)
        q2 = jnp.concatenate([jnp.where(map0, q, zero), jnp.where(map0, zero, q)], axis=0)
        k = k_ref[0, :, cols].astype(BF16)
        s = lax.dot_general(q2, k, (((1,), (1,)), ((), ())), preferred_element_type=F32)
        m = jnp.max(s, axis=-1, keepdims=True)
        if rb is None:
            e = jnp.exp2(s - m)
            l = jnp.sum(e, axis=-1, keepdims=True)
            p = e.astype(BF16)
        else:
            s_scr, m_scr, l_scr, p_scr = scratch
            s_scr[...] = s
            m_scr[...] = m

            def exp_rows(r, carry):
                rows = pl.ds(pl.multiple_of(r * rb, rb), rb)
                e = jnp.exp2(s_scr[rows, :] - m_scr[rows, :])
                part = e[:, 0:LANES]
                for cblk in range(1, e.shape[1] // LANES):
                    part = part + e[:, cblk * LANES:(cblk + 1) * LANES]
                l_scr[rows, :] = part
                p_scr[rows, :] = e.astype(BF16)
                return carry

            lax.fori_loop(0, 2 * tq // rb, exp_rows, 0, unroll=2)
            l = jnp.sum(l_scr[...], axis=-1, keepdims=True)
            p = p_scr[...]
        v = v_ref[0, :, cols].astype(BF16)
        o2 = jnp.dot(p, v, preferred_element_type=F32)
        o = o2[:tq] * (1.0 / l[:tq]) - o2[tq:] * (lam / l[tq:])
        o = _rms(o, sg_ref[...]) * (1.0 - LAM_INIT)
        o_ref[0, :, cols] = o.astype(o_ref.dtype)


def _attn_call(q, k, v, diff_lambda, subln_g, *, tq, rb):
    b, sq, _ = q.shape
    sk = k.shape[1]
    scratch = []
    if rb is not None:
        scratch = [pltpu.VMEM((2 * tq, sk), F32), pltpu.VMEM((2 * tq, 1), F32),
                   pltpu.VMEM((2 * tq, LANES), F32), pltpu.VMEM((2 * tq, sk), BF16)]
    return pl.pallas_call(
        functools.partial(_attn_kernel, tq=tq, rb=rb),
        grid=(b, sq // tq),
        in_specs=[pl.BlockSpec((1, tq, D_ATTN), lambda bi, qi: (bi, qi, 0)),
                  pl.BlockSpec((1, sk, D_ATTN), lambda bi, qi: (bi, 0, 0)),
                  pl.BlockSpec((1, sk, D_ATTN), lambda bi, qi: (bi, 0, 0)),
                  pl.BlockSpec(diff_lambda.shape, lambda bi, qi: (0, 0)),
                  pl.BlockSpec(subln_g.shape, lambda bi, qi: (0, 0))],
        out_specs=pl.BlockSpec((1, tq, D_ATTN), lambda bi, qi: (bi, qi, 0)),
        out_shape=jax.ShapeDtypeStruct((b, sq, D_ATTN), BF16),
        scratch_shapes=scratch,
        compiler_params=pltpu.CompilerParams(
            dimension_semantics=("arbitrary", "arbitrary"), vmem_limit_bytes=VMEM_LIMIT),
        name="attn",
    )(q, k, v, diff_lambda, subln_g)


def _scan8(a, b, reverse):
    sub = lax.broadcasted_iota(jnp.int32, a.shape, 1)
    for d in (1, 2, 4):
        if reverse:
            shift, m = SUBLANES - d, sub < SUBLANES - d
        else:
            shift, m = d, sub >= d
        a_s = pltpu.roll(a, shift, axis=1)
        b_s = pltpu.roll(b, shift, axis=1)
        b = jnp.where(m, a * b_s + b, b)
        a = jnp.where(m, a * a_s, a)
    return a, b


def _scan_block(a, b, carry, reverse):
    tb, l = a.shape
    g = tb // SUBLANES
    a3, b3 = _scan8(a.reshape(g, SUBLANES, l), b.reshape(g, SUBLANES, l), reverse)
    hs = [None] * g
    order = range(g - 1, -1, -1) if reverse else range(g)
    for gi in order:
        hg = a3[gi] * carry + b3[gi]
        hs[gi] = hg
        carry = hg[0:1] if reverse else hg[SUBLANES - 1:SUBLANES]
    return jnp.concatenate(hs, axis=0), carry


def _softplus(x):
    return jnp.maximum(x, 0.0) + jnp.log(1.0 + jnp.exp(-jnp.abs(x)))


def _gelu_tanh(x):
    return 0.5 * x * (1.0 + jnp.tanh(math.sqrt(2.0 / math.pi) * (x + 0.044715 * (x * x * x))))


def _neg_expm1(y):
    small = y > -0.25
    series = -y * (1.0 + y * (0.5 + y * (1.0 / 6.0 + y * (1.0 / 24.0 + y * (1.0 / 120.0
             + y * (1.0 / 720.0 + y * (1.0 / 5040.0 + y * (1.0 / 40320.0))))))))
    return jnp.where(small, series, 1.0 - jnp.exp(y))


def _rglru_kernel(xr_ref, xg_ref, h0_ref, cw_ref, cb_ref, wg_ref, bg_ref, lam_ref,
                  y_ref, hl_ref, xpad, ab, bb, hf, *, s, cc, tb):
    nblk = s // tb
    nsub = cc // LANES
    cbase = pl.program_id(1) * nsub
    for j in range(nsub):
        ls = slice(j * LANES, (j + 1) * LANES)
        xpad[0:SUBLANES, ls] = jnp.zeros((SUBLANES, LANES), F32)
        xpad[s + SUBLANES:s + 2 * SUBLANES, ls] = jnp.zeros((SUBLANES, LANES), F32)
        xpad[SUBLANES:s + SUBLANES, ls] = xr_ref[0, :, ls]
        sp_f = _softplus(-lam_ref[0:1, ls])
        sp_b = _softplus(-lam_ref[1:2, ls])
        wg = wg_ref[cbase + j]
        bg = bg_ref[:, ls]
        cw = cw_ref[:, ls]
        cb = cb_ref[:, ls]

        def gate_ab(pre_r, pre_i, sp, xc):
            r = _sigmoid(pre_r)
            i = _sigmoid(pre_i)
            log_a = (-RG_C) * r * sp
            a = jnp.exp(log_a)
            bx = jnp.sqrt(_neg_expm1(2.0 * log_a)) * (i * xc)
            return a, bx

        def fwd_block(t, carry):
            t0 = pl.multiple_of(t * tb, tb)
            xw = xpad[pl.ds(t0, tb + 2 * SUBLANES), ls]
            xc = cb
            for tap in range(CONV_W):
                lo = SUBLANES - CONV_PAD_LEFT + tap
                xc = xc + xw[lo:lo + tb] * cw[tap:tap + 1]
            pre = jnp.dot(xc.astype(BF16), wg, preferred_element_type=F32)
            a_f, b_f = gate_ab(pre[:, 0:128] + bg[0:1], pre[:, 128:256] + bg[1:2], sp_f, xc)
            a_b, b_b = gate_ab(pre[:, 256:384] + bg[2:3], pre[:, 384:512] + bg[3:4], sp_b, xc)
            ab[pl.ds(t0, tb), ls] = a_b
            bb[pl.ds(t0, tb), ls] = b_b
            h, carry = _scan_block(a_f, b_f, carry, False)
            hf[pl.ds(t0, tb), ls] = h
            return carry

        hl_ref[0, 0:1, ls] = lax.fori_loop(0, nblk, fwd_block, h0_ref[0, 0:1, ls])

        def bwd_block(tt, carry):
            t0 = pl.multiple_of((nblk - 1 - tt) * tb, tb)
            h, carry = _scan_block(ab[pl.ds(t0, tb), ls], bb[pl.ds(t0, tb), ls], carry, True)
            y = (hf[pl.ds(t0, tb), ls] + h) * _gelu_tanh(xg_ref[0, pl.ds(t0, tb), ls])
            y_ref[0, pl.ds(t0, tb), ls] = y.astype(y_ref.dtype)
            return carry

        hl_ref[0, 1:2, ls] = lax.fori_loop(0, nblk, bwd_block, h0_ref[0, 1:2, ls])


def _rglru_call(xr, xg, h0, cw, cb, wg, bg, lam, *, cc):
    b, s, _ = xr.shape
    nch = D_RG // cc
    tb = 256
    seq = pl.BlockSpec((1, s, cc), lambda bi, ci: (bi, 0, ci))
    st = pl.BlockSpec((1, 2, cc), lambda bi, ci: (bi, 0, ci))
    return pl.pallas_call(
        functools.partial(_rglru_kernel, s=s, cc=cc, tb=tb),
        grid=(b, nch),
        in_specs=[seq, seq, st,
                  pl.BlockSpec((CONV_W, cc), lambda bi, ci: (0, ci)),
                  pl.BlockSpec((1, cc), lambda bi, ci: (0, ci)),
                  pl.BlockSpec(wg.shape, lambda bi, ci: (0, 0, 0)),
                  pl.BlockSpec((4, cc), lambda bi, ci: (0, ci)),
                  pl.BlockSpec((2, cc), lambda bi, ci: (0, ci))],
        out_specs=[seq, st],
        out_shape=[jax.ShapeDtypeStruct((b, s, D_RG), BF16),
                   jax.ShapeDtypeStruct((b, 2, D_RG), F32)],
        scratch_shapes=[pltpu.VMEM((s + 2 * SUBLANES, cc), F32),
                        pltpu.VMEM((s, cc), F32), pltpu.VMEM((s, cc), F32),
                        pltpu.VMEM((s, cc), F32)],
        compiler_params=pltpu.CompilerParams(
            dimension_semantics=("arbitrary", "arbitrary"), vmem_limit_bytes=VMEM_LIMIT),
        name="rglru",
    )(xr, xg, h0, cw, cb, wg, bg, lam)


def _rope_tables(s):
    pos = jnp.arange(s)
    row = (pos // GRID_W).astype(F32)
    col = (pos % GRID_W).astype(F32)
    n_freq = HEAD_DIM // 4
    inv_freq = ROPE_BASE ** (-jnp.arange(n_freq, dtype=F32) / n_freq)
    lane = jnp.arange(LANES)
    p = lane % HEAD_DIM
    freq = inv_freq[p % n_freq]
    ang = jnp.where((p < HEAD_DIM // 2)[None, :], row[:, None], col[:, None]) * freq[None, :]
    sign = jnp.where((p % (2 * n_freq)) < n_freq, -1.0, 1.0).astype(F32)
    return jnp.cos(ang), jnp.sin(ang) * sign[None, :]


def _gate_weights(w_r, w_i):
    def bd(w):
        w = w.reshape(4, 2, RG_BLOCK_W, RG_BLOCK_W)
        z = jnp.zeros_like(w[:, 0])
        top = jnp.concatenate([w[:, 0], z], axis=2)
        bot = jnp.concatenate([z, w[:, 1]], axis=2)
        return jnp.concatenate([top, bot], axis=1)
    return jnp.concatenate([bd(w_r[0]), bd(w_i[0]), bd(w_r[1]), bd(w_i[1])], axis=2).astype(BF16)


def kernel(x_prompt, x_sample, cache_attn_k, cache_attn_v, state_rglru, c, c_ctx, norm_g, w_mod, b_mod, ffn_w_gate, ffn_w_up, ffn_w_down, w_in, w_out, diff_lambda, subln_g, conv_w, conv_b, rg_w_r, rg_b_r, rg_w_i, rg_b_i, rg_lambda, final_g):
    l = 0
    bsz, seq, _ = x_prompt.shape
    dbsz, dseq, _ = x_sample.shape
    past = cache_attn_k.shape[2]

    ng = norm_g[l]
    wgu = jnp.concatenate(
        [ffn_w_gate[l].reshape(2, D_MODEL, N_FF_CHUNKS, FF_CHUNK),
         ffn_w_up[l].reshape(2, D_MODEL, N_FF_CHUNKS, FF_CHUNK)], axis=-1)
    wgu = wgu.transpose(0, 2, 1, 3).astype(BF16)
    wd = ffn_w_down[l].reshape(2, N_FF_CHUNKS, FF_CHUNK, D_MODEL).astype(BF16)
    win = w_in[l].astype(BF16)
    wout = w_out[l].astype(BF16)
    wg = _gate_weights(rg_w_r[l], rg_w_i[l])
    bg = jnp.stack([rg_b_r[l, 0], rg_b_i[l, 0], rg_b_r[l, 1], rg_b_i[l, 1]])
    cw = conv_w[l]
    cb = conv_b[l][None, :]
    lam = rg_lambda[l]
    dl = diff_lambda[l]
    sg = subln_g[l][None, :]
    fg = final_g[None, :]

    c8 = jnp.concatenate([c_ctx[None, :], c, jnp.zeros((SUBLANES - 1 - dbsz, D_MODEL), F32)], axis=0)
    mod3 = _mod_call(c8, w_mod[l], b_mod[l][None, :]).reshape(SUBLANES, N_MOD, D_MODEL)

    def layer(x, *, rows_per_mod, mod_base, nb, s, k_ctx, v_ctx, h0, cc, tq, rb, rope_tabs, kv_dtype):
        tm = 512
        tm_ffn = 1024
        x1 = _ffn_call(x, mod3, ng, wgu[0], wd[0], sub=0, rows_per_mod=rows_per_mod,
                       mod_base=mod_base, tm=tm_ffn)
        q, k, v, xr, xg = _proj_call(x1, mod3, ng, win, rows_per_mod=rows_per_mod,
                                     mod_base=mod_base, tm=tm, rope_tabs=rope_tabs,
                                     kv_dtype=kv_dtype)
        q3 = q.reshape(nb, s, D_ATTN)
        k3 = k.reshape(nb, s, D_ATTN)
        v3 = v.reshape(nb, s, D_ATTN)
        if k_ctx is not None:
            k3 = jnp.concatenate([k3, k_ctx], axis=1)
            v3 = jnp.concatenate([v3, v_ctx], axis=1)
        o = _attn_call(q3, k3, v3, dl, sg, tq=tq, rb=rb)
        rg, h_last = _rglru_call(xr.reshape(nb, s, D_RG), xg.reshape(nb, s, D_RG), h0,
                                 cw, cb, wg, bg, lam, cc=cc)
        y = _ffn_call(x1, mod3, ng, wgu[1], wd[1], sub=2, rows_per_mod=rows_per_mod,
                      mod_base=mod_base, tm=tm_ffn,
                      mix=(o.reshape(nb * s, D_ATTN), rg.reshape(nb * s, D_RG), wout),
                      final_g=fg)
        return y, k, v, h_last

    yp, k_new, v_new, h_new = layer(
        x_prompt.reshape(bsz * seq, D_MODEL), rows_per_mod=bsz * seq, mod_base=0, nb=bsz, s=seq,
        k_ctx=None, v_ctx=None, h0=jnp.zeros((bsz, 2, D_RG), F32), cc=D_RG, tq=seq, rb=None,
        rope_tabs=None, kv_dtype=F32)
    ys, _, _, _ = layer(
        x_sample.reshape(dbsz * dseq, D_MODEL), rows_per_mod=dseq, mod_base=1, nb=dbsz, s=dseq,
        k_ctx=cache_attn_k[:, l].reshape(dbsz, past, D_ATTN).astype(BF16),
        v_ctx=cache_attn_v[:, l].reshape(dbsz, past, D_ATTN).astype(BF16),
        h0=state_rglru[:, l], cc=LANES, tq=256, rb=32, rope_tabs=_rope_tables(dseq), kv_dtype=BF16)

    return (yp.reshape(bsz, seq, D_MODEL),
            ys.reshape(dbsz, dseq, D_MODEL),
            k_new.reshape(bsz, 1, seq, N_HEADS, V_DIM),
            v_new.reshape(bsz, 1, seq, N_HEADS, V_DIM),
            h_new.reshape(bsz, 1, 2, D_RG))
```

```python
import functools
import math

import jax
import jax.numpy as jnp
from jax import lax
from jax.experimental import pallas as pl
from jax.experimental.pallas import tpu as pltpu

F32 = jnp.float32
BF16 = jnp.bfloat16

D_MODEL = 1024
N_HEADS = 4
HEAD_DIM = 64
V_DIM = 2 * HEAD_DIM
D_ATTN = N_HEADS * V_DIM
D_RG = 512
RG_BLOCK_W = 64
RG_C = 8.0
CONV_W = 4
CONV_PAD_LEFT = 2
D_FF = 2816
N_MOD = 9
GRID_W = 64
ROPE_BASE = 10000.0
EPS = 1e-6
LAM_INIT = 0.8 - 0.6 * math.exp(-0.3 * 0)
LOG2_E = math.log2(math.e)

LANES = 128
SUBLANES = 8
MXU_N = 256
EXP_ROWS = 64
FF_CHUNK = MXU_N
N_FF_CHUNKS = D_FF // FF_CHUNK
VMEM_LIMIT = 56 * 1024 * 1024


def _sigmoid(x):
    return 1.0 / (1.0 + jnp.exp(-x))


def _rms(x, g):
    ms = jnp.mean(x * x, axis=-1, keepdims=True)
    return x * lax.rsqrt(ms + EPS) * g


def _mod_kernel(c_ref, w_ref, b_ref, o_ref):
    c = c_ref[...]
    s = (c * _sigmoid(c)).astype(BF16)
    o_ref[...] = jnp.dot(s, w_ref[...].astype(BF16), preferred_element_type=F32) + b_ref[...]


def _mod_call(c8, w_mod, b_mod):
    n = w_mod.shape[1]
    tn = 1536
    return pl.pallas_call(
        _mod_kernel,
        grid=(n // tn,),
        in_specs=[
            pl.BlockSpec((SUBLANES, D_MODEL), lambda j: (0, 0)),
            pl.BlockSpec((D_MODEL, tn), lambda j: (0, j)),
            pl.BlockSpec((1, tn), lambda j: (0, j)),
        ],
        out_specs=pl.BlockSpec((SUBLANES, tn), lambda j: (0, j)),
        out_shape=jax.ShapeDtypeStruct((SUBLANES, n), F32),
        compiler_params=pltpu.CompilerParams(
            dimension_semantics=("arbitrary",), vmem_limit_bytes=VMEM_LIMIT),
        name="mod",
    )(c8, w_mod, b_mod)


def _ffn_kernel(*refs, sub, fuse_mix, final_norm):
    it = iter(refs)
    x_ref = next(it)
    if fuse_mix:
        o_ref_in = next(it)
        rg_ref = next(it)
        wout_ref = next(it)
    mod_ref = next(it)
    ng_ref = next(it)
    wgu_ref = next(it)
    wd_ref = next(it)
    fg_ref = next(it) if final_norm else None
    out_ref = next(it)
    acc_ref = next(it)
    h_ref = next(it)
    gu0_ref = next(it)
    gu1_ref = next(it)

    x = x_ref[...]
    if fuse_mix:
        mix = jnp.dot(o_ref_in[...], wout_ref[0:D_ATTN, :], preferred_element_type=F32)
        mix = mix + jnp.dot(rg_ref[...], wout_ref[D_ATTN:, :], preferred_element_type=F32)
        x = x + mod_ref[0, 5:6, :] * mix
    sh = mod_ref[0, 3 * sub:3 * sub + 1, :]
    sc = mod_ref[0, 3 * sub + 1:3 * sub + 2, :]
    gate = mod_ref[0, 3 * sub + 2:3 * sub + 3, :]
    h_ref[...] = (_rms(x, ng_ref[sub:sub + 1, :]) * (1.0 + sc) + sh).astype(BF16)
    def gate_up(j):
        return jnp.dot(h_ref[...], wgu_ref[j], preferred_element_type=F32)

    def down(j, gu):
        g = gu[:, :FF_CHUNK]
        u = gu[:, FF_CHUNK:]
        a = (g * _sigmoid(g) * u).astype(BF16)
        return jnp.dot(a, wd_ref[j], preferred_element_type=F32)

    gu0_ref[...] = gate_up(0)
    acc_ref[...] = jnp.zeros_like(acc_ref)

    def body(i, carry):
        gu1_ref[...] = gate_up(2 * i + 1)
        acc_ref[...] += down(2 * i, gu0_ref[...])
        gu0_ref[...] = gate_up(2 * i + 2)
        acc_ref[...] += down(2 * i + 1, gu1_ref[...])
        return carry

    assert N_FF_CHUNKS % 2 == 1
    lax.fori_loop(0, N_FF_CHUNKS // 2, body, 0)
    y = x + (0.5 * gate) * (acc_ref[...] + down(N_FF_CHUNKS - 1, gu0_ref[...]))
    if final_norm:
        y = _rms(y, fg_ref[...])
    out_ref[...] = y


def _ffn_call(x, mod3, ng, wgu, wd, *, sub, rows_per_mod, mod_base, tm,
              mix=None, final_g=None):
    t = x.shape[0]
    fuse_mix = mix is not None
    final_norm = final_g is not None
    tiles_per_mod = rows_per_mod // tm

    def row_map(i):
        return (i, 0)

    def mod_map(i):
        return (mod_base + i // tiles_per_mod, 0, 0)

    const2 = lambda i: (0, 0)
    const3 = lambda i: (0, 0, 0)
    in_specs = [pl.BlockSpec((tm, D_MODEL), row_map)]
    args = [x]
    if fuse_mix:
        o, rg, wout = mix
        in_specs += [pl.BlockSpec((tm, D_ATTN), row_map),
                     pl.BlockSpec((tm, D_RG), row_map),
                     pl.BlockSpec(wout.shape, const2, pipeline_mode=pl.Buffered(1))]
        args += [o, rg, wout]
    in_specs += [pl.BlockSpec((1, N_MOD, D_MODEL), mod_map),
                 pl.BlockSpec(ng.shape, const2),
                 pl.BlockSpec(wgu.shape, const3, pipeline_mode=pl.Buffered(1)),
                 pl.BlockSpec(wd.shape, const3, pipeline_mode=pl.Buffered(1))]
    args += [mod3, ng, wgu, wd]
    if final_norm:
        in_specs.append(pl.BlockSpec((1, D_MODEL), const2))
        args.append(final_g)
    return pl.pallas_call(
        functools.partial(_ffn_kernel, sub=sub, fuse_mix=fuse_mix, final_norm=final_norm),
        grid=(t // tm,),
        in_specs=in_specs,
        out_specs=pl.BlockSpec((tm, D_MODEL), row_map),
        out_shape=jax.ShapeDtypeStruct((t, D_MODEL), F32),
        scratch_shapes=[pltpu.VMEM((tm, D_MODEL), F32), pltpu.VMEM((tm, D_MODEL), BF16),
                        pltpu.VMEM((tm, 2 * FF_CHUNK), F32), pltpu.VMEM((tm, 2 * FF_CHUNK), F32)],
        compiler_params=pltpu.CompilerParams(
            dimension_semantics=("arbitrary",), vmem_limit_bytes=VMEM_LIMIT),
        name="ffn%d" % sub,
    )(*args)


def _rope(x, cos, sin_signed, first_half):
    outs = []
    for cblk in range(x.shape[1] // LANES):
        xs = x[:, cblk * LANES:(cblk + 1) * LANES]
        partner = jnp.where(first_half, pltpu.roll(xs, LANES - 16, axis=1),
                            pltpu.roll(xs, 16, axis=1))
        outs.append(xs * cos + partner * sin_signed)
    return jnp.concatenate(outs, axis=1)


def _proj_kernel(*refs, rope):
    it = iter(refs)
    x_ref = next(it)
    mod_ref = next(it)
    ng_ref = next(it)
    win_ref = next(it)
    if rope:
        cos_ref = next(it)
        sin_ref = next(it)
    q_ref, k_ref, v_ref, xr_ref, xg_ref = it

    x = x_ref[...]
    sh = mod_ref[0, 3:4, :]
    sc = mod_ref[0, 4:5, :]
    h = (_rms(x, ng_ref[1:2, :]) * (1.0 + sc) + sh).astype(BF16)

    def col(j):
        return jnp.dot(h, win_ref[:, j * D_ATTN:(j + 1) * D_ATTN], preferred_element_type=F32)

    q = col(0)
    k = col(1)
    if rope:
        cos = cos_ref[...]
        sin = sin_ref[...]
        lane = lax.broadcasted_iota(jnp.int32, (1, LANES), 1)
        first_half = (lane % 32) < 16
        q = _rope(q, cos, sin, first_half)
        k = _rope(k, cos, sin, first_half)
    q_ref[...] = (q * (HEAD_DIM ** -0.5 * LOG2_E)).astype(q_ref.dtype)
    k_ref[...] = k.astype(k_ref.dtype)
    v_ref[...] = col(2).astype(v_ref.dtype)
    xr_ref[...] = col(3)
    xg_ref[...] = col(4)


def _proj_call(x, mod3, ng, win, *, rows_per_mod, mod_base, tm, rope_tabs=None, kv_dtype=F32):
    t = x.shape[0]
    rope = rope_tabs is not None
    tiles_per_mod = rows_per_mod // tm
    row_map = lambda i: (i, 0)
    const2 = lambda i: (0, 0)
    in_specs = [pl.BlockSpec((tm, D_MODEL), row_map),
                pl.BlockSpec((1, N_MOD, D_MODEL), lambda i: (mod_base + i // tiles_per_mod, 0, 0)),
                pl.BlockSpec(ng.shape, const2),
                pl.BlockSpec(win.shape, const2, pipeline_mode=pl.Buffered(1))]
    args = [x, mod3, ng, win]
    if rope:
        cos, sin = rope_tabs
        tiles_per_seq = cos.shape[0] // tm
        tab_map = lambda i: (i % tiles_per_seq, 0)
        in_specs += [pl.BlockSpec((tm, LANES), tab_map), pl.BlockSpec((tm, LANES), tab_map)]
        args += [cos, sin]
    half = pl.BlockSpec((tm, D_ATTN), row_map)
    return pl.pallas_call(
        functools.partial(_proj_kernel, rope=rope),
        grid=(t // tm,),
        in_specs=in_specs,
        out_specs=[half] * 5,
        out_shape=[jax.ShapeDtypeStruct((t, D_ATTN), BF16),
                   jax.ShapeDtypeStruct((t, D_ATTN), kv_dtype),
                   jax.ShapeDtypeStruct((t, D_ATTN), kv_dtype),
                   jax.ShapeDtypeStruct((t, D_RG), F32),
                   jax.ShapeDtypeStruct((t, D_RG), F32)],
        compiler_params=pltpu.CompilerParams(
            dimension_semantics=("arbitrary",), vmem_limit_bytes=VMEM_LIMIT),
        name="proj",
    )(*args)


def _diff_lambda(dl_ref):
    dl = dl_ref[...]
    return (jnp.exp(jnp.sum(dl[0:1] * dl[1:2], axis=-1, keepdims=True))
            - jnp.exp(jnp.sum(dl[2:3] * dl[3:4], axis=-1, keepdims=True)) + LAM_INIT)


def _stack_maps(q):
    map0 = lax.broadcasted_iota(jnp.int32, (1, V_DIM), 1) < HEAD_DIM
    zero = jnp.zeros_like(q)
    return jnp.concatenate([jnp.where(map0, q, zero), jnp.where(map0, zero, q)], axis=0)


def _finish_head(o2, l, lam, sg, tq):
    o = o2[:tq] * (1.0 / l[:tq]) - o2[tq:] * (lam / l[tq:])
    return _rms(o, sg) * (1.0 - LAM_INIT)


def _attn_small_kernel(q_ref, k_ref, v_ref, dl_ref, sg_ref, o_ref, *, tq):
    lam = _diff_lambda(dl_ref)
    for hd in range(N_HEADS):
        cols = slice(hd * V_DIM, (hd + 1) * V_DIM)
        q2 = _stack_maps(q_ref[0, :, cols])
        k = k_ref[0, :, cols].astype(BF16)
        s = lax.dot_general(q2, k, (((1,), (1,)), ((), ())), preferred_element_type=F32)
        e = jnp.exp2(s - jnp.max(s, axis=-1, keepdims=True))
        l = jnp.sum(e, axis=-1, keepdims=True)
        o2 = jnp.dot(e.astype(BF16), v_ref[0, :, cols].astype(BF16), preferred_element_type=F32)
        o_ref[0, :, cols] = _finish_head(o2, l, lam, sg_ref[...], tq).astype(o_ref.dtype)


def _attn_pipe_kernel(q_ref, k_ref, v_ref, kc_ref, vc_ref, dl_ref, sg_ref, o_ref,
                      s0, s1, sc0, sc1, p0, p1, pc0, pc1, l0, l1,
                      q2_scr, mv_scr, mb_scr, oacc_scr, *, tq, kt):
    lam = _diff_lambda(dl_ref)
    s_bufs, sc_bufs, p_bufs, pc_bufs, l_bufs = (s0, s1), (sc0, sc1), (p0, p1), (pc0, pc1), (l0, l1)
    n_steps = k_ref.shape[1] // kt
    for t in range(N_HEADS + 2):
        ha, hb, hc = t, t - 1, t - 2
        do_a, do_b, do_c = 0 <= ha < N_HEADS, 0 <= hb < N_HEADS, 0 <= hc < N_HEADS
        cols_a = slice(ha * V_DIM, (ha + 1) * V_DIM)
        cols_c = slice(hc * V_DIM, (hc + 1) * V_DIM)
        if do_a:
            q2_scr[...] = _stack_maps(q_ref[0, :, cols_a])
            mv_scr[...] = jnp.full(mv_scr.shape, -jnp.inf, F32)
        if do_b:
            l_bufs[hb % 2][...] = jnp.zeros(mv_scr.shape, F32)
        if do_c:
            oacc_scr[...] = jnp.zeros(oacc_scr.shape, F32)

        def key_block(k_blk, v_blk, s_a, s_b, p_b, p_c):
            whole = (slice(None), slice(None))
            if do_a:
                s = lax.dot_general(q2_scr[...], k_blk(), (((1,), (1,)), ((), ())),
                                    preferred_element_type=F32)
                s_a[0][s_a[1] + whole] = s
                mx = s[:, 0:LANES]
                for cb in range(1, s.shape[1] // LANES):
                    mx = jnp.maximum(mx, s[:, cb * LANES:(cb + 1) * LANES])
                mv_scr[...] = jnp.maximum(mv_scr[...], mx)
            if do_b:
                s_ref, s_idx = s_b
                p_ref, p_idx = p_b
                l_ref = l_bufs[hb % 2]
                n_keys = s_ref.shape[-1]
                for r0 in range(0, 2 * tq, EXP_ROWS):
                    rows = slice(r0, r0 + EXP_ROWS)
                    mb = mb_scr[rows, :]
                    part = l_ref[rows, :]
                    for c0 in range(0, n_keys, LANES):
                        at = s_idx + (rows, slice(c0, c0 + LANES))
                        e = jnp.exp2(s_ref[at] - mb)
                        part = part + e
                        p_ref[p_idx + (rows, slice(c0, c0 + LANES))] = e.astype(BF16)
                    l_ref[rows, :] = part
            if do_c:
                oacc_scr[...] += jnp.dot(p_c[0][p_c[1] + whole], v_blk(),
                                         preferred_element_type=F32)

        def new_keys_step(j, carry):
            keys = pl.ds(pl.multiple_of(j * kt, kt), kt)
            key_block(lambda: k_ref[0, keys, cols_a], lambda: v_ref[0, keys, cols_c],
                      (s_bufs[ha % 2], (j,)), (s_bufs[hb % 2], (j,)),
                      (p_bufs[hb % 2], (j,)), (p_bufs[hc % 2], (j,)))
            return carry

        lax.fori_loop(0, n_steps, new_keys_step, 0, unroll=True)
        key_block(lambda: kc_ref[0, :, cols_a], lambda: vc_ref[0, :, cols_c],
                  (sc_bufs[ha % 2], ()), (sc_bufs[hb % 2], ()),
                  (pc_bufs[hb % 2], ()), (pc_bufs[hc % 2], ()))
        if do_a:
            m = jnp.max(mv_scr[...], axis=-1, keepdims=True)
            mb_scr[...] = jnp.broadcast_to(m, mb_scr.shape)
        if do_c:
            l = jnp.sum(l_bufs[hc % 2][...], axis=-1, keepdims=True)
            o = _finish_head(oacc_scr[...], l, lam, sg_ref[...], tq)
            o_ref[0, :, cols_c] = o.astype(o_ref.dtype)


def _attn_call(q, k, v, diff_lambda, subln_g, *, tq, cache=None):
    b, sq, _ = q.shape
    sk = k.shape[1]
    qmap = lambda bi, qi: (bi, qi, 0)
    seqmap = lambda bi, qi: (bi, 0, 0)
    const2 = lambda bi, qi: (0, 0)
    kv_mode = pl.Buffered(1) if sq > tq else None
    in_specs = [pl.BlockSpec((1, tq, D_ATTN), qmap),
                pl.BlockSpec((1, sk, D_ATTN), seqmap, pipeline_mode=kv_mode),
                pl.BlockSpec((1, sk, D_ATTN), seqmap, pipeline_mode=kv_mode)]
    args = [q, k, v]
    if cache is not None:
        kc, vc = cache
        past = kc.shape[1]
        kt = 1024
        rows = 2 * tq
        body = functools.partial(_attn_pipe_kernel, tq=tq, kt=kt)
        in_specs += [pl.BlockSpec((1, past, D_ATTN), seqmap)] * 2
        args += [kc, vc]
        scratch = ([pltpu.VMEM((sk // kt, rows, kt), F32)] * 2 + [pltpu.VMEM((rows, past), F32)] * 2
                   + [pltpu.VMEM((sk // kt, rows, kt), BF16)] * 2 + [pltpu.VMEM((rows, past), BF16)] * 2
                   + [pltpu.VMEM((rows, LANES), F32)] * 2 + [pltpu.VMEM((rows, V_DIM), BF16)]
                   + [pltpu.VMEM((rows, LANES), F32)] * 3)
    else:
        body = functools.partial(_attn_small_kernel, tq=tq)
        scratch = []
    in_specs += [pl.BlockSpec(diff_lambda.shape, const2), pl.BlockSpec(subln_g.shape, const2)]
    args += [diff_lambda, subln_g]
    return pl.pallas_call(
        body,
        grid=(b, sq // tq),
        in_specs=in_specs,
        out_specs=pl.BlockSpec((1, tq, D_ATTN), qmap),
        out_shape=jax.ShapeDtypeStruct((b, sq, D_ATTN), BF16),
        scratch_shapes=scratch,
        compiler_params=pltpu.CompilerParams(
            dimension_semantics=("arbitrary", "arbitrary"), vmem_limit_bytes=VMEM_LIMIT),
        name="attn",
    )(*args)


def _scan8(a, b, reverse):
    sub = lax.broadcasted_iota(jnp.int32, a.shape, 1)
    for d in (1, 2, 4):
        if reverse:
            shift, m = SUBLANES - d, sub < SUBLANES - d
        else:
            shift, m = d, sub >= d
        a_s = pltpu.roll(a, shift, axis=1)
        b_s = pltpu.roll(b, shift, axis=1)
        b = jnp.where(m, a * b_s + b, b)
        a = jnp.where(m, a * a_s, a)
    return a, b


def _scan_block(a, b, carry, reverse):
    tb, l = a.shape
    g = tb // SUBLANES
    a3, b3 = _scan8(a.reshape(g, SUBLANES, l), b.reshape(g, SUBLANES, l), reverse)
    hs = [None] * g
    order = range(g - 1, -1, -1) if reverse else range(g)
    for gi in order:
        hg = a3[gi] * carry + b3[gi]
        hs[gi] = hg
        carry = hg[0:1] if reverse else hg[SUBLANES - 1:SUBLANES]
    return jnp.concatenate(hs, axis=0), carry


def _softplus(x):
    return jnp.maximum(x, 0.0) + jnp.log(1.0 + jnp.exp(-jnp.abs(x)))


def _gelu_tanh(x):
    return 0.5 * x * (1.0 + jnp.tanh(math.sqrt(2.0 / math.pi) * (x + 0.044715 * (x * x * x))))


def _neg_expm1(y):
    small = y > -0.25
    series = -y * (1.0 + y * (0.5 + y * (1.0 / 6.0 + y * (1.0 / 24.0 + y * (1.0 / 120.0
             + y * (1.0 / 720.0 + y * (1.0 / 5040.0 + y * (1.0 / 40320.0))))))))
    return jnp.where(small, series, 1.0 - jnp.exp(y))


def _rglru_kernel(xr_ref, xg_ref, h0_ref, cw_ref, cb_ref, wg_ref, bg_ref, lam_ref,
                  y_ref, hl_ref, xpad, ab, bb, hf, *, s, cc, tb):
    nblk = s // tb
    nsub = cc // LANES
    cbase = pl.program_id(1) * nsub
    for j in range(nsub):
        ls = slice(j * LANES, (j + 1) * LANES)
        xpad[0:SUBLANES, ls] = jnp.zeros((SUBLANES, LANES), F32)
        xpad[s + SUBLANES:s + 2 * SUBLANES, ls] = jnp.zeros((SUBLANES, LANES), F32)
        xpad[SUBLANES:s + SUBLANES, ls] = xr_ref[0, :, ls]
        sp_f = _softplus(-lam_ref[0:1, ls])
        sp_b = _softplus(-lam_ref[1:2, ls])
        wg = wg_ref[cbase + j]
        bg = bg_ref[:, ls]
        cw = cw_ref[:, ls]
        cb = cb_ref[:, ls]

        def gate_ab(pre_r, pre_i, sp, xc):
            r = _sigmoid(pre_r)
            i = _sigmoid(pre_i)
            log_a = (-RG_C) * r * sp
            a = jnp.exp(log_a)
            bx = jnp.sqrt(_neg_expm1(2.0 * log_a)) * (i * xc)
            return a, bx

        def fwd_block(t, carry):
            t0 = pl.multiple_of(t * tb, tb)
            xw = xpad[pl.ds(t0, tb + 2 * SUBLANES), ls]
            xc = cb
            for tap in range(CONV_W):
                lo = SUBLANES - CONV_PAD_LEFT + tap
                xc = xc + xw[lo:lo + tb] * cw[tap:tap + 1]
            pre = jnp.dot(xc.astype(BF16), wg, preferred_element_type=F32)
            a_f, b_f = gate_ab(pre[:, 0:128] + bg[0:1], pre[:, 128:256] + bg[1:2], sp_f, xc)
            a_b, b_b = gate_ab(pre[:, 256:384] + bg[2:3], pre[:, 384:512] + bg[3:4], sp_b, xc)
            ab[pl.ds(t0, tb), ls] = a_b
            bb[pl.ds(t0, tb), ls] = b_b
            h, carry = _scan_block(a_f, b_f, carry, False)
            hf[pl.ds(t0, tb), ls] = h
            return carry

        hl_ref[0, 0:1, ls] = lax.fori_loop(0, nblk, fwd_block, h0_ref[0, 0:1, ls])

        def bwd_block(tt, carry):
            t0 = pl.multiple_of((nblk - 1 - tt) * tb, tb)
            h, carry = _scan_block(ab[pl.ds(t0, tb), ls], bb[pl.ds(t0, tb), ls], carry, True)
            y = (hf[pl.ds(t0, tb), ls] + h) * _gelu_tanh(xg_ref[0, pl.ds(t0, tb), ls])
            y_ref[0, pl.ds(t0, tb), ls] = y.astype(y_ref.dtype)
            return carry

        hl_ref[0, 1:2, ls] = lax.fori_loop(0, nblk, bwd_block, h0_ref[0, 1:2, ls])


def _rglru_call(xr, xg, h0, cw, cb, wg, bg, lam, *, cc):
    b, s, _ = xr.shape
    nch = D_RG // cc
    tb = 256
    seq = pl.BlockSpec((1, s, cc), lambda bi, ci: (bi, 0, ci))
    st = pl.BlockSpec((1, 2, cc), lambda bi, ci: (bi, 0, ci))
    return pl.pallas_call(
        functools.partial(_rglru_kernel, s=s, cc=cc, tb=tb),
        grid=(b, nch),
        in_specs=[seq, seq, st,
                  pl.BlockSpec((CONV_W, cc), lambda bi, ci: (0, ci)),
                  pl.BlockSpec((1, cc), lambda bi, ci: (0, ci)),
                  pl.BlockSpec(wg.shape, lambda bi, ci: (0, 0, 0)),
                  pl.BlockSpec((4, cc), lambda bi, ci: (0, ci)),
                  pl.BlockSpec((2, cc), lambda bi, ci: (0, ci))],
        out_specs=[seq, st],
        out_shape=[jax.ShapeDtypeStruct((b, s, D_RG), BF16),
                   jax.ShapeDtypeStruct((b, 2, D_RG), F32)],
        scratch_shapes=[pltpu.VMEM((s + 2 * SUBLANES, cc), F32),
                        pltpu.VMEM((s, cc), F32), pltpu.VMEM((s, cc), F32),
                        pltpu.VMEM((s, cc), F32)],
        compiler_params=pltpu.CompilerParams(
            dimension_semantics=("arbitrary", "arbitrary"), vmem_limit_bytes=VMEM_LIMIT),
        name="rglru",
    )(xr, xg, h0, cw, cb, wg, bg, lam)


def _rope_tables(s):
    pos = jnp.arange(s)
    row = (pos // GRID_W).astype(F32)
    col = (pos % GRID_W).astype(F32)
    n_freq = HEAD_DIM // 4
    inv_freq = ROPE_BASE ** (-jnp.arange(n_freq, dtype=F32) / n_freq)
    lane = jnp.arange(LANES)
    p = lane % HEAD_DIM
    freq = inv_freq[p % n_freq]
    ang = jnp.where((p < HEAD_DIM // 2)[None, :], row[:, None], col[:, None]) * freq[None, :]
    sign = jnp.where((p % (2 * n_freq)) < n_freq, -1.0, 1.0).astype(F32)
    return jnp.cos(ang), jnp.sin(ang) * sign[None, :]


def _gate_weights(w_r, w_i):
    def bd(w):
        w = w.reshape(4, 2, RG_BLOCK_W, RG_BLOCK_W)
        z = jnp.zeros_like(w[:, 0])
        top = jnp.concatenate([w[:, 0], z], axis=2)
        bot = jnp.concatenate([z, w[:, 1]], axis=2)
        return jnp.concatenate([top, bot], axis=1)
    return jnp.concatenate([bd(w_r[0]), bd(w_i[0]), bd(w_r[1]), bd(w_i[1])], axis=2).astype(BF16)


def kernel(x_prompt, x_sample, cache_attn_k, cache_attn_v, state_rglru, c, c_ctx, norm_g, w_mod, b_mod, ffn_w_gate, ffn_w_up, ffn_w_down, w_in, w_out, diff_lambda, subln_g, conv_w, conv_b, rg_w_r, rg_b_r, rg_w_i, rg_b_i, rg_lambda, final_g):
    l = 0
    bsz, seq, _ = x_prompt.shape
    dbsz, dseq, _ = x_sample.shape
    past = cache_attn_k.shape[2]

    ng = norm_g[l]
    wgu = jnp.concatenate(
        [ffn_w_gate[l].reshape(2, D_MODEL, N_FF_CHUNKS, FF_CHUNK),
         ffn_w_up[l].reshape(2, D_MODEL, N_FF_CHUNKS, FF_CHUNK)], axis=-1)
    wgu = wgu.transpose(0, 2, 1, 3).astype(BF16)
    wd = ffn_w_down[l].reshape(2, N_FF_CHUNKS, FF_CHUNK, D_MODEL).astype(BF16)
    win = w_in[l].astype(BF16)
    wout = w_out[l].astype(BF16)
    wg = _gate_weights(rg_w_r[l], rg_w_i[l])
    bg = jnp.stack([rg_b_r[l, 0], rg_b_i[l, 0], rg_b_r[l, 1], rg_b_i[l, 1]])
    cw = conv_w[l]
    cb = conv_b[l][None, :]
    lam = rg_lambda[l]
    dl = diff_lambda[l]
    sg = subln_g[l][None, :]
    fg = final_g[None, :]

    c8 = jnp.concatenate([c_ctx[None, :], c, jnp.zeros((SUBLANES - 1 - dbsz, D_MODEL), F32)], axis=0)
    mod3 = _mod_call(c8, w_mod[l], b_mod[l][None, :]).reshape(SUBLANES, N_MOD, D_MODEL)

    def layer(x, *, rows_per_mod, mod_base, nb, s, k_ctx, v_ctx, h0, cc, tq, rope_tabs, kv_dtype):
        tm = 512
        tm_ffn = 1024
        x1 = _ffn_call(x, mod3, ng, wgu[0], wd[0], sub=0, rows_per_mod=rows_per_mod,
                       mod_base=mod_base, tm=tm_ffn)
        q, k, v, xr, xg = _proj_call(x1, mod3, ng, win, rows_per_mod=rows_per_mod,
                                     mod_base=mod_base, tm=tm, rope_tabs=rope_tabs,
                                     kv_dtype=kv_dtype)
        q3 = q.reshape(nb, s, D_ATTN)
        k3 = k.reshape(nb, s, D_ATTN)
        v3 = v.reshape(nb, s, D_ATTN)
        o = _attn_call(q3, k3, v3, dl, sg, tq=tq,
                       cache=None if k_ctx is None else (k_ctx, v_ctx))
        rg, h_last = _rglru_call(xr.reshape(nb, s, D_RG), xg.reshape(nb, s, D_RG), h0,
                                 cw, cb, wg, bg, lam, cc=cc)
        y = _ffn_call(x1, mod3, ng, wgu[1], wd[1], sub=2, rows_per_mod=rows_per_mod,
                      mod_base=mod_base, tm=tm_ffn,
                      mix=(o.reshape(nb * s, D_ATTN), rg.reshape(nb * s, D_RG), wout),
                      final_g=fg)
        return y, k, v, h_last

    yp, k_new, v_new, h_new = layer(
        x_prompt.reshape(bsz * seq, D_MODEL), rows_per_mod=bsz * seq, mod_base=0, nb=bsz, s=seq,
        k_ctx=None, v_ctx=None, h0=jnp.zeros((bsz, 2, D_RG), F32), cc=D_RG, tq=seq,
        rope_tabs=None, kv_dtype=F32)
    ys, _, _, _ = layer(
        x_sample.reshape(dbsz * dseq, D_MODEL), rows_per_mod=dseq, mod_base=1, nb=dbsz, s=dseq,
        k_ctx=cache_attn_k[:, l].reshape(dbsz, past, D_ATTN).astype(BF16),
        v_ctx=cache_attn_v[:, l].reshape(dbsz, past, D_ATTN).astype(BF16),
        h0=state_rglru[:, l], cc=LANES, tq=256, rope_tabs=_rope_tables(dseq), kv_dtype=BF16)

    return (yp.reshape(bsz, seq, D_MODEL),
            ys.reshape(dbsz, dseq, D_MODEL),
            k_new.reshape(bsz, 1, seq, N_HEADS, V_DIM),
            v_new.reshape(bsz, 1, seq, N_HEADS, V_DIM),
            h_new.reshape(bsz, 1, 2, D_RG))
```

```python
import functools
import math

import jax
import jax.numpy as jnp
import numpy as np
from jax import lax
from jax.experimental import pallas as pl
from jax.experimental.pallas import tpu as pltpu

F32 = jnp.float32
BF16 = jnp.bfloat16

D_MODEL = 1024
N_HEADS = 4
HEAD_DIM = 64
V_DIM = 2 * HEAD_DIM
D_ATTN = N_HEADS * V_DIM
D_RG = 512
RG_BLOCK_W = 64
RG_C = 8.0
CONV_W = 4
CONV_PAD_LEFT = 2
D_FF = 2816
N_MOD = 9
GRID_W = 64
ROPE_BASE = 10000.0
EPS = 1e-6
LAM_INIT = 0.8 - 0.6 * math.exp(-0.3 * 0)
LOG2_E = math.log2(math.e)

LANES = 128
SUBLANES = 8
MXU_N = 256
EXP_ROWS = 64
FF_CHUNK = MXU_N
N_FF_CHUNKS = D_FF // FF_CHUNK
VMEM_LIMIT = 56 * 1024 * 1024


def _sigmoid(x):
    return 1.0 / (1.0 + jnp.exp(-x))


def _rms(x, g):
    ms = jnp.mean(x * x, axis=-1, keepdims=True)
    return x * lax.rsqrt(ms + EPS) * g


def _mod_kernel(c_ref, w_ref, b_ref, o_ref):
    c = c_ref[...]
    s = (c * _sigmoid(c)).astype(BF16)
    o_ref[...] = jnp.dot(s, w_ref[...].astype(BF16), preferred_element_type=F32) + b_ref[...]


def _mod_call(c8, w_mod, b_mod):
    n = w_mod.shape[1]
    tn = 1536
    return pl.pallas_call(
        _mod_kernel,
        grid=(n // tn,),
        in_specs=[
            pl.BlockSpec((SUBLANES, D_MODEL), lambda j: (0, 0)),
            pl.BlockSpec((D_MODEL, tn), lambda j: (0, j)),
            pl.BlockSpec((1, tn), lambda j: (0, j)),
        ],
        out_specs=pl.BlockSpec((SUBLANES, tn), lambda j: (0, j)),
        out_shape=jax.ShapeDtypeStruct((SUBLANES, n), F32),
        compiler_params=pltpu.CompilerParams(
            dimension_semantics=("arbitrary",), vmem_limit_bytes=VMEM_LIMIT),
        name="mod",
    )(c8, w_mod, b_mod)


def _ffn_kernel(*refs, sub, fuse_mix, final_norm):
    it = iter(refs)
    x_ref = next(it)
    if fuse_mix:
        o_ref_in = next(it)
        rg_ref = next(it)
        wout_ref = next(it)
    mod_ref = next(it)
    ng_ref = next(it)
    wg_ref = next(it)
    wu_ref = next(it)
    wd_ref = next(it)
    fg_ref = next(it) if final_norm else None
    out_ref = next(it)
    acc_ref = next(it)
    h_ref = next(it)
    gu0_ref = next(it)
    gu1_ref = next(it)

    x = x_ref[...]
    if fuse_mix:
        mix = jnp.dot(o_ref_in[...], wout_ref[0:D_ATTN, :], preferred_element_type=F32)
        mix = mix + jnp.dot(rg_ref[...], wout_ref[D_ATTN:, :], preferred_element_type=F32)
        x = x + mod_ref[0, 5:6, :] * mix
    sh = mod_ref[0, 3 * sub:3 * sub + 1, :]
    sc = mod_ref[0, 3 * sub + 1:3 * sub + 2, :]
    gate = mod_ref[0, 3 * sub + 2:3 * sub + 3, :]
    h_ref[...] = (_rms(x, ng_ref[sub:sub + 1, :]) * (1.0 + sc) + sh).astype(BF16)

    def chunk(j):
        return pl.ds(pl.multiple_of(j * FF_CHUNK, FF_CHUNK), FF_CHUNK)

    def gate_up(j):
        h = h_ref[...]
        return jnp.concatenate(
            [jnp.dot(h, wg_ref[0, :, chunk(j)], preferred_element_type=F32),
             jnp.dot(h, wu_ref[0, :, chunk(j)], preferred_element_type=F32)], axis=1)

    def down(j, gu):
        g = gu[:, :FF_CHUNK]
        u = gu[:, FF_CHUNK:]
        a = (g * _sigmoid(g) * u).astype(BF16)
        return jnp.dot(a, wd_ref[0, chunk(j), :], preferred_element_type=F32)

    gu0_ref[...] = gate_up(0)
    acc_ref[...] = jnp.zeros_like(acc_ref)

    def body(i, carry):
        gu1_ref[...] = gate_up(2 * i + 1)
        acc_ref[...] += down(2 * i, gu0_ref[...])
        gu0_ref[...] = gate_up(2 * i + 2)
        acc_ref[...] += down(2 * i + 1, gu1_ref[...])
        return carry

    assert N_FF_CHUNKS % 2 == 1
    lax.fori_loop(0, N_FF_CHUNKS // 2, body, 0)
    y = x + (0.5 * gate) * (acc_ref[...] + down(N_FF_CHUNKS - 1, gu0_ref[...]))
    if final_norm:
        y = _rms(y, fg_ref[...])
    out_ref[...] = y


def _ffn_call(x, mod3, ng, wg, wu, wd, *, sub, rows_per_mod, mod_base, tm,
              mix=None, final_g=None):
    t = x.shape[0]
    ffn_idx = sub // 2
    fuse_mix = mix is not None
    final_norm = final_g is not None
    tiles_per_mod = rows_per_mod // tm

    def row_map(i):
        return (i, 0)

    def mod_map(i):
        return (mod_base + i // tiles_per_mod, 0, 0)

    const2 = lambda i: (0, 0)
    ffn_map = lambda i: (ffn_idx, 0, 0)
    in_specs = [pl.BlockSpec((tm, D_MODEL), row_map)]
    args = [x]
    if fuse_mix:
        o, rg, wout = mix
        in_specs += [pl.BlockSpec((tm, D_ATTN), row_map),
                     pl.BlockSpec((tm, D_RG), row_map),
                     pl.BlockSpec(wout.shape, const2, pipeline_mode=pl.Buffered(1))]
        args += [o, rg, wout]
    in_specs += [pl.BlockSpec((1, N_MOD, D_MODEL), mod_map),
                 pl.BlockSpec(ng.shape, const2),
                 pl.BlockSpec((1,) + wg.shape[1:], ffn_map, pipeline_mode=pl.Buffered(1)),
                 pl.BlockSpec((1,) + wu.shape[1:], ffn_map, pipeline_mode=pl.Buffered(1)),
                 pl.BlockSpec((1,) + wd.shape[1:], ffn_map, pipeline_mode=pl.Buffered(1))]
    args += [mod3, ng, wg, wu, wd]
    if final_norm:
        in_specs.append(pl.BlockSpec((1, D_MODEL), const2))
        args.append(final_g)
    return pl.pallas_call(
        functools.partial(_ffn_kernel, sub=sub, fuse_mix=fuse_mix, final_norm=final_norm),
        grid=(t // tm,),
        in_specs=in_specs,
        out_specs=pl.BlockSpec((tm, D_MODEL), row_map),
        out_shape=jax.ShapeDtypeStruct((t, D_MODEL), F32),
        scratch_shapes=[pltpu.VMEM((tm, D_MODEL), F32), pltpu.VMEM((tm, D_MODEL), BF16),
                        pltpu.VMEM((tm, 2 * FF_CHUNK), F32), pltpu.VMEM((tm, 2 * FF_CHUNK), F32)],
        compiler_params=pltpu.CompilerParams(
            dimension_semantics=("arbitrary",), vmem_limit_bytes=VMEM_LIMIT),
        name="ffn%d" % sub,
    )(*args)


def _rope(x, cos, sin_signed, first_half):
    outs = []
    for cblk in range(x.shape[1] // LANES):
        xs = x[:, cblk * LANES:(cblk + 1) * LANES]
        partner = jnp.where(first_half, pltpu.roll(xs, LANES - 16, axis=1),
                            pltpu.roll(xs, 16, axis=1))
        outs.append(xs * cos + partner * sin_signed)
    return jnp.concatenate(outs, axis=1)


def _proj_kernel(*refs, rope):
    it = iter(refs)
    x_ref = next(it)
    mod_ref = next(it)
    ng_ref = next(it)
    win_ref = next(it)
    if rope:
        cos_ref = next(it)
        sin_ref = next(it)
    q_ref, k_ref, v_ref, xr_ref, xg_ref = it

    x = x_ref[...]
    sh = mod_ref[0, 3:4, :]
    sc = mod_ref[0, 4:5, :]
    h = (_rms(x, ng_ref[1:2, :]) * (1.0 + sc) + sh).astype(BF16)

    def col(j):
        return jnp.dot(h, win_ref[:, j * D_ATTN:(j + 1) * D_ATTN], preferred_element_type=F32)

    q = col(0)
    k = col(1)
    if rope:
        cos = cos_ref[...]
        sin = sin_ref[...]
        lane = lax.broadcasted_iota(jnp.int32, (1, LANES), 1)
        first_half = (lane % 32) < 16
        q = _rope(q, cos, sin, first_half)
        k = _rope(k, cos, sin, first_half)
    q_ref[...] = (q * (HEAD_DIM ** -0.5 * LOG2_E)).astype(q_ref.dtype)
    v = col(2)
    if rope:
        k_ref[...] = k.astype(k_ref.dtype)
        v_ref[...] = v.astype(v_ref.dtype)
    else:
        tm = x.shape[0]
        for hd in range(N_HEADS):
            k_ref[pl.ds(hd, tm, stride=N_HEADS), :] = k[:, hd * V_DIM:(hd + 1) * V_DIM]
            v_ref[pl.ds(hd, tm, stride=N_HEADS), :] = v[:, hd * V_DIM:(hd + 1) * V_DIM]
    xr_ref[...] = col(3)
    xg_ref[...] = col(4)


def _proj_call(x, mod3, ng, win, *, rows_per_mod, mod_base, tm, rope_tabs=None):
    t = x.shape[0]
    rope = rope_tabs is not None
    tiles_per_mod = rows_per_mod // tm
    row_map = lambda i: (i, 0)
    const2 = lambda i: (0, 0)
    in_specs = [pl.BlockSpec((tm, D_MODEL), row_map),
                pl.BlockSpec((1, N_MOD, D_MODEL), lambda i: (mod_base + i // tiles_per_mod, 0, 0)),
                pl.BlockSpec(ng.shape, const2),
                pl.BlockSpec(win.shape, const2, pipeline_mode=pl.Buffered(1))]
    args = [x, mod3, ng, win]
    if rope:
        cos, sin = rope_tabs
        tiles_per_seq = cos.shape[0] // tm
        tab_map = lambda i: (i % tiles_per_seq, 0)
        in_specs += [pl.BlockSpec((tm, LANES), tab_map), pl.BlockSpec((tm, LANES), tab_map)]
        args += [cos, sin]
    half = pl.BlockSpec((tm, D_ATTN), row_map)
    if rope:
        kv_spec, kv_shape = half, jax.ShapeDtypeStruct((t, D_ATTN), BF16)
    else:
        kv_spec = pl.BlockSpec((tm * N_HEADS, V_DIM), row_map)
        kv_shape = jax.ShapeDtypeStruct((t * N_HEADS, V_DIM), F32)
    return pl.pallas_call(
        functools.partial(_proj_kernel, rope=rope),
        grid=(t // tm,),
        in_specs=in_specs,
        out_specs=[half, kv_spec, kv_spec, half, half],
        out_shape=[jax.ShapeDtypeStruct((t, D_ATTN), BF16),
                   kv_shape,
                   kv_shape,
                   jax.ShapeDtypeStruct((t, D_RG), F32),
                   jax.ShapeDtypeStruct((t, D_RG), F32)],
        compiler_params=pltpu.CompilerParams(
            dimension_semantics=("arbitrary",), vmem_limit_bytes=VMEM_LIMIT),
        name="proj",
    )(*args)


def _diff_lambda(dl_ref):
    dl = dl_ref[...]
    return (jnp.exp(jnp.sum(dl[0:1] * dl[1:2], axis=-1, keepdims=True))
            - jnp.exp(jnp.sum(dl[2:3] * dl[3:4], axis=-1, keepdims=True)) + LAM_INIT)


def _stack_maps(q):
    map0 = lax.broadcasted_iota(jnp.int32, (1, V_DIM), 1) < HEAD_DIM
    zero = jnp.zeros_like(q)
    return jnp.concatenate([jnp.where(map0, q, zero), jnp.where(map0, zero, q)], axis=0)


def _finish_head(o2, l, lam, sg, tq):
    o = o2[:tq] * (1.0 / l[:tq]) - o2[tq:] * (lam / l[tq:])
    return _rms(o, sg) * (1.0 - LAM_INIT)


def _attn_small_kernel(q_ref, k_ref, v_ref, dl_ref, sg_ref, o_ref, *, tq):
    lam = _diff_lambda(dl_ref)
    sk = k_ref.shape[0] // N_HEADS
    for hd in range(N_HEADS):
        cols = slice(hd * V_DIM, (hd + 1) * V_DIM)
        head_rows = pl.ds(hd, sk, stride=N_HEADS)
        q2 = _stack_maps(q_ref[0, :, cols])
        k = k_ref[head_rows, :].astype(BF16)
        s = lax.dot_general(q2, k, (((1,), (1,)), ((), ())), preferred_element_type=F32)
        e = jnp.exp2(s - jnp.max(s, axis=-1, keepdims=True))
        l = jnp.sum(e, axis=-1, keepdims=True)
        o2 = jnp.dot(e.astype(BF16), v_ref[head_rows, :].astype(BF16), preferred_element_type=F32)
        o_ref[0, :, cols] = _finish_head(o2, l, lam, sg_ref[...], tq).astype(o_ref.dtype)


def _attn_pipe_kernel(q_ref, k_ref, v_ref, kc_ref, vc_ref, dl_ref, sg_ref, o_ref,
                      s0, s1, sc0, sc1, p0, p1, pc0, pc1, l0, l1,
                      q2_scr, mv_scr, mb_scr, oacc_scr, *, tq, kt):
    lam = _diff_lambda(dl_ref)
    s_bufs, sc_bufs, p_bufs, pc_bufs, l_bufs = (s0, s1), (sc0, sc1), (p0, p1), (pc0, pc1), (l0, l1)
    n_steps = k_ref.shape[1] // kt
    for t in range(N_HEADS + 2):
        ha, hb, hc = t, t - 1, t - 2
        do_a, do_b, do_c = 0 <= ha < N_HEADS, 0 <= hb < N_HEADS, 0 <= hc < N_HEADS
        cols_a = slice(ha * V_DIM, (ha + 1) * V_DIM)
        cols_c = slice(hc * V_DIM, (hc + 1) * V_DIM)
        if do_a:
            q2_scr[...] = _stack_maps(q_ref[0, :, cols_a])
            mv_scr[...] = jnp.full(mv_scr.shape, -jnp.inf, F32)
        if do_b:
            l_bufs[hb % 2][...] = jnp.zeros(mv_scr.shape, F32)
        if do_c:
            oacc_scr[...] = jnp.zeros(oacc_scr.shape, F32)

        def key_block(k_blk, v_blk, s_a, s_b, p_b, p_c):
            whole = (slice(None), slice(None))
            if do_a:
                s = lax.dot_general(q2_scr[...], k_blk(), (((1,), (1,)), ((), ())),
                                    preferred_element_type=F32)
                s_a[0][s_a[1] + whole] = s
                mx = s[:, 0:LANES]
                for cb in range(1, s.shape[1] // LANES):
                    mx = jnp.maximum(mx, s[:, cb * LANES:(cb + 1) * LANES])
                mv_scr[...] = jnp.maximum(mv_scr[...], mx)
            if do_b:
                s_ref, s_idx = s_b
                p_ref, p_idx = p_b
                l_ref = l_bufs[hb % 2]
                n_keys = s_ref.shape[-1]
                for r0 in range(0, 2 * tq, EXP_ROWS):
                    rows = slice(r0, r0 + EXP_ROWS)
                    mb = mb_scr[rows, :]
                    part = l_ref[rows, :]
                    for c0 in range(0, n_keys, LANES):
                        at = s_idx + (rows, slice(c0, c0 + LANES))
                        e = jnp.exp2(s_ref[at] - mb)
                        part = part + e
                        p_ref[p_idx + (rows, slice(c0, c0 + LANES))] = e.astype(BF16)
                    l_ref[rows, :] = part
            if do_c:
                oacc_scr[...] += jnp.dot(p_c[0][p_c[1] + whole], v_blk(),
                                         preferred_element_type=F32)

        def new_keys_step(j, carry):
            keys = pl.ds(pl.multiple_of(j * kt, kt), kt)
            key_block(lambda: k_ref[0, keys, cols_a], lambda: v_ref[0, keys, cols_c],
                      (s_bufs[ha % 2], (j,)), (s_bufs[hb % 2], (j,)),
                      (p_bufs[hb % 2], (j,)), (p_bufs[hc % 2], (j,)))
            return carry

        lax.fori_loop(0, n_steps, new_keys_step, 0, unroll=True)
        key_block(lambda: kc_ref[0, :, cols_a], lambda: vc_ref[0, :, cols_c],
                  (sc_bufs[ha % 2], ()), (sc_bufs[hb % 2], ()),
                  (pc_bufs[hb % 2], ()), (pc_bufs[hc % 2], ()))
        if do_a:
            m = jnp.max(mv_scr[...], axis=-1, keepdims=True)
            mb_scr[...] = jnp.broadcast_to(m, mb_scr.shape)
        if do_c:
            l = jnp.sum(l_bufs[hc % 2][...], axis=-1, keepdims=True)
            o = _finish_head(oacc_scr[...], l, lam, sg_ref[...], tq)
            o_ref[0, :, cols_c] = o.astype(o_ref.dtype)


def _attn_call(q, k, v, diff_lambda, subln_g, *, tq, cache=None):
    b, sq, _ = q.shape
    qmap = lambda bi, qi: (bi, qi, 0)
    seqmap = lambda bi, qi: (bi, 0, 0)
    const2 = lambda bi, qi: (0, 0)
    in_specs = [pl.BlockSpec((1, tq, D_ATTN), qmap)]
    args = [q, k, v]
    if cache is not None:
        sk = k.shape[1]
        in_specs += [pl.BlockSpec((1, sk, D_ATTN), seqmap, pipeline_mode=pl.Buffered(1))] * 2
        kc, vc = cache
        past = kc.shape[1]
        kt = 1024
        rows = 2 * tq
        body = functools.partial(_attn_pipe_kernel, tq=tq, kt=kt)
        in_specs += [pl.BlockSpec((1, past, D_ATTN), seqmap)] * 2
        args += [kc, vc]
        scratch = ([pltpu.VMEM((sk // kt, rows, kt), F32)] * 2 + [pltpu.VMEM((rows, past), F32)] * 2
                   + [pltpu.VMEM((sk // kt, rows, kt), BF16)] * 2 + [pltpu.VMEM((rows, past), BF16)] * 2
                   + [pltpu.VMEM((rows, LANES), F32)] * 2 + [pltpu.VMEM((rows, V_DIM), BF16)]
                   + [pltpu.VMEM((rows, LANES), F32)] * 3)
    else:
        assert sq == tq
        body = functools.partial(_attn_small_kernel, tq=tq)
        in_specs += [pl.BlockSpec((k.shape[0] // b, V_DIM), lambda bi, qi: (bi, 0))] * 2
        scratch = []
    in_specs += [pl.BlockSpec(diff_lambda.shape, const2), pl.BlockSpec(subln_g.shape, const2)]
    args += [diff_lambda, subln_g]
    return pl.pallas_call(
        body,
        grid=(b, sq // tq),
        in_specs=in_specs,
        out_specs=pl.BlockSpec((1, tq, D_ATTN), qmap),
        out_shape=jax.ShapeDtypeStruct((b, sq, D_ATTN), BF16),
        scratch_shapes=scratch,
        compiler_params=pltpu.CompilerParams(
            dimension_semantics=("arbitrary", "arbitrary"), vmem_limit_bytes=VMEM_LIMIT),
        name="attn",
    )(*args)


def _scan8(a, b, reverse):
    sub = lax.broadcasted_iota(jnp.int32, a.shape, 1)
    for d in (1, 2, 4):
        if reverse:
            shift, m = SUBLANES - d, sub < SUBLANES - d
        else:
            shift, m = d, sub >= d
        a_s = pltpu.roll(a, shift, axis=1)
        b_s = pltpu.roll(b, shift, axis=1)
        b = jnp.where(m, a * b_s + b, b)
        a = jnp.where(m, a * a_s, a)
    return a, b


def _scan_block(a, b, carry, reverse):
    tb, l = a.shape
    g = tb // SUBLANES
    a3, b3 = _scan8(a.reshape(g, SUBLANES, l), b.reshape(g, SUBLANES, l), reverse)
    hs = [None] * g
    order = range(g - 1, -1, -1) if reverse else range(g)
    for gi in order:
        hg = a3[gi] * carry + b3[gi]
        hs[gi] = hg
        carry = hg[0:1] if reverse else hg[SUBLANES - 1:SUBLANES]
    return jnp.concatenate(hs, axis=0), carry


def _softplus(x):
    return jnp.maximum(x, 0.0) + jnp.log(1.0 + jnp.exp(-jnp.abs(x)))


def _gelu_tanh(x):
    return 0.5 * x * (1.0 + jnp.tanh(math.sqrt(2.0 / math.pi) * (x + 0.044715 * (x * x * x))))


def _neg_expm1(y):
    small = y > -0.25
    series = -y * (1.0 + y * (0.5 + y * (1.0 / 6.0 + y * (1.0 / 24.0 + y * (1.0 / 120.0
             + y * (1.0 / 720.0 + y * (1.0 / 5040.0 + y * (1.0 / 40320.0))))))))
    return jnp.where(small, series, 1.0 - jnp.exp(y))


def _rglru_kernel(xr_ref, xg_ref, h0_ref, cw_ref, cb_ref, wg_ref, bg_ref, lam_ref,
                  y_ref, hl_ref, xpad, ab, bb, hf, *, s, cc, tb):
    nblk = s // tb
    nsub = cc // LANES
    cbase = pl.program_id(1) * nsub
    for j in range(nsub):
        ls = slice(j * LANES, (j + 1) * LANES)
        xpad[0:SUBLANES, ls] = jnp.zeros((SUBLANES, LANES), F32)
        xpad[s + SUBLANES:s + 2 * SUBLANES, ls] = jnp.zeros((SUBLANES, LANES), F32)
        xpad[SUBLANES:s + SUBLANES, ls] = xr_ref[0, :, ls]
        sp_f = _softplus(-lam_ref[0:1, ls])
        sp_b = _softplus(-lam_ref[1:2, ls])
        wg = wg_ref[cbase + j]
        bg = bg_ref[:, ls]
        cw = cw_ref[:, ls]
        cb = cb_ref[:, ls]

        def gate_ab(pre_r, pre_i, sp, xc):
            r = _sigmoid(pre_r)
            i = _sigmoid(pre_i)
            log_a = (-RG_C) * r * sp
            a = jnp.exp(log_a)
            bx = jnp.sqrt(_neg_expm1(2.0 * log_a)) * (i * xc)
            return a, bx

        def fwd_block(t, carry):
            t0 = pl.multiple_of(t * tb, tb)
            xw = xpad[pl.ds(t0, tb + 2 * SUBLANES), ls]
            xc = cb
            for tap in range(CONV_W):
                lo = SUBLANES - CONV_PAD_LEFT + tap
                xc = xc + xw[lo:lo + tb] * cw[tap:tap + 1]
            pre = jnp.dot(xc.astype(BF16), wg, preferred_element_type=F32)
            a_f, b_f = gate_ab(pre[:, 0:128] + bg[0:1], pre[:, 128:256] + bg[1:2], sp_f, xc)
            a_b, b_b = gate_ab(pre[:, 256:384] + bg[2:3], pre[:, 384:512] + bg[3:4], sp_b, xc)
            ab[pl.ds(t0, tb), ls] = a_b
            bb[pl.ds(t0, tb), ls] = b_b
            h, carry = _scan_block(a_f, b_f, carry, False)
            hf[pl.ds(t0, tb), ls] = h
            return carry

        hl_ref[0, 0:1, ls] = lax.fori_loop(0, nblk, fwd_block, h0_ref[0, 0:1, ls])

        def bwd_block(tt, carry):
            t0 = pl.multiple_of((nblk - 1 - tt) * tb, tb)
            h, carry = _scan_block(ab[pl.ds(t0, tb), ls], bb[pl.ds(t0, tb), ls], carry, True)
            y = (hf[pl.ds(t0, tb), ls] + h) * _gelu_tanh(xg_ref[0, pl.ds(t0, tb), ls])
            y_ref[0, pl.ds(t0, tb), ls] = y.astype(y_ref.dtype)
            return carry

        hl_ref[0, 1:2, ls] = lax.fori_loop(0, nblk, bwd_block, h0_ref[0, 1:2, ls])


def _rglru_call(xr, xg, h0, cw, cb, wg, bg, lam, *, cc):
    b, s, _ = xr.shape
    nch = D_RG // cc
    tb = 256
    seq = pl.BlockSpec((1, s, cc), lambda bi, ci: (bi, 0, ci))
    st = pl.BlockSpec((1, 2, cc), lambda bi, ci: (bi, 0, ci))
    return pl.pallas_call(
        functools.partial(_rglru_kernel, s=s, cc=cc, tb=tb),
        grid=(b, nch),
        in_specs=[seq, seq, st,
                  pl.BlockSpec((CONV_W, cc), lambda bi, ci: (0, ci)),
                  pl.BlockSpec((1, cc), lambda bi, ci: (0, ci)),
                  pl.BlockSpec(wg.shape, lambda bi, ci: (0, 0, 0)),
                  pl.BlockSpec((4, cc), lambda bi, ci: (0, ci)),
                  pl.BlockSpec((2, cc), lambda bi, ci: (0, ci))],
        out_specs=[seq, st],
        out_shape=[jax.ShapeDtypeStruct((b, s, D_RG), BF16),
                   jax.ShapeDtypeStruct((b, 2, D_RG), F32)],
        scratch_shapes=[pltpu.VMEM((s + 2 * SUBLANES, cc), F32),
                        pltpu.VMEM((s, cc), F32), pltpu.VMEM((s, cc), F32),
                        pltpu.VMEM((s, cc), F32)],
        compiler_params=pltpu.CompilerParams(
            dimension_semantics=("arbitrary", "arbitrary"), vmem_limit_bytes=VMEM_LIMIT),
        name="rglru",
    )(xr, xg, h0, cw, cb, wg, bg, lam)


def _rope_tables(s):
    pos = np.arange(s)
    row = (pos // GRID_W).astype(np.float32)
    col = (pos % GRID_W).astype(np.float32)
    n_freq = HEAD_DIM // 4
    inv_freq = (ROPE_BASE ** (-np.arange(n_freq, dtype=np.float32) / n_freq)).astype(np.float32)
    p = np.arange(LANES) % HEAD_DIM
    freq = inv_freq[p % n_freq]
    ang = (np.where((p < HEAD_DIM // 2)[None, :], row[:, None], col[:, None])
           * freq[None, :]).astype(np.float32)
    sign = np.where((p % (2 * n_freq)) < n_freq, -1.0, 1.0)
    cos = np.cos(ang.astype(np.float64)).astype(np.float32)
    sin = (np.sin(ang.astype(np.float64)) * sign[None, :]).astype(np.float32)
    return jnp.asarray(cos), jnp.asarray(sin)


def _gate_weights(w_r, w_i):
    def bd(w):
        w = w.reshape(4, 2, RG_BLOCK_W, RG_BLOCK_W)
        z = jnp.zeros_like(w[:, 0])
        top = jnp.concatenate([w[:, 0], z], axis=2)
        bot = jnp.concatenate([z, w[:, 1]], axis=2)
        return jnp.concatenate([top, bot], axis=1)
    return jnp.concatenate([bd(w_r[0]), bd(w_i[0]), bd(w_r[1]), bd(w_i[1])], axis=2).astype(BF16)


def kernel(x_prompt, x_sample, cache_attn_k, cache_attn_v, state_rglru, c, c_ctx, norm_g, w_mod, b_mod, ffn_w_gate, ffn_w_up, ffn_w_down, w_in, w_out, diff_lambda, subln_g, conv_w, conv_b, rg_w_r, rg_b_r, rg_w_i, rg_b_i, rg_lambda, final_g):
    l = 0
    bsz, seq, _ = x_prompt.shape
    dbsz, dseq, _ = x_sample.shape
    past = cache_attn_k.shape[2]

    ng = norm_g[l]
    wgate = ffn_w_gate[l].astype(BF16)
    wup = ffn_w_up[l].astype(BF16)
    wd = ffn_w_down[l].astype(BF16)
    win = w_in[l].astype(BF16)
    wout = w_out[l].astype(BF16)
    wg = _gate_weights(rg_w_r[l], rg_w_i[l])
    bg = jnp.stack([rg_b_r[l, 0], rg_b_i[l, 0], rg_b_r[l, 1], rg_b_i[l, 1]])
    cw = conv_w[l]
    cb = conv_b[l][None, :]
    lam = rg_lambda[l]
    dl = diff_lambda[l]
    sg = subln_g[l][None, :]
    fg = final_g[None, :]

    c8 = jnp.concatenate([c_ctx[None, :], c, jnp.zeros((SUBLANES - 1 - dbsz, D_MODEL), F32)], axis=0)
    mod3 = _mod_call(c8, w_mod[l], b_mod[l][None, :]).reshape(SUBLANES, N_MOD, D_MODEL)

    def layer(x, *, rows_per_mod, mod_base, nb, s, k_ctx, v_ctx, h0, cc, tq, rope_tabs):
        tm = 512
        tm_ffn = 1024
        x1 = _ffn_call(x, mod3, ng, wgate, wup, wd, sub=0, rows_per_mod=rows_per_mod,
                       mod_base=mod_base, tm=tm_ffn)
        q, k, v, xr, xg = _proj_call(x1, mod3, ng, win, rows_per_mod=rows_per_mod,
                                     mod_base=mod_base, tm=tm, rope_tabs=rope_tabs)
        q3 = q.reshape(nb, s, D_ATTN)
        if k_ctx is None:
            o = _attn_call(q3, k, v, dl, sg, tq=tq)
        else:
            o = _attn_call(q3, k.reshape(nb, s, D_ATTN), v.reshape(nb, s, D_ATTN), dl, sg, tq=tq,
                           cache=(k_ctx, v_ctx))
        rg, h_last = _rglru_call(xr.reshape(nb, s, D_RG), xg.reshape(nb, s, D_RG), h0,
                                 cw, cb, wg, bg, lam, cc=cc)
        y = _ffn_call(x1, mod3, ng, wgate, wup, wd, sub=2, rows_per_mod=rows_per_mod,
                      mod_base=mod_base, tm=tm_ffn,
                      mix=(o.reshape(nb * s, D_ATTN), rg.reshape(nb * s, D_RG), wout),
                      final_g=fg)
        return y, k, v, h_last

    yp, k_new, v_new, h_new = layer(
        x_prompt.reshape(bsz * seq, D_MODEL), rows_per_mod=bsz * seq, mod_base=0, nb=bsz, s=seq,
        k_ctx=None, v_ctx=None, h0=jnp.zeros((bsz, 2, D_RG), F32), cc=D_RG, tq=seq,
        rope_tabs=None)
    ys, _, _, _ = layer(
        x_sample.reshape(dbsz * dseq, D_MODEL), rows_per_mod=dseq, mod_base=1, nb=dbsz, s=dseq,
        k_ctx=cache_attn_k[:, l].reshape(dbsz, past, D_ATTN).astype(BF16),
        v_ctx=cache_attn_v[:, l].reshape(dbsz, past, D_ATTN).astype(BF16),
        h0=state_rglru[:, l], cc=LANES, tq=256, rope_tabs=_rope_tables(dseq))

    return (yp.reshape(bsz, seq, D_MODEL),
            ys.reshape(dbsz, dseq, D_MODEL),
            k_new.reshape(bsz, 1, seq, N_HEADS, V_DIM),
            v_new.reshape(bsz, 1, seq, N_HEADS, V_DIM),
            h_new.reshape(bsz, 1, 2, D_RG))
```

```python
import functools
import math

import jax
import jax.numpy as jnp
import numpy as np
from jax import lax
from jax.experimental import pallas as pl
from jax.experimental.pallas import tpu as pltpu

F32 = jnp.float32
BF16 = jnp.bfloat16

D_MODEL = 1024
N_HEADS = 4
HEAD_DIM = 64
V_DIM = 2 * HEAD_DIM
D_ATTN = N_HEADS * V_DIM
D_RG = 512
RG_BLOCK_W = 64
RG_C = 8.0
CONV_W = 4
CONV_PAD_LEFT = 2
D_FF = 2816
N_MOD = 9
GRID_W = 64
ROPE_BASE = 10000.0
EPS = 1e-6
LAM_INIT = 0.8 - 0.6 * math.exp(-0.3 * 0)
LOG2_E = math.log2(math.e)

LANES = 128
SUBLANES = 8
MXU_N = 256
SEQS_PER_STEP = 2
EXP_ROWS = 64
FF_CHUNK = MXU_N
N_FF_CHUNKS = D_FF // FF_CHUNK
VMEM_LIMIT = 56 * 1024 * 1024


def _sigmoid(x):
    return 1.0 / (1.0 + jnp.exp(-x))


def _rms(x, g):
    ms = jnp.mean(x * x, axis=-1, keepdims=True)
    return x * lax.rsqrt(ms + EPS) * g


def _mod_kernel(c_ref, w_ref, b_ref, o_ref):
    c = c_ref[...]
    s = (c * _sigmoid(c)).astype(BF16)
    o_ref[...] = jnp.dot(s, w_ref[...].astype(BF16), preferred_element_type=F32) + b_ref[...]


def _mod_call(c8, w_mod, b_mod):
    n = w_mod.shape[1]
    tn = 1536
    return pl.pallas_call(
        _mod_kernel,
        grid=(n // tn,),
        in_specs=[
            pl.BlockSpec((SUBLANES, D_MODEL), lambda j: (0, 0)),
            pl.BlockSpec((D_MODEL, tn), lambda j: (0, j)),
            pl.BlockSpec((1, tn), lambda j: (0, j)),
        ],
        out_specs=pl.BlockSpec((SUBLANES, tn), lambda j: (0, j)),
        out_shape=jax.ShapeDtypeStruct((SUBLANES, n), F32),
        compiler_params=pltpu.CompilerParams(
            dimension_semantics=("arbitrary",), vmem_limit_bytes=VMEM_LIMIT),
        name="mod",
    )(c8, w_mod, b_mod)


def _ffn_kernel(*refs, sub, fuse_mix, final_norm):
    it = iter(refs)
    x_ref = next(it)
    if fuse_mix:
        o_ref_in = next(it)
        rg_ref = next(it)
        wout_ref = next(it)
    mod_ref = next(it)
    ng_ref = next(it)
    wg_ref = next(it)
    wu_ref = next(it)
    wd_ref = next(it)
    fg_ref = next(it) if final_norm else None
    out_ref = next(it)
    acc_ref = next(it)
    h_ref = next(it)
    gu0_ref = next(it)
    gu1_ref = next(it)

    x = x_ref[...]
    if fuse_mix:
        mix = jnp.dot(o_ref_in[...], wout_ref[0:D_ATTN, :], preferred_element_type=F32)
        mix = mix + jnp.dot(rg_ref[...], wout_ref[D_ATTN:, :], preferred_element_type=F32)
        x = x + mod_ref[0, 5:6, :] * mix
    sh = mod_ref[0, 3 * sub:3 * sub + 1, :]
    sc = mod_ref[0, 3 * sub + 1:3 * sub + 2, :]
    gate = mod_ref[0, 3 * sub + 2:3 * sub + 3, :]
    h_ref[...] = (_rms(x, ng_ref[sub:sub + 1, :]) * (1.0 + sc) + sh).astype(BF16)

    def chunk(j):
        return pl.ds(pl.multiple_of(j * FF_CHUNK, FF_CHUNK), FF_CHUNK)

    def gate_up(j):
        h = h_ref[...]
        return jnp.concatenate(
            [jnp.dot(h, wg_ref[0, :, chunk(j)], preferred_element_type=F32),
             jnp.dot(h, wu_ref[0, :, chunk(j)], preferred_element_type=F32)], axis=1)

    def down(j, gu):
        g = gu[:, :FF_CHUNK]
        u = gu[:, FF_CHUNK:]
        a = (g * _sigmoid(g) * u).astype(BF16)
        return jnp.dot(a, wd_ref[0, chunk(j), :], preferred_element_type=F32)

    gu0_ref[...] = gate_up(0)
    acc_ref[...] = jnp.zeros_like(acc_ref)

    def body(i, carry):
        gu1_ref[...] = gate_up(2 * i + 1)
        acc_ref[...] += down(2 * i, gu0_ref[...])
        gu0_ref[...] = gate_up(2 * i + 2)
        acc_ref[...] += down(2 * i + 1, gu1_ref[...])
        return carry

    assert N_FF_CHUNKS % 2 == 1
    lax.fori_loop(0, N_FF_CHUNKS // 2, body, 0)
    y = x + (0.5 * gate) * (acc_ref[...] + down(N_FF_CHUNKS - 1, gu0_ref[...]))
    if final_norm:
        y = _rms(y, fg_ref[...])
    out_ref[...] = y


def _ffn_call(x, mod3, ng, wg, wu, wd, *, sub, rows_per_mod, mod_base, tm,
              mix=None, final_g=None):
    t = x.shape[0]
    ffn_idx = sub // 2
    fuse_mix = mix is not None
    final_norm = final_g is not None
    tiles_per_mod = rows_per_mod // tm

    def row_map(i):
        return (i, 0)

    def mod_map(i):
        return (mod_base + i // tiles_per_mod, 0, 0)

    const2 = lambda i: (0, 0)
    ffn_map = lambda i: (ffn_idx, 0, 0)
    in_specs = [pl.BlockSpec((tm, D_MODEL), row_map)]
    args = [x]
    if fuse_mix:
        o, rg, wout = mix
        in_specs += [pl.BlockSpec((tm, D_ATTN), row_map),
                     pl.BlockSpec((tm, D_RG), row_map),
                     pl.BlockSpec(wout.shape, const2, pipeline_mode=pl.Buffered(1))]
        args += [o, rg, wout]
    in_specs += [pl.BlockSpec((1, N_MOD, D_MODEL), mod_map),
                 pl.BlockSpec(ng.shape, const2),
                 pl.BlockSpec((1,) + wg.shape[1:], ffn_map, pipeline_mode=pl.Buffered(1)),
                 pl.BlockSpec((1,) + wu.shape[1:], ffn_map, pipeline_mode=pl.Buffered(1)),
                 pl.BlockSpec((1,) + wd.shape[1:], ffn_map, pipeline_mode=pl.Buffered(1))]
    args += [mod3, ng, wg, wu, wd]
    if final_norm:
        in_specs.append(pl.BlockSpec((1, D_MODEL), const2))
        args.append(final_g)
    return pl.pallas_call(
        functools.partial(_ffn_kernel, sub=sub, fuse_mix=fuse_mix, final_norm=final_norm),
        grid=(t // tm,),
        in_specs=in_specs,
        out_specs=pl.BlockSpec((tm, D_MODEL), row_map),
        out_shape=jax.ShapeDtypeStruct((t, D_MODEL), F32),
        scratch_shapes=[pltpu.VMEM((tm, D_MODEL), F32), pltpu.VMEM((tm, D_MODEL), BF16),
                        pltpu.VMEM((tm, 2 * FF_CHUNK), F32), pltpu.VMEM((tm, 2 * FF_CHUNK), F32)],
        compiler_params=pltpu.CompilerParams(
            dimension_semantics=("arbitrary",), vmem_limit_bytes=VMEM_LIMIT),
        name="ffn%d" % sub,
    )(*args)


def _rope(x, cos, sin_signed, first_half):
    outs = []
    for cblk in range(x.shape[1] // LANES):
        xs = x[:, cblk * LANES:(cblk + 1) * LANES]
        partner = jnp.where(first_half, pltpu.roll(xs, LANES - 16, axis=1),
                            pltpu.roll(xs, 16, axis=1))
        outs.append(xs * cos + partner * sin_signed)
    return jnp.concatenate(outs, axis=1)


def _proj_kernel(*refs, rope):
    it = iter(refs)
    x_ref = next(it)
    mod_ref = next(it)
    ng_ref = next(it)
    win_ref = next(it)
    if rope:
        cos_ref = next(it)
        sin_ref = next(it)
    q_ref, k_ref, v_ref, xr_ref, xg_ref = it

    x = x_ref[...]
    sh = mod_ref[0, 3:4, :]
    sc = mod_ref[0, 4:5, :]
    h = (_rms(x, ng_ref[1:2, :]) * (1.0 + sc) + sh).astype(BF16)

    def col(j):
        return jnp.dot(h, win_ref[:, j * D_ATTN:(j + 1) * D_ATTN], preferred_element_type=F32)

    q = col(0)
    k = col(1)
    if rope:
        cos = cos_ref[...]
        sin = sin_ref[...]
        lane = lax.broadcasted_iota(jnp.int32, (1, LANES), 1)
        first_half = (lane % 32) < 16
        q = _rope(q, cos, sin, first_half)
        k = _rope(k, cos, sin, first_half)
    q_ref[...] = (q * (HEAD_DIM ** -0.5 * LOG2_E)).astype(q_ref.dtype)
    v = col(2)
    if rope:
        k_ref[...] = k.astype(k_ref.dtype)
        v_ref[...] = v.astype(v_ref.dtype)
    else:
        tm = x.shape[0]
        for hd in range(N_HEADS):
            k_ref[pl.ds(hd, tm, stride=N_HEADS), :] = k[:, hd * V_DIM:(hd + 1) * V_DIM]
            v_ref[pl.ds(hd, tm, stride=N_HEADS), :] = v[:, hd * V_DIM:(hd + 1) * V_DIM]
    xr_ref[...] = col(3)
    xg_ref[...] = col(4)


def _proj_call(x, mod3, ng, win, *, rows_per_mod, mod_base, tm, rope_tabs=None):
    t = x.shape[0]
    rope = rope_tabs is not None
    tiles_per_mod = rows_per_mod // tm
    row_map = lambda i: (i, 0)
    const2 = lambda i: (0, 0)
    in_specs = [pl.BlockSpec((tm, D_MODEL), row_map),
                pl.BlockSpec((1, N_MOD, D_MODEL), lambda i: (mod_base + i // tiles_per_mod, 0, 0)),
                pl.BlockSpec(ng.shape, const2),
                pl.BlockSpec(win.shape, const2, pipeline_mode=pl.Buffered(1))]
    args = [x, mod3, ng, win]
    if rope:
        cos, sin = rope_tabs
        tiles_per_seq = cos.shape[0] // tm
        tab_map = lambda i: (i % tiles_per_seq, 0)
        in_specs += [pl.BlockSpec((tm, LANES), tab_map), pl.BlockSpec((tm, LANES), tab_map)]
        args += [cos, sin]
    half = pl.BlockSpec((tm, D_ATTN), row_map)
    if rope:
        kv_spec, kv_shape = half, jax.ShapeDtypeStruct((t, D_ATTN), BF16)
    else:
        kv_spec = pl.BlockSpec((tm * N_HEADS, V_DIM), row_map)
        kv_shape = jax.ShapeDtypeStruct((t * N_HEADS, V_DIM), F32)
    return pl.pallas_call(
        functools.partial(_proj_kernel, rope=rope),
        grid=(t // tm,),
        in_specs=in_specs,
        out_specs=[half, kv_spec, kv_spec, half, half],
        out_shape=[jax.ShapeDtypeStruct((t, D_ATTN), BF16),
                   kv_shape,
                   kv_shape,
                   jax.ShapeDtypeStruct((t, D_RG), F32),
                   jax.ShapeDtypeStruct((t, D_RG), F32)],
        compiler_params=pltpu.CompilerParams(
            dimension_semantics=("arbitrary",), vmem_limit_bytes=VMEM_LIMIT),
        name="proj",
    )(*args)


def _diff_lambda(dl_ref):
    dl = dl_ref[...]
    return (jnp.exp(jnp.sum(dl[0:1] * dl[1:2], axis=-1, keepdims=True))
            - jnp.exp(jnp.sum(dl[2:3] * dl[3:4], axis=-1, keepdims=True)) + LAM_INIT)


def _stack_maps(q):
    map0 = lax.broadcasted_iota(jnp.int32, (1, V_DIM), 1) < HEAD_DIM
    zero = jnp.zeros_like(q)
    return jnp.concatenate([jnp.where(map0, q, zero), jnp.where(map0, zero, q)], axis=0)


def _finish_head(o2, l, lam, sg, tq):
    o = o2[:tq] * (1.0 / l[:tq]) - o2[tq:] * (lam / l[tq:])
    return _rms(o, sg) * (1.0 - LAM_INIT)


def _attn_small_kernel(q_ref, k_ref, v_ref, dl_ref, sg_ref, o_ref, *, tq):
    lam = _diff_lambda(dl_ref)
    nseq = q_ref.shape[0]
    sk = k_ref.shape[0] // (N_HEADS * nseq)
    ones = jnp.ones((sk, V_DIM), BF16)
    for bi in range(nseq):
        for hd in range(N_HEADS):
            cols = slice(hd * V_DIM, (hd + 1) * V_DIM)
            head_rows = pl.ds(bi * sk * N_HEADS + hd, sk, stride=N_HEADS)
            q2 = _stack_maps(q_ref[bi, :, cols])
            k = k_ref[head_rows, :].astype(BF16)
            s = lax.dot_general(q2, k, (((1,), (1,)), ((), ())), preferred_element_type=F32)
            e = jnp.exp2(s - jnp.max(s, axis=-1, keepdims=True))
            v1 = jnp.concatenate([v_ref[head_rows, :].astype(BF16), ones], axis=1)
            o2 = jnp.dot(e.astype(BF16), v1, preferred_element_type=F32)
            o = _finish_head(o2[:, :V_DIM], o2[:, V_DIM:], lam, sg_ref[...], tq)
            o_ref[bi, :, cols] = o.astype(o_ref.dtype)


def _attn_pipe_kernel(q_ref, k_ref, v_ref, kc_ref, vc_ref, dl_ref, sg_ref, o_ref,
                      s0, s1, sc0, sc1, p0, p1, pc0, pc1,
                      q2_scr, mv_scr, mb_scr, oacc_scr, *, tq, kt):
    lam = _diff_lambda(dl_ref)
    s_bufs, sc_bufs, p_bufs, pc_bufs = (s0, s1), (sc0, sc1), (p0, p1), (pc0, pc1)
    n_steps = k_ref.shape[1] // kt
    for t in range(N_HEADS + 2):
        ha, hb, hc = t, t - 1, t - 2
        do_a, do_b, do_c = 0 <= ha < N_HEADS, 0 <= hb < N_HEADS, 0 <= hc < N_HEADS
        cols_a = slice(ha * V_DIM, (ha + 1) * V_DIM)
        cols_c = slice(hc * V_DIM, (hc + 1) * V_DIM)
        if do_a:
            q2_scr[...] = _stack_maps(q_ref[0, :, cols_a])
            mv_scr[...] = jnp.full(mv_scr.shape, -jnp.inf, F32)
        if do_c:
            oacc_scr[...] = jnp.zeros(oacc_scr.shape, F32)

        def key_block(k_blk, v_blk, s_a, s_b, p_b, p_c):
            whole = (slice(None), slice(None))
            if do_a:
                s = lax.dot_general(q2_scr[...], k_blk(), (((1,), (1,)), ((), ())),
                                    preferred_element_type=F32)
                s_a[0][s_a[1] + whole] = s
                mx = s[:, 0:LANES]
                for cb in range(1, s.shape[1] // LANES):
                    mx = jnp.maximum(mx, s[:, cb * LANES:(cb + 1) * LANES])
                mv_scr[...] = jnp.maximum(mv_scr[...], mx)
            if do_b:
                s_ref, s_idx = s_b
                p_ref, p_idx = p_b
                n_keys = s_ref.shape[-1]
                for r0 in range(0, 2 * tq, EXP_ROWS):
                    rows = slice(r0, r0 + EXP_ROWS)
                    mb = mb_scr[rows, :]
                    for c0 in range(0, n_keys, LANES):
                        at = (rows, slice(c0, c0 + LANES))
                        p_ref[p_idx + at] = jnp.exp2(s_ref[s_idx + at] - mb).astype(BF16)
            if do_c:
                v = v_blk()
                v1 = jnp.concatenate([v, jnp.ones(v.shape, BF16)], axis=1)
                oacc_scr[...] += jnp.dot(p_c[0][p_c[1] + whole], v1, preferred_element_type=F32)

        def new_keys_step(j, carry):
            keys = pl.ds(pl.multiple_of(j * kt, kt), kt)
            key_block(lambda: k_ref[0, keys, cols_a], lambda: v_ref[0, keys, cols_c],
                      (s_bufs[ha % 2], (j,)), (s_bufs[hb % 2], (j,)),
                      (p_bufs[hb % 2], (j,)), (p_bufs[hc % 2], (j,)))
            return carry

        lax.fori_loop(0, n_steps, new_keys_step, 0, unroll=True)
        key_block(lambda: kc_ref[0, :, cols_a], lambda: vc_ref[0, :, cols_c],
                  (sc_bufs[ha % 2], ()), (sc_bufs[hb % 2], ()),
                  (pc_bufs[hb % 2], ()), (pc_bufs[hc % 2], ()))
        if do_a:
            m = jnp.max(mv_scr[...], axis=-1, keepdims=True)
            mb_scr[...] = jnp.broadcast_to(m, mb_scr.shape)
        if do_c:
            o = _finish_head(oacc_scr[:, :V_DIM], oacc_scr[:, V_DIM:], lam, sg_ref[...], tq)
            o_ref[0, :, cols_c] = o.astype(o_ref.dtype)


def _attn_call(q, k, v, diff_lambda, subln_g, *, tq, cache=None):
    b, sq, _ = q.shape
    qmap = lambda bi, qi: (bi, qi, 0)
    seqmap = lambda bi, qi: (bi, 0, 0)
    const2 = lambda bi, qi: (0, 0)
    nseq = 1 if cache is not None else SEQS_PER_STEP
    in_specs = [pl.BlockSpec((nseq, tq, D_ATTN), qmap)]
    args = [q, k, v]
    if cache is not None:
        sk = k.shape[1]
        in_specs += [pl.BlockSpec((1, sk, D_ATTN), seqmap, pipeline_mode=pl.Buffered(1))] * 2
        kc, vc = cache
        past = kc.shape[1]
        kt = 1024
        rows = 2 * tq
        body = functools.partial(_attn_pipe_kernel, tq=tq, kt=kt)
        in_specs += [pl.BlockSpec((1, past, D_ATTN), seqmap)] * 2
        args += [kc, vc]
        scratch = ([pltpu.VMEM((sk // kt, rows, kt), F32)] * 2 + [pltpu.VMEM((rows, past), F32)] * 2
                   + [pltpu.VMEM((sk // kt, rows, kt), BF16)] * 2 + [pltpu.VMEM((rows, past), BF16)] * 2
                   + [pltpu.VMEM((rows, V_DIM), BF16)] + [pltpu.VMEM((rows, LANES), F32)] * 2
                   + [pltpu.VMEM((rows, 2 * V_DIM), F32)])
    else:
        assert sq == tq and b % nseq == 0
        body = functools.partial(_attn_small_kernel, tq=tq)
        in_specs += [pl.BlockSpec((nseq * (k.shape[0] // b), V_DIM), lambda bi, qi: (bi, 0))] * 2
        scratch = []
    in_specs += [pl.BlockSpec(diff_lambda.shape, const2), pl.BlockSpec(subln_g.shape, const2)]
    args += [diff_lambda, subln_g]
    return pl.pallas_call(
        body,
        grid=(b // nseq, sq // tq),
        in_specs=in_specs,
        out_specs=pl.BlockSpec((nseq, tq, D_ATTN), qmap),
        out_shape=jax.ShapeDtypeStruct((b, sq, D_ATTN), BF16),
        scratch_shapes=scratch,
        compiler_params=pltpu.CompilerParams(
            dimension_semantics=("arbitrary", "arbitrary"), vmem_limit_bytes=VMEM_LIMIT),
        name="attn",
    )(*args)


def _scan8(a, b, reverse):
    sub = lax.broadcasted_iota(jnp.int32, a.shape, 1)
    for d in (1, 2, 4):
        if reverse:
            shift, m = SUBLANES - d, sub < SUBLANES - d
        else:
            shift, m = d, sub >= d
        a_s = pltpu.roll(a, shift, axis=1)
        b_s = pltpu.roll(b, shift, axis=1)
        b = jnp.where(m, a * b_s + b, b)
        a = jnp.where(m, a * a_s, a)
    return a, b


def _scan_block(a, b, carry, reverse):
    tb, l = a.shape
    g = tb // SUBLANES
    a3, b3 = _scan8(a.reshape(g, SUBLANES, l), b.reshape(g, SUBLANES, l), reverse)
    hs = [None] * g
    order = range(g - 1, -1, -1) if reverse else range(g)
    for gi in order:
        hg = a3[gi] * carry + b3[gi]
        hs[gi] = hg
        carry = hg[0:1] if reverse else hg[SUBLANES - 1:SUBLANES]
    return jnp.concatenate(hs, axis=0), carry


def _softplus(x):
    return jnp.maximum(x, 0.0) + jnp.log(1.0 + jnp.exp(-jnp.abs(x)))


def _gelu_tanh(x):
    return 0.5 * x * (1.0 + jnp.tanh(math.sqrt(2.0 / math.pi) * (x + 0.044715 * (x * x * x))))


def _rglru_kernel(xr_ref, xg_ref, h0_ref, cw_ref, cb_ref, wg_ref, bg_ref, lam_ref,
                  y_ref, hl_ref, xpad, ab, bb, hf, *, s, cc, tb):
    nblk = s // tb
    nsub = cc // LANES
    cbase = pl.program_id(1) * nsub
    for j in range(nsub):
        ls = slice(j * LANES, (j + 1) * LANES)
        xpad[0:SUBLANES, ls] = jnp.zeros((SUBLANES, LANES), F32)
        xpad[s + SUBLANES:s + 2 * SUBLANES, ls] = jnp.zeros((SUBLANES, LANES), F32)
        xpad[SUBLANES:s + SUBLANES, ls] = xr_ref[0, :, ls]
        sp_f = RG_C * _softplus(-lam_ref[0:1, ls])
        sp_b = RG_C * _softplus(-lam_ref[1:2, ls])
        wg = wg_ref[cbase + j]
        bg = bg_ref[:, ls]
        cw = cw_ref[:, ls]
        cb = cb_ref[:, ls]

        def gate_ab(pre_r, pre_i, sp, xc):
            r = 0.5 * jnp.tanh(0.5 * pre_r) + 0.5
            i = 0.5 * jnp.tanh(0.5 * pre_i) + 0.5
            z = r * sp
            a = jnp.exp2(r * (sp * (-LOG2_E)))
            bx = jnp.sqrt((1.0 + a * a) * jnp.tanh(z)) * (i * xc)
            return a, bx

        def fwd_block(t, carry):
            t0 = pl.multiple_of(t * tb, tb)
            xw = xpad[pl.ds(t0, tb + 2 * SUBLANES), ls]
            xc = cb
            for tap in range(CONV_W):
                lo = SUBLANES - CONV_PAD_LEFT + tap
                xc = xc + xw[lo:lo + tb] * cw[tap:tap + 1]
            pre = jnp.dot(xc.astype(BF16), wg, preferred_element_type=F32)
            a_f, b_f = gate_ab(pre[:, 0:128] + bg[0:1], pre[:, 128:256] + bg[1:2], sp_f, xc)
            a_b, b_b = gate_ab(pre[:, 256:384] + bg[2:3], pre[:, 384:512] + bg[3:4], sp_b, xc)
            ab[pl.ds(t0, tb), ls] = a_b
            bb[pl.ds(t0, tb), ls] = b_b
            h, carry = _scan_block(a_f, b_f, carry, False)
            hf[pl.ds(t0, tb), ls] = h
            return carry

        hl_ref[0, 0:1, ls] = lax.fori_loop(0, nblk, fwd_block, h0_ref[0, 0:1, ls])

        def bwd_block(tt, carry):
            t0 = pl.multiple_of((nblk - 1 - tt) * tb, tb)
            h, carry = _scan_block(ab[pl.ds(t0, tb), ls], bb[pl.ds(t0, tb), ls], carry, True)
            y = (hf[pl.ds(t0, tb), ls] + h) * _gelu_tanh(xg_ref[0, pl.ds(t0, tb), ls])
            y_ref[0, pl.ds(t0, tb), ls] = y.astype(y_ref.dtype)
            return carry

        hl_ref[0, 1:2, ls] = lax.fori_loop(0, nblk, bwd_block, h0_ref[0, 1:2, ls])


def _rglru_call(xr, xg, h0, cw, cb, wg, bg, lam, *, cc):
    b, s, _ = xr.shape
    nch = D_RG // cc
    tb = 256
    seq = pl.BlockSpec((1, s, cc), lambda bi, ci: (bi, 0, ci))
    st = pl.BlockSpec((1, 2, cc), lambda bi, ci: (bi, 0, ci))
    return pl.pallas_call(
        functools.partial(_rglru_kernel, s=s, cc=cc, tb=tb),
        grid=(b, nch),
        in_specs=[seq, seq, st,
                  pl.BlockSpec((CONV_W, cc), lambda bi, ci: (0, ci)),
                  pl.BlockSpec((1, cc), lambda bi, ci: (0, ci)),
                  pl.BlockSpec(wg.shape, lambda bi, ci: (0, 0, 0)),
                  pl.BlockSpec((4, cc), lambda bi, ci: (0, ci)),
                  pl.BlockSpec((2, cc), lambda bi, ci: (0, ci))],
        out_specs=[seq, st],
        out_shape=[jax.ShapeDtypeStruct((b, s, D_RG), BF16),
                   jax.ShapeDtypeStruct((b, 2, D_RG), F32)],
        scratch_shapes=[pltpu.VMEM((s + 2 * SUBLANES, cc), F32),
                        pltpu.VMEM((s, cc), F32), pltpu.VMEM((s, cc), F32),
                        pltpu.VMEM((s, cc), F32)],
        compiler_params=pltpu.CompilerParams(
            dimension_semantics=("arbitrary", "arbitrary"), vmem_limit_bytes=VMEM_LIMIT),
        name="rglru",
    )(xr, xg, h0, cw, cb, wg, bg, lam)


def _rope_tables(s):
    pos = np.arange(s)
    row = (pos // GRID_W).astype(np.float32)
    col = (pos % GRID_W).astype(np.float32)
    n_freq = HEAD_DIM // 4
    inv_freq = (ROPE_BASE ** (-np.arange(n_freq, dtype=np.float32) / n_freq)).astype(np.float32)
    p = np.arange(LANES) % HEAD_DIM
    freq = inv_freq[p % n_freq]
    ang = (np.where((p < HEAD_DIM // 2)[None, :], row[:, None], col[:, None])
           * freq[None, :]).astype(np.float32)
    sign = np.where((p % (2 * n_freq)) < n_freq, -1.0, 1.0)
    cos = np.cos(ang.astype(np.float64)).astype(np.float32)
    sin = (np.sin(ang.astype(np.float64)) * sign[None, :]).astype(np.float32)
    return jnp.asarray(cos), jnp.asarray(sin)


def _gate_weights(w_r, w_i):
    def bd(w):
        w = w.reshape(4, 2, RG_BLOCK_W, RG_BLOCK_W)
        z = jnp.zeros_like(w[:, 0])
        top = jnp.concatenate([w[:, 0], z], axis=2)
        bot = jnp.concatenate([z, w[:, 1]], axis=2)
        return jnp.concatenate([top, bot], axis=1)
    return jnp.concatenate([bd(w_r[0]), bd(w_i[0]), bd(w_r[1]), bd(w_i[1])], axis=2).astype(BF16)


def kernel(x_prompt, x_sample, cache_attn_k, cache_attn_v, state_rglru, c, c_ctx, norm_g, w_mod, b_mod, ffn_w_gate, ffn_w_up, ffn_w_down, w_in, w_out, diff_lambda, subln_g, conv_w, conv_b, rg_w_r, rg_b_r, rg_w_i, rg_b_i, rg_lambda, final_g):
    l = 0
    bsz, seq, _ = x_prompt.shape
    dbsz, dseq, _ = x_sample.shape
    past = cache_attn_k.shape[2]

    ng = norm_g[l]
    wgate = ffn_w_gate[l].astype(BF16)
    wup = ffn_w_up[l].astype(BF16)
    wd = ffn_w_down[l].astype(BF16)
    win = w_in[l].astype(BF16)
    wout = w_out[l].astype(BF16)
    wg = _gate_weights(rg_w_r[l], rg_w_i[l])
    bg = jnp.stack([rg_b_r[l, 0], rg_b_i[l, 0], rg_b_r[l, 1], rg_b_i[l, 1]])
    cw = conv_w[l]
    cb = conv_b[l][None, :]
    lam = rg_lambda[l]
    dl = diff_lambda[l]
    sg = subln_g[l][None, :]
    fg = final_g[None, :]

    c8 = jnp.concatenate([c_ctx[None, :], c, jnp.zeros((SUBLANES - 1 - dbsz, D_MODEL), F32)], axis=0)
    mod3 = _mod_call(c8, w_mod[l], b_mod[l][None, :]).reshape(SUBLANES, N_MOD, D_MODEL)

    def layer(x, *, rows_per_mod, mod_base, nb, s, k_ctx, v_ctx, h0, cc, tq, rope_tabs):
        tm = 512
        tm_ffn = 1024
        x1 = _ffn_call(x, mod3, ng, wgate, wup, wd, sub=0, rows_per_mod=rows_per_mod,
                       mod_base=mod_base, tm=tm_ffn)
        q, k, v, xr, xg = _proj_call(x1, mod3, ng, win, rows_per_mod=rows_per_mod,
                                     mod_base=mod_base, tm=tm, rope_tabs=rope_tabs)
        q3 = q.reshape(nb, s, D_ATTN)
        if k_ctx is None:
            o = _attn_call(q3, k, v, dl, sg, tq=tq)
        else:
            o = _attn_call(q3, k.reshape(nb, s, D_ATTN), v.reshape(nb, s, D_ATTN), dl, sg, tq=tq,
                           cache=(k_ctx, v_ctx))
        rg, h_last = _rglru_call(xr.reshape(nb, s, D_RG), xg.reshape(nb, s, D_RG), h0,
                                 cw, cb, wg, bg, lam, cc=cc)
        y = _ffn_call(x1, mod3, ng, wgate, wup, wd, sub=2, rows_per_mod=rows_per_mod,
                      mod_base=mod_base, tm=tm_ffn,
                      mix=(o.reshape(nb * s, D_ATTN), rg.reshape(nb * s, D_RG), wout),
                      final_g=fg)
        return y, k, v, h_last

    yp, k_new, v_new, h_new = layer(
        x_prompt.reshape(bsz * seq, D_MODEL), rows_per_mod=bsz * seq, mod_base=0, nb=bsz, s=seq,
        k_ctx=None, v_ctx=None, h0=jnp.zeros((bsz, 2, D_RG), F32), cc=D_RG, tq=seq,
        rope_tabs=None)
    ys, _, _, _ = layer(
        x_sample.reshape(dbsz * dseq, D_MODEL), rows_per_mod=dseq, mod_base=1, nb=dbsz, s=dseq,
        k_ctx=cache_attn_k[:, l].reshape(dbsz, past, D_ATTN).astype(BF16),
        v_ctx=cache_attn_v[:, l].reshape(dbsz, past, D_ATTN).astype(BF16),
        h0=state_rglru[:, l], cc=LANES, tq=256, rope_tabs=_rope_tables(dseq))

    return (yp.reshape(bsz, seq, D_MODEL),
            ys.reshape(dbsz, dseq, D_MODEL),
            k_new.reshape(bsz, 1, seq, N_HEADS, V_DIM),
            v_new.reshape(bsz, 1, seq, N_HEADS, V_DIM),
            h_new.reshape(bsz, 1, 2, D_RG))
```

```python
import functools
import math

import jax
import jax.numpy as jnp
import numpy as np
from jax import lax
from jax.experimental import pallas as pl
from jax.experimental.pallas import tpu as pltpu

F32 = jnp.float32
BF16 = jnp.bfloat16

D_MODEL = 1024
N_HEADS = 4
HEAD_DIM = 64
V_DIM = 2 * HEAD_DIM
D_ATTN = N_HEADS * V_DIM
D_RG = 512
RG_BLOCK_W = 64
RG_C = 8.0
CONV_W = 4
CONV_PAD_LEFT = 2
D_FF = 2816
N_MOD = 9
GRID_W = 64
ROPE_BASE = 10000.0
EPS = 1e-6
LAM_INIT = 0.8 - 0.6 * math.exp(-0.3 * 0)
LOG2_E = math.log2(math.e)

LANES = 128
SUBLANES = 8
MXU_N = 256
SEQS_PER_STEP = 2
EXP_ROWS = 64
FF_CHUNK = MXU_N
N_FF_CHUNKS = D_FF // FF_CHUNK
VMEM_LIMIT = 56 * 1024 * 1024


def _sigmoid(x):
    return 1.0 / (1.0 + jnp.exp(-x))


def _rms(x, g):
    ms = jnp.mean(x * x, axis=-1, keepdims=True)
    return x * lax.rsqrt(ms + EPS) * g


def _mod_kernel(c_ref, w_ref, b_ref, o_ref):
    c = c_ref[...]
    s = (c * _sigmoid(c)).astype(BF16)
    o_ref[...] = jnp.dot(s, w_ref[...].astype(BF16), preferred_element_type=F32) + b_ref[...]


def _mod_call(c8, w_mod, b_mod):
    n = w_mod.shape[1]
    tn = 1536
    return pl.pallas_call(
        _mod_kernel,
        grid=(n // tn,),
        in_specs=[
            pl.BlockSpec((SUBLANES, D_MODEL), lambda j: (0, 0)),
            pl.BlockSpec((D_MODEL, tn), lambda j: (0, j)),
            pl.BlockSpec((1, tn), lambda j: (0, j)),
        ],
        out_specs=pl.BlockSpec((SUBLANES, tn), lambda j: (0, j)),
        out_shape=jax.ShapeDtypeStruct((SUBLANES, n), F32),
        compiler_params=pltpu.CompilerParams(
            dimension_semantics=("arbitrary",), vmem_limit_bytes=VMEM_LIMIT),
        name="mod",
    )(c8, w_mod, b_mod)


def _ffn_kernel(*refs, sub, fuse_mix, final_norm):
    it = iter(refs)
    x_ref = next(it)
    if fuse_mix:
        o_ref_in = next(it)
        rg_ref = next(it)
        wout_ref = next(it)
    mod_ref = next(it)
    ng_ref = next(it)
    wg_ref = next(it)
    wu_ref = next(it)
    wd_ref = next(it)
    fg_ref = next(it) if final_norm else None
    out_ref = next(it)
    acc_ref = out_ref
    h_ref = next(it)
    gu0_ref = next(it)
    gu1_ref = next(it)

    x = x_ref[...]
    if fuse_mix:
        mix = jnp.dot(o_ref_in[...], wout_ref[0:D_ATTN, :], preferred_element_type=F32)
        mix = mix + jnp.dot(rg_ref[...], wout_ref[D_ATTN:, :], preferred_element_type=F32)
        x = x + mod_ref[0, 5:6, :] * mix
    sh = mod_ref[0, 3 * sub:3 * sub + 1, :]
    sc = mod_ref[0, 3 * sub + 1:3 * sub + 2, :]
    gate = mod_ref[0, 3 * sub + 2:3 * sub + 3, :]
    h_ref[...] = (_rms(x, ng_ref[sub:sub + 1, :]) * (1.0 + sc) + sh).astype(BF16)

    def chunk(j):
        return pl.ds(pl.multiple_of(j * FF_CHUNK, FF_CHUNK), FF_CHUNK)

    def gate_up(j):
        h = h_ref[...]
        return jnp.concatenate(
            [jnp.dot(h, wg_ref[0, :, chunk(j)], preferred_element_type=F32),
             jnp.dot(h, wu_ref[0, :, chunk(j)], preferred_element_type=F32)], axis=1)

    def down(j, gu):
        g = gu[:, :FF_CHUNK]
        u = gu[:, FF_CHUNK:]
        a = (g * _sigmoid(g) * u).astype(BF16)
        return jnp.dot(a, wd_ref[0, chunk(j), :], preferred_element_type=F32)

    gu0_ref[...] = gate_up(0)
    gu1_ref[...] = gate_up(1)
    acc_ref[...] = down(0, gu0_ref[...])

    def body(i, carry):
        gu0_ref[...] = gate_up(2 * i + 2)
        acc_ref[...] += down(2 * i + 1, gu1_ref[...])
        gu1_ref[...] = gate_up(2 * i + 3)
        acc_ref[...] += down(2 * i + 2, gu0_ref[...])
        return carry

    assert N_FF_CHUNKS % 2 == 1 and N_FF_CHUNKS >= 5
    lax.fori_loop(0, (N_FF_CHUNKS - 3) // 2, body, 0, unroll=2)
    gu0_ref[...] = gate_up(N_FF_CHUNKS - 1)
    acc_ref[...] += down(N_FF_CHUNKS - 2, gu1_ref[...])
    y = x + (0.5 * gate) * (acc_ref[...] + down(N_FF_CHUNKS - 1, gu0_ref[...]))
    if final_norm:
        y = _rms(y, fg_ref[...])
    out_ref[...] = y


def _ffn_call(x, mod3, ng, wg, wu, wd, *, sub, rows_per_mod, mod_base, tm,
              mix=None, final_g=None):
    t = x.shape[0]
    ffn_idx = sub // 2
    fuse_mix = mix is not None
    final_norm = final_g is not None
    tiles_per_mod = rows_per_mod // tm

    def row_map(i):
        return (i, 0)

    def mod_map(i):
        return (mod_base + i // tiles_per_mod, 0, 0)

    const2 = lambda i: (0, 0)
    ffn_map = lambda i: (ffn_idx, 0, 0)
    in_specs = [pl.BlockSpec((tm, D_MODEL), row_map)]
    args = [x]
    if fuse_mix:
        o, rg, wout = mix
        in_specs += [pl.BlockSpec((tm, D_ATTN), row_map),
                     pl.BlockSpec((tm, D_RG), row_map),
                     pl.BlockSpec(wout.shape, const2, pipeline_mode=pl.Buffered(1))]
        args += [o, rg, wout]
    in_specs += [pl.BlockSpec((1, N_MOD, D_MODEL), mod_map),
                 pl.BlockSpec(ng.shape, const2),
                 pl.BlockSpec((1,) + wg.shape[1:], ffn_map, pipeline_mode=pl.Buffered(1)),
                 pl.BlockSpec((1,) + wu.shape[1:], ffn_map, pipeline_mode=pl.Buffered(1)),
                 pl.BlockSpec((1,) + wd.shape[1:], ffn_map, pipeline_mode=pl.Buffered(1))]
    args += [mod3, ng, wg, wu, wd]
    if final_norm:
        in_specs.append(pl.BlockSpec((1, D_MODEL), const2))
        args.append(final_g)
    return pl.pallas_call(
        functools.partial(_ffn_kernel, sub=sub, fuse_mix=fuse_mix, final_norm=final_norm),
        grid=(t // tm,),
        in_specs=in_specs,
        out_specs=pl.BlockSpec((tm, D_MODEL), row_map),
        out_shape=jax.ShapeDtypeStruct((t, D_MODEL), F32),
        scratch_shapes=[pltpu.VMEM((tm, D_MODEL), BF16),
                        pltpu.VMEM((tm, 2 * FF_CHUNK), F32), pltpu.VMEM((tm, 2 * FF_CHUNK), F32)],
        compiler_params=pltpu.CompilerParams(
            dimension_semantics=("arbitrary",), vmem_limit_bytes=VMEM_LIMIT),
        name="ffn%d" % sub,
    )(*args)


def _rope(x, cos, sin_signed, first_half):
    outs = []
    for cblk in range(x.shape[1] // LANES):
        xs = x[:, cblk * LANES:(cblk + 1) * LANES]
        partner = jnp.where(first_half, pltpu.roll(xs, LANES - 16, axis=1),
                            pltpu.roll(xs, 16, axis=1))
        outs.append(xs * cos + partner * sin_signed)
    return jnp.concatenate(outs, axis=1)


def _proj_kernel(*refs, rope):
    it = iter(refs)
    x_ref = next(it)
    mod_ref = next(it)
    ng_ref = next(it)
    win_ref = next(it)
    if rope:
        cos_ref = next(it)
        sin_ref = next(it)
    q_ref, k_ref, v_ref, xr_ref, xg_ref = it

    x = x_ref[...]
    sh = mod_ref[0, 3:4, :]
    sc = mod_ref[0, 4:5, :]
    h = (_rms(x, ng_ref[1:2, :]) * (1.0 + sc) + sh).astype(BF16)

    def col(j):
        return jnp.dot(h, win_ref[:, j * D_ATTN:(j + 1) * D_ATTN], preferred_element_type=F32)

    q = col(0)
    k = col(1)
    if rope:
        cos = cos_ref[...]
        sin = sin_ref[...]
        lane = lax.broadcasted_iota(jnp.int32, (1, LANES), 1)
        first_half = (lane % 32) < 16
        q = _rope(q, cos, sin, first_half)
        k = _rope(k, cos, sin, first_half)
    q_ref[...] = (q * (HEAD_DIM ** -0.5 * LOG2_E)).astype(q_ref.dtype)
    v = col(2)
    if rope:
        k_ref[...] = k.astype(k_ref.dtype)
        v_ref[...] = v.astype(v_ref.dtype)
    else:
        tm = x.shape[0]
        for hd in range(N_HEADS):
            k_ref[pl.ds(hd, tm, stride=N_HEADS), :] = k[:, hd * V_DIM:(hd + 1) * V_DIM]
            v_ref[pl.ds(hd, tm, stride=N_HEADS), :] = v[:, hd * V_DIM:(hd + 1) * V_DIM]
    xr_ref[...] = col(3)
    xg_ref[...] = col(4)


def _proj_call(x, mod3, ng, win, *, rows_per_mod, mod_base, tm, rope_tabs=None):
    t = x.shape[0]
    rope = rope_tabs is not None
    tiles_per_mod = rows_per_mod // tm
    row_map = lambda i: (i, 0)
    const2 = lambda i: (0, 0)
    in_specs = [pl.BlockSpec((tm, D_MODEL), row_map),
                pl.BlockSpec((1, N_MOD, D_MODEL), lambda i: (mod_base + i // tiles_per_mod, 0, 0)),
                pl.BlockSpec(ng.shape, const2),
                pl.BlockSpec(win.shape, const2, pipeline_mode=pl.Buffered(1))]
    args = [x, mod3, ng, win]
    if rope:
        cos, sin = rope_tabs
        tiles_per_seq = cos.shape[0] // tm
        tab_map = lambda i: (i % tiles_per_seq, 0)
        in_specs += [pl.BlockSpec((tm, LANES), tab_map), pl.BlockSpec((tm, LANES), tab_map)]
        args += [cos, sin]
    half = pl.BlockSpec((tm, D_ATTN), row_map)
    if rope:
        kv_spec, kv_shape = half, jax.ShapeDtypeStruct((t, D_ATTN), BF16)
    else:
        kv_spec = pl.BlockSpec((tm * N_HEADS, V_DIM), row_map)
        kv_shape = jax.ShapeDtypeStruct((t * N_HEADS, V_DIM), F32)
    return pl.pallas_call(
        functools.partial(_proj_kernel, rope=rope),
        grid=(t // tm,),
        in_specs=in_specs,
        out_specs=[half, kv_spec, kv_spec, half, half],
        out_shape=[jax.ShapeDtypeStruct((t, D_ATTN), BF16),
                   kv_shape,
                   kv_shape,
                   jax.ShapeDtypeStruct((t, D_RG), F32),
                   jax.ShapeDtypeStruct((t, D_RG), F32)],
        compiler_params=pltpu.CompilerParams(
            dimension_semantics=("arbitrary",), vmem_limit_bytes=VMEM_LIMIT),
        name="proj",
    )(*args)


def _diff_lambda(dl_ref):
    dl = dl_ref[...]
    return (jnp.exp(jnp.sum(dl[0:1] * dl[1:2], axis=-1, keepdims=True))
            - jnp.exp(jnp.sum(dl[2:3] * dl[3:4], axis=-1, keepdims=True)) + LAM_INIT)


def _stack_maps(q):
    map0 = lax.broadcasted_iota(jnp.int32, (1, V_DIM), 1) < HEAD_DIM
    zero = jnp.zeros_like(q)
    return jnp.concatenate([jnp.where(map0, q, zero), jnp.where(map0, zero, q)], axis=0)


def _finish_head(o2, l, lam, sg, tq):
    o = o2[:tq] * (1.0 / l[:tq]) - o2[tq:] * (lam / l[tq:])
    return _rms(o, sg) * (1.0 - LAM_INIT)


def _attn_small_kernel(q_ref, k_ref, v_ref, dl_ref, sg_ref, o_ref, *, tq):
    lam = _diff_lambda(dl_ref)
    nseq = q_ref.shape[0]
    sk = k_ref.shape[0] // (N_HEADS * nseq)
    ones = jnp.ones((sk, V_DIM), BF16)
    for bi in range(nseq):
        for hd in range(N_HEADS):
            cols = slice(hd * V_DIM, (hd + 1) * V_DIM)
            head_rows = pl.ds(bi * sk * N_HEADS + hd, sk, stride=N_HEADS)
            q2 = _stack_maps(q_ref[bi, :, cols])
            k = k_ref[head_rows, :].astype(BF16)
            s = lax.dot_general(q2, k, (((1,), (1,)), ((), ())), preferred_element_type=F32)
            e = jnp.exp2(s - jnp.max(s, axis=-1, keepdims=True))
            v1 = jnp.concatenate([v_ref[head_rows, :].astype(BF16), ones], axis=1)
            o2 = jnp.dot(e.astype(BF16), v1, preferred_element_type=F32)
            o = _finish_head(o2[:, :V_DIM], o2[:, V_DIM:], lam, sg_ref[...], tq)
            o_ref[bi, :, cols] = o.astype(o_ref.dtype)


def _attn_pipe_kernel(q_ref, k_ref, v_ref, kc_ref, vc_ref, dl_ref, sg_ref, o_ref,
                      s0, s1, sc0, sc1, p0, p1, pc0, pc1,
                      q2_scr, mv_scr, mb_scr, oacc_scr, *, tq, kt):
    lam = _diff_lambda(dl_ref)
    s_bufs, sc_bufs, p_bufs, pc_bufs = (s0, s1), (sc0, sc1), (p0, p1), (pc0, pc1)
    n_steps = k_ref.shape[1] // kt
    for t in range(N_HEADS + 2):
        ha, hb, hc = t, t - 1, t - 2
        do_a, do_b, do_c = 0 <= ha < N_HEADS, 0 <= hb < N_HEADS, 0 <= hc < N_HEADS
        cols_a = slice(ha * V_DIM, (ha + 1) * V_DIM)
        cols_c = slice(hc * V_DIM, (hc + 1) * V_DIM)
        if do_a:
            q2_scr[...] = _stack_maps(q_ref[0, :, cols_a])
            mv_scr[...] = jnp.full(mv_scr.shape, -jnp.inf, F32)
        if do_c:
            oacc_scr[...] = jnp.zeros(oacc_scr.shape, F32)

        def key_block(k_blk, v_blk, s_a, s_b, p_b, p_c):
            whole = (slice(None), slice(None))
            if do_a:
                s = lax.dot_general(q2_scr[...], k_blk(), (((1,), (1,)), ((), ())),
                                    preferred_element_type=F32)
                s_a[0][s_a[1] + whole] = s
                mx = s[:, 0:LANES]
                for cb in range(1, s.shape[1] // LANES):
                    mx = jnp.maximum(mx, s[:, cb * LANES:(cb + 1) * LANES])
                mv_scr[...] = jnp.maximum(mv_scr[...], mx)
            if do_b:
                s_ref, s_idx = s_b
                p_ref, p_idx = p_b
                n_keys = s_ref.shape[-1]
                for r0 in range(0, 2 * tq, EXP_ROWS):
                    rows = slice(r0, r0 + EXP_ROWS)
                    mb = mb_scr[rows, :]
                    for c0 in range(0, n_keys, LANES):
                        at = (rows, slice(c0, c0 + LANES))
                        p_ref[p_idx + at] = jnp.exp2(s_ref[s_idx + at] - mb).astype(BF16)
            if do_c:
                v = v_blk()
                v1 = jnp.concatenate([v, jnp.ones(v.shape, BF16)], axis=1)
                oacc_scr[...] += jnp.dot(p_c[0][p_c[1] + whole], v1, preferred_element_type=F32)

        def new_keys_step(j, carry):
            keys = pl.ds(pl.multiple_of(j * kt, kt), kt)
            key_block(lambda: k_ref[0, keys, cols_a], lambda: v_ref[0, keys, cols_c],
                      (s_bufs[ha % 2], (j,)), (s_bufs[hb % 2], (j,)),
                      (p_bufs[hb % 2], (j,)), (p_bufs[hc % 2], (j,)))
            return carry

        lax.fori_loop(0, n_steps, new_keys_step, 0, unroll=True)
        key_block(lambda: kc_ref[0, :, cols_a], lambda: vc_ref[0, :, cols_c],
                  (sc_bufs[ha % 2], ()), (sc_bufs[hb % 2], ()),
                  (pc_bufs[hb % 2], ()), (pc_bufs[hc % 2], ()))
        if do_a:
            m = jnp.max(mv_scr[...], axis=-1, keepdims=True)
            mb_scr[...] = jnp.broadcast_to(m, mb_scr.shape)
        if do_c:
            o = _finish_head(oacc_scr[:, :V_DIM], oacc_scr[:, V_DIM:], lam, sg_ref[...], tq)
            o_ref[0, :, cols_c] = o.astype(o_ref.dtype)


def _attn_call(q, k, v, diff_lambda, subln_g, *, tq, cache=None):
    b, sq, _ = q.shape
    qmap = lambda bi, qi: (bi, qi, 0)
    seqmap = lambda bi, qi: (bi, 0, 0)
    const2 = lambda bi, qi: (0, 0)
    nseq = 1 if cache is not None else SEQS_PER_STEP
    in_specs = [pl.BlockSpec((nseq, tq, D_ATTN), qmap)]
    args = [q, k, v]
    if cache is not None:
        sk = k.shape[1]
        in_specs += [pl.BlockSpec((1, sk, D_ATTN), seqmap, pipeline_mode=pl.Buffered(1))] * 2
        kc, vc = cache
        past = kc.shape[1]
        kt = 1024
        rows = 2 * tq
        body = functools.partial(_attn_pipe_kernel, tq=tq, kt=kt)
        in_specs += [pl.BlockSpec((1, past, D_ATTN), seqmap)] * 2
        args += [kc, vc]
        scratch = ([pltpu.VMEM((sk // kt, rows, kt), F32)] * 2 + [pltpu.VMEM((rows, past), F32)] * 2
                   + [pltpu.VMEM((sk // kt, rows, kt), BF16)] * 2 + [pltpu.VMEM((rows, past), BF16)] * 2
                   + [pltpu.VMEM((rows, V_DIM), BF16)] + [pltpu.VMEM((rows, LANES), F32)] * 2
                   + [pltpu.VMEM((rows, 2 * V_DIM), F32)])
    else:
        assert sq == tq and b % nseq == 0
        body = functools.partial(_attn_small_kernel, tq=tq)
        in_specs += [pl.BlockSpec((nseq * (k.shape[0] // b), V_DIM), lambda bi, qi: (bi, 0))] * 2
        scratch = []
    in_specs += [pl.BlockSpec(diff_lambda.shape, const2), pl.BlockSpec(subln_g.shape, const2)]
    args += [diff_lambda, subln_g]
    return pl.pallas_call(
        body,
        grid=(b // nseq, sq // tq),
        in_specs=in_specs,
        out_specs=pl.BlockSpec((nseq, tq, D_ATTN), qmap),
        out_shape=jax.ShapeDtypeStruct((b, sq, D_ATTN), BF16),
        scratch_shapes=scratch,
        compiler_params=pltpu.CompilerParams(
            dimension_semantics=("arbitrary", "arbitrary"), vmem_limit_bytes=VMEM_LIMIT),
        name="attn",
    )(*args)


def _scan8(a, b, reverse):
    sub = lax.broadcasted_iota(jnp.int32, a.shape, 1)
    for d in (1, 2, 4):
        if reverse:
            shift, m = SUBLANES - d, sub < SUBLANES - d
        else:
            shift, m = d, sub >= d
        a_s = pltpu.roll(a, shift, axis=1)
        b_s = pltpu.roll(b, shift, axis=1)
        b = jnp.where(m, a * b_s + b, b)
        a = jnp.where(m, a * a_s, a)
    return a, b


def _scan_block(a, b, carry, reverse):
    tb, l = a.shape
    g = tb // SUBLANES
    a3, b3 = _scan8(a.reshape(g, SUBLANES, l), b.reshape(g, SUBLANES, l), reverse)
    hs = [None] * g
    order = range(g - 1, -1, -1) if reverse else range(g)
    for gi in order:
        hg = a3[gi] * carry + b3[gi]
        hs[gi] = hg
        carry = hg[0:1] if reverse else hg[SUBLANES - 1:SUBLANES]
    return jnp.concatenate(hs, axis=0), carry


def _softplus(x):
    return jnp.maximum(x, 0.0) + jnp.log(1.0 + jnp.exp(-jnp.abs(x)))


def _gelu_tanh(x):
    return 0.5 * x * (1.0 + jnp.tanh(math.sqrt(2.0 / math.pi) * (x + 0.044715 * (x * x * x))))


def _rglru_kernel(xr_ref, xg_ref, h0_ref, cw_ref, cb_ref, wg_ref, bg_ref, lam_ref,
                  y_ref, hl_ref, xpad, ab, bb, hf, *, s, cc, tb):
    nblk = s // tb
    nsub = cc // LANES
    cbase = pl.program_id(1) * nsub
    for j in range(nsub):
        ls = slice(j * LANES, (j + 1) * LANES)
        xpad[0:SUBLANES, ls] = jnp.zeros((SUBLANES, LANES), F32)
        xpad[s + SUBLANES:s + 2 * SUBLANES, ls] = jnp.zeros((SUBLANES, LANES), F32)
        xpad[SUBLANES:s + SUBLANES, ls] = xr_ref[0, :, ls]
        sp_f = RG_C * _softplus(-lam_ref[0:1, ls])
        sp_b = RG_C * _softplus(-lam_ref[1:2, ls])
        wg = wg_ref[cbase + j]
        bg = bg_ref[:, ls]
        cw = cw_ref[:, ls]
        cb = cb_ref[:, ls]

        def gate_ab(pre_r, pre_i, sp, xc):
            r = 0.5 * jnp.tanh(0.5 * pre_r) + 0.5
            i = 0.5 * jnp.tanh(0.5 * pre_i) + 0.5
            z = r * sp
            a = jnp.exp2(r * (sp * (-LOG2_E)))
            bx = jnp.sqrt((1.0 + a * a) * jnp.tanh(z)) * (i * xc)
            return a, bx

        def fwd_block(t, carry):
            t0 = pl.multiple_of(t * tb, tb)
            xw = xpad[pl.ds(t0, tb + 2 * SUBLANES), ls]
            xc = cb
            for tap in range(CONV_W):
                lo = SUBLANES - CONV_PAD_LEFT + tap
                xc = xc + xw[lo:lo + tb] * cw[tap:tap + 1]
            pre = jnp.dot(xc.astype(BF16), wg, preferred_element_type=F32)
            a_f, b_f = gate_ab(pre[:, 0:128] + bg[0:1], pre[:, 128:256] + bg[1:2], sp_f, xc)
            a_b, b_b = gate_ab(pre[:, 256:384] + bg[2:3], pre[:, 384:512] + bg[3:4], sp_b, xc)
            ab[pl.ds(t0, tb), ls] = a_b
            bb[pl.ds(t0, tb), ls] = b_b
            h, carry = _scan_block(a_f, b_f, carry, False)
            hf[pl.ds(t0, tb), ls] = h
            return carry

        hl_ref[0, 0:1, ls] = lax.fori_loop(0, nblk, fwd_block, h0_ref[0, 0:1, ls])

        def bwd_block(tt, carry):
            t0 = pl.multiple_of((nblk - 1 - tt) * tb, tb)
            h, carry = _scan_block(ab[pl.ds(t0, tb), ls], bb[pl.ds(t0, tb), ls], carry, True)
            y = (hf[pl.ds(t0, tb), ls] + h) * _gelu_tanh(xg_ref[0, pl.ds(t0, tb), ls])
            y_ref[0, pl.ds(t0, tb), ls] = y.astype(y_ref.dtype)
            return carry

        hl_ref[0, 1:2, ls] = lax.fori_loop(0, nblk, bwd_block, h0_ref[0, 1:2, ls])


def _rglru_call(xr, xg, h0, cw, cb, wg, bg, lam, *, cc):
    b, s, _ = xr.shape
    nch = D_RG // cc
    tb = 256
    seq = pl.BlockSpec((1, s, cc), lambda bi, ci: (bi, 0, ci))
    st = pl.BlockSpec((1, 2, cc), lambda bi, ci: (bi, 0, ci))
    return pl.pallas_call(
        functools.partial(_rglru_kernel, s=s, cc=cc, tb=tb),
        grid=(b, nch),
        in_specs=[seq, seq, st,
                  pl.BlockSpec((CONV_W, cc), lambda bi, ci: (0, ci)),
                  pl.BlockSpec((1, cc), lambda bi, ci: (0, ci)),
                  pl.BlockSpec(wg.shape, lambda bi, ci: (0, 0, 0)),
                  pl.BlockSpec((4, cc), lambda bi, ci: (0, ci)),
                  pl.BlockSpec((2, cc), lambda bi, ci: (0, ci))],
        out_specs=[seq, st],
        out_shape=[jax.ShapeDtypeStruct((b, s, D_RG), BF16),
                   jax.ShapeDtypeStruct((b, 2, D_RG), F32)],
        scratch_shapes=[pltpu.VMEM((s + 2 * SUBLANES, cc), F32),
                        pltpu.VMEM((s, cc), F32), pltpu.VMEM((s, cc), F32),
                        pltpu.VMEM((s, cc), F32)],
        compiler_params=pltpu.CompilerParams(
            dimension_semantics=("arbitrary", "arbitrary"), vmem_limit_bytes=VMEM_LIMIT),
        name="rglru",
    )(xr, xg, h0, cw, cb, wg, bg, lam)


def _rope_tables(s):
    pos = np.arange(s)
    row = (pos // GRID_W).astype(np.float32)
    col = (pos % GRID_W).astype(np.float32)
    n_freq = HEAD_DIM // 4
    inv_freq = (ROPE_BASE ** (-np.arange(n_freq, dtype=np.float32) / n_freq)).astype(np.float32)
    p = np.arange(LANES) % HEAD_DIM
    freq = inv_freq[p % n_freq]
    ang = (np.where((p < HEAD_DIM // 2)[None, :], row[:, None], col[:, None])
           * freq[None, :]).astype(np.float32)
    sign = np.where((p % (2 * n_freq)) < n_freq, -1.0, 1.0)
    cos = np.cos(ang.astype(np.float64)).astype(np.float32)
    sin = (np.sin(ang.astype(np.float64)) * sign[None, :]).astype(np.float32)
    return jnp.asarray(cos), jnp.asarray(sin)


def _gate_weights(w_r, w_i):
    def bd(w):
        w = w.reshape(4, 2, RG_BLOCK_W, RG_BLOCK_W)
        z = jnp.zeros_like(w[:, 0])
        top = jnp.concatenate([w[:, 0], z], axis=2)
        bot = jnp.concatenate([z, w[:, 1]], axis=2)
        return jnp.concatenate([top, bot], axis=1)
    return jnp.concatenate([bd(w_r[0]), bd(w_i[0]), bd(w_r[1]), bd(w_i[1])], axis=2).astype(BF16)


def kernel(x_prompt, x_sample, cache_attn_k, cache_attn_v, state_rglru, c, c_ctx, norm_g, w_mod, b_mod, ffn_w_gate, ffn_w_up, ffn_w_down, w_in, w_out, diff_lambda, subln_g, conv_w, conv_b, rg_w_r, rg_b_r, rg_w_i, rg_b_i, rg_lambda, final_g):
    l = 0
    bsz, seq, _ = x_prompt.shape
    dbsz, dseq, _ = x_sample.shape
    past = cache_attn_k.shape[2]

    ng = norm_g[l]
    wgate = ffn_w_gate[l].astype(BF16)
    wup = ffn_w_up[l].astype(BF16)
    wd = ffn_w_down[l].astype(BF16)
    win = w_in[l].astype(BF16)
    wout = w_out[l].astype(BF16)
    wg = _gate_weights(rg_w_r[l], rg_w_i[l])
    bg = jnp.stack([rg_b_r[l, 0], rg_b_i[l, 0], rg_b_r[l, 1], rg_b_i[l, 1]])
    cw = conv_w[l]
    cb = conv_b[l][None, :]
    lam = rg_lambda[l]
    dl = diff_lambda[l]
    sg = subln_g[l][None, :]
    fg = final_g[None, :]

    c8 = jnp.concatenate([c_ctx[None, :], c, jnp.zeros((SUBLANES - 1 - dbsz, D_MODEL), F32)], axis=0)
    mod3 = _mod_call(c8, w_mod[l], b_mod[l][None, :]).reshape(SUBLANES, N_MOD, D_MODEL)

    def layer(x, *, rows_per_mod, mod_base, nb, s, k_ctx, v_ctx, h0, cc, tq, rope_tabs):
        tm = 512
        tm_ffn = 1024
        x1 = _ffn_call(x, mod3, ng, wgate, wup, wd, sub=0, rows_per_mod=rows_per_mod,
                       mod_base=mod_base, tm=tm_ffn)
        q, k, v, xr, xg = _proj_call(x1, mod3, ng, win, rows_per_mod=rows_per_mod,
                                     mod_base=mod_base, tm=tm, rope_tabs=rope_tabs)
        q3 = q.reshape(nb, s, D_ATTN)
        if k_ctx is None:
            o = _attn_call(q3, k, v, dl, sg, tq=tq)
        else:
            o = _attn_call(q3, k.reshape(nb, s, D_ATTN), v.reshape(nb, s, D_ATTN), dl, sg, tq=tq,
                           cache=(k_ctx, v_ctx))
        rg, h_last = _rglru_call(xr.reshape(nb, s, D_RG), xg.reshape(nb, s, D_RG), h0,
                                 cw, cb, wg, bg, lam, cc=cc)
        y = _ffn_call(x1, mod3, ng, wgate, wup, wd, sub=2, rows_per_mod=rows_per_mod,
                      mod_base=mod_base, tm=tm_ffn,
                      mix=(o.reshape(nb * s, D_ATTN), rg.reshape(nb * s, D_RG), wout),
                      final_g=fg)
        return y, k, v, h_last

    yp, k_new, v_new, h_new = layer(
        x_prompt.reshape(bsz * seq, D_MODEL), rows_per_mod=bsz * seq, mod_base=0, nb=bsz, s=seq,
        k_ctx=None, v_ctx=None, h0=jnp.zeros((bsz, 2, D_RG), F32), cc=D_RG, tq=seq,
        rope_tabs=None)
    ys, _, _, _ = layer(
        x_sample.reshape(dbsz * dseq, D_MODEL), rows_per_mod=dseq, mod_base=1, nb=dbsz, s=dseq,
        k_ctx=cache_attn_k[:, l].reshape(dbsz, past, D_ATTN).astype(BF16),
        v_ctx=cache_attn_v[:, l].reshape(dbsz, past, D_ATTN).astype(BF16),
        h0=state_rglru[:, l], cc=LANES, tq=256, rope_tabs=_rope_tables(dseq))

    return (yp.reshape(bsz, seq, D_MODEL),
            ys.reshape(dbsz, dseq, D_MODEL),
            k_new.reshape(bsz, 1, seq, N_HEADS, V_DIM),
            v_new.reshape(bsz, 1, seq, N_HEADS, V_DIM),
            h_new.reshape(bsz, 1, 2, D_RG))
```

```python
import functools
import math

import jax
import jax.numpy as jnp
import numpy as np
from jax import lax
from jax.experimental import pallas as pl
from jax.experimental.pallas import tpu as pltpu

F32 = jnp.float32
BF16 = jnp.bfloat16

D_MODEL = 1024
N_HEADS = 4
HEAD_DIM = 64
V_DIM = 2 * HEAD_DIM
D_ATTN = N_HEADS * V_DIM
D_RG = 512
RG_BLOCK_W = 64
RG_C = 8.0
CONV_W = 4
CONV_PAD_LEFT = 2
D_FF = 2816
N_MOD = 9
GRID_W = 64
ROPE_BASE = 10000.0
EPS = 1e-6
LAM_INIT = 0.8 - 0.6 * math.exp(-0.3 * 0)
LOG2_E = math.log2(math.e)

LANES = 128
SUBLANES = 8
MXU_N = 256
S_BUFS = 2
SEQS_PER_STEP = 2
SQRT_FLOOR = 1e-30
SCAN_ROWS = SUBLANES * SUBLANES
EXP_ROWS = 64
FF_CHUNK = MXU_N
N_FF_CHUNKS = D_FF // FF_CHUNK
VMEM_LIMIT = 56 * 1024 * 1024


def _sigmoid(x):
    return 1.0 / (1.0 + jnp.exp(-x))


def _gelu_tanh(x):
    return 0.5 * x * (1.0 + jnp.tanh(math.sqrt(2.0 / math.pi) * (x + 0.044715 * (x * x * x))))


def _rms(x, g):
    ms = jnp.mean(x * x, axis=-1, keepdims=True)
    return x * lax.rsqrt(ms + EPS) * g


def _mod_kernel(c_ref, w_ref, b_ref, o_ref):
    c = c_ref[...]
    s = (c * _sigmoid(c)).astype(BF16)
    o_ref[...] = jnp.dot(s, w_ref[...].astype(BF16), preferred_element_type=F32) + b_ref[...]


def _mod_call(c8, w_mod, b_mod):
    n = w_mod.shape[1]
    tn = 1536
    return pl.pallas_call(
        _mod_kernel,
        grid=(n // tn,),
        in_specs=[
            pl.BlockSpec((SUBLANES, D_MODEL), lambda j: (0, 0)),
            pl.BlockSpec((D_MODEL, tn), lambda j: (0, j)),
            pl.BlockSpec((1, tn), lambda j: (0, j)),
        ],
        out_specs=pl.BlockSpec((SUBLANES, tn), lambda j: (0, j)),
        out_shape=jax.ShapeDtypeStruct((SUBLANES, n), F32),
        compiler_params=pltpu.CompilerParams(
            dimension_semantics=("arbitrary",), vmem_limit_bytes=VMEM_LIMIT),
        name="mod",
    )(c8, w_mod, b_mod)


def _ffn_kernel(*refs, sub, fuse_mix, final_norm):
    it = iter(refs)
    x_ref = next(it)
    if fuse_mix:
        o_ref_in = next(it)
        rg_ref = next(it)
        wout_ref = next(it)
    mod_ref = next(it)
    ng_ref = next(it)
    wg_ref = next(it)
    wu_ref = next(it)
    wd_ref = next(it)
    fg_ref = next(it) if final_norm else None
    out_ref = next(it)
    acc_ref = out_ref
    h_ref = next(it)
    gu0_ref = next(it)
    gu1_ref = next(it)

    x = x_ref[...]
    if fuse_mix:
        mix = jnp.dot(o_ref_in[...], wout_ref[0:D_ATTN, :], preferred_element_type=F32)
        mix = mix + jnp.dot(rg_ref[...], wout_ref[D_ATTN:, :], preferred_element_type=F32)
        x = x + mod_ref[0, 5:6, :] * mix
    sh = mod_ref[0, 3 * sub:3 * sub + 1, :]
    sc = mod_ref[0, 3 * sub + 1:3 * sub + 2, :]
    gate = mod_ref[0, 3 * sub + 2:3 * sub + 3, :]
    h_ref[...] = (_rms(x, ng_ref[sub:sub + 1, :]) * (1.0 + sc) + sh).astype(BF16)

    def chunk(j):
        return pl.ds(pl.multiple_of(j * FF_CHUNK, FF_CHUNK), FF_CHUNK)

    def gate_up(j):
        h = h_ref[...]
        return jnp.concatenate(
            [jnp.dot(h, wg_ref[0, :, chunk(j)], preferred_element_type=F32),
             jnp.dot(h, wu_ref[0, :, chunk(j)], preferred_element_type=F32)], axis=1)

    def down(j, gu):
        g = gu[:, :FF_CHUNK]
        u = gu[:, FF_CHUNK:]
        a = (g * _sigmoid(g) * u).astype(BF16)
        return jnp.dot(a, wd_ref[0, chunk(j), :], preferred_element_type=F32)

    gu0_ref[...] = gate_up(0)
    gu1_ref[...] = gate_up(1)
    acc_ref[...] = down(0, gu0_ref[...])

    def body(i, carry):
        gu0_ref[...] = gate_up(2 * i + 2)
        acc_ref[...] += down(2 * i + 1, gu1_ref[...])
        gu1_ref[...] = gate_up(2 * i + 3)
        acc_ref[...] += down(2 * i + 2, gu0_ref[...])
        return carry

    assert N_FF_CHUNKS % 2 == 1 and N_FF_CHUNKS >= 5
    lax.fori_loop(0, (N_FF_CHUNKS - 3) // 2, body, 0, unroll=2)
    gu0_ref[...] = gate_up(N_FF_CHUNKS - 1)
    acc_ref[...] += down(N_FF_CHUNKS - 2, gu1_ref[...])
    y = x + (0.5 * gate) * (acc_ref[...] + down(N_FF_CHUNKS - 1, gu0_ref[...]))
    if final_norm:
        y = _rms(y, fg_ref[...])
    out_ref[...] = y


def _ffn_call(x, mod3, ng, wg, wu, wd, *, sub, rows_per_mod, mod_base, tm,
              mix=None, final_g=None):
    t = x.shape[0]
    ffn_idx = sub // 2
    fuse_mix = mix is not None
    final_norm = final_g is not None
    tiles_per_mod = rows_per_mod // tm

    def row_map(i):
        return (i, 0)

    def mod_map(i):
        return (mod_base + i // tiles_per_mod, 0, 0)

    const2 = lambda i: (0, 0)
    ffn_map = lambda i: (ffn_idx, 0, 0)
    in_specs = [pl.BlockSpec((tm, D_MODEL), row_map)]
    args = [x]
    if fuse_mix:
        o, rg, wout = mix
        in_specs += [pl.BlockSpec((tm, D_ATTN), row_map),
                     pl.BlockSpec((tm, D_RG), row_map),
                     pl.BlockSpec(wout.shape, const2, pipeline_mode=pl.Buffered(1))]
        args += [o, rg, wout]
    in_specs += [pl.BlockSpec((1, N_MOD, D_MODEL), mod_map),
                 pl.BlockSpec(ng.shape, const2),
                 pl.BlockSpec((1,) + wg.shape[1:], ffn_map, pipeline_mode=pl.Buffered(1)),
                 pl.BlockSpec((1,) + wu.shape[1:], ffn_map, pipeline_mode=pl.Buffered(1)),
                 pl.BlockSpec((1,) + wd.shape[1:], ffn_map, pipeline_mode=pl.Buffered(1))]
    args += [mod3, ng, wg, wu, wd]
    if final_norm:
        in_specs.append(pl.BlockSpec((1, D_MODEL), const2))
        args.append(final_g)
    return pl.pallas_call(
        functools.partial(_ffn_kernel, sub=sub, fuse_mix=fuse_mix, final_norm=final_norm),
        grid=(t // tm,),
        in_specs=in_specs,
        out_specs=pl.BlockSpec((tm, D_MODEL), row_map),
        out_shape=jax.ShapeDtypeStruct((t, D_MODEL), F32),
        scratch_shapes=[pltpu.VMEM((tm, D_MODEL), BF16),
                        pltpu.VMEM((tm, 2 * FF_CHUNK), F32), pltpu.VMEM((tm, 2 * FF_CHUNK), F32)],
        compiler_params=pltpu.CompilerParams(
            dimension_semantics=("arbitrary",), vmem_limit_bytes=VMEM_LIMIT),
        name="ffn%d" % sub,
    )(*args)


def _rope(x, cos, sin_signed, first_half):
    outs = []
    for cblk in range(x.shape[1] // LANES):
        xs = x[:, cblk * LANES:(cblk + 1) * LANES]
        partner = jnp.where(first_half, pltpu.roll(xs, LANES - 16, axis=1),
                            pltpu.roll(xs, 16, axis=1))
        outs.append(xs * cos + partner * sin_signed)
    return jnp.concatenate(outs, axis=1)


def _proj_kernel(*refs, rope):
    it = iter(refs)
    x_ref = next(it)
    mod_ref = next(it)
    ng_ref = next(it)
    win_ref = next(it)
    if rope:
        cos_ref = next(it)
        sin_ref = next(it)
    q_ref, k_ref, v_ref, xr_ref, xg_ref = it

    x = x_ref[...]
    sh = mod_ref[0, 3:4, :]
    sc = mod_ref[0, 4:5, :]
    h = (_rms(x, ng_ref[1:2, :]) * (1.0 + sc) + sh).astype(BF16)

    def col(j):
        return jnp.dot(h, win_ref[:, j * D_ATTN:(j + 1) * D_ATTN], preferred_element_type=F32)

    q = col(0)
    k = col(1)
    if rope:
        cos = cos_ref[...]
        sin = sin_ref[...]
        lane = lax.broadcasted_iota(jnp.int32, (1, LANES), 1)
        first_half = (lane % 32) < 16
        q = _rope(q, cos, sin, first_half)
        k = _rope(k, cos, sin, first_half)
    q_ref[...] = (q * (HEAD_DIM ** -0.5 * LOG2_E)).astype(q_ref.dtype)
    v = col(2)
    if rope:
        k_ref[...] = k.astype(k_ref.dtype)
        v_ref[...] = v.astype(v_ref.dtype)
    else:
        tm = x.shape[0]
        for hd in range(N_HEADS):
            k_ref[pl.ds(hd, tm, stride=N_HEADS), :] = k[:, hd * V_DIM:(hd + 1) * V_DIM]
            v_ref[pl.ds(hd, tm, stride=N_HEADS), :] = v[:, hd * V_DIM:(hd + 1) * V_DIM]
    xr_ref[...] = col(3)
    xg_ref[...] = col(4)


def _proj_call(x, mod3, ng, win, *, rows_per_mod, mod_base, tm, rope_tabs=None):
    t = x.shape[0]
    rope = rope_tabs is not None
    tiles_per_mod = rows_per_mod // tm
    row_map = lambda i: (i, 0)
    const2 = lambda i: (0, 0)
    in_specs = [pl.BlockSpec((tm, D_MODEL), row_map),
                pl.BlockSpec((1, N_MOD, D_MODEL), lambda i: (mod_base + i // tiles_per_mod, 0, 0)),
                pl.BlockSpec(ng.shape, const2),
                pl.BlockSpec(win.shape, const2, pipeline_mode=pl.Buffered(1))]
    args = [x, mod3, ng, win]
    if rope:
        cos, sin = rope_tabs
        tiles_per_seq = cos.shape[0] // tm
        tab_map = lambda i: (i % tiles_per_seq, 0)
        in_specs += [pl.BlockSpec((tm, LANES), tab_map), pl.BlockSpec((tm, LANES), tab_map)]
        args += [cos, sin]
    half = pl.BlockSpec((tm, D_ATTN), row_map)
    if rope:
        kv_spec, kv_shape = half, jax.ShapeDtypeStruct((t, D_ATTN), BF16)
    else:
        kv_spec = pl.BlockSpec((tm * N_HEADS, V_DIM), row_map)
        kv_shape = jax.ShapeDtypeStruct((t * N_HEADS, V_DIM), F32)
    return pl.pallas_call(
        functools.partial(_proj_kernel, rope=rope),
        grid=(t // tm,),
        in_specs=in_specs,
        out_specs=[half, kv_spec, kv_spec, half, half],
        out_shape=[jax.ShapeDtypeStruct((t, D_ATTN), BF16),
                   kv_shape,
                   kv_shape,
                   jax.ShapeDtypeStruct((t, D_RG), F32),
                   jax.ShapeDtypeStruct((t, D_RG), F32)],
        compiler_params=pltpu.CompilerParams(
            dimension_semantics=("arbitrary",), vmem_limit_bytes=VMEM_LIMIT),
        name="proj",
    )(*args)


def _diff_lambda(dl_ref):
    dl = dl_ref[...]
    return (jnp.exp(jnp.sum(dl[0:1] * dl[1:2], axis=-1, keepdims=True))
            - jnp.exp(jnp.sum(dl[2:3] * dl[3:4], axis=-1, keepdims=True)) + LAM_INIT)


def _stack_maps(q):
    map0 = lax.broadcasted_iota(jnp.int32, (1, V_DIM), 1) < HEAD_DIM
    zero = jnp.zeros_like(q)
    return jnp.concatenate([jnp.where(map0, q, zero), jnp.where(map0, zero, q)], axis=0)


def _finish_head(o2, l, lam, sg, tq):
    o = o2[:tq] * (1.0 / l[:tq]) - o2[tq:] * (lam / l[tq:])
    return _rms(o, sg) * (1.0 - LAM_INIT)


def _attn_small_kernel(q_ref, k_ref, v_ref, dl_ref, sg_ref, o_ref, *, tq):
    lam = _diff_lambda(dl_ref)
    nseq = q_ref.shape[0]
    sk = k_ref.shape[0] // (N_HEADS * nseq)
    ones = jnp.ones((sk, V_DIM), BF16)
    for bi in range(nseq):
        for hd in range(N_HEADS):
            cols = slice(hd * V_DIM, (hd + 1) * V_DIM)
            head_rows = pl.ds(bi * sk * N_HEADS + hd, sk, stride=N_HEADS)
            q2 = _stack_maps(q_ref[bi, :, cols])
            k = k_ref[head_rows, :].astype(BF16)
            s = lax.dot_general(q2, k, (((1,), (1,)), ((), ())), preferred_element_type=F32)
            e = jnp.exp2(s - jnp.max(s, axis=-1, keepdims=True))
            v1 = jnp.concatenate([v_ref[head_rows, :].astype(BF16), ones], axis=1)
            o2 = jnp.dot(e.astype(BF16), v1, preferred_element_type=F32)
            o = _finish_head(o2[:, :V_DIM], o2[:, V_DIM:], lam, sg_ref[...], tq)
            o_ref[bi, :, cols] = o.astype(o_ref.dtype)


def _attn_pipe_kernel(q_ref, k_ref, v_ref, kc_ref, vc_ref, dl_ref, sg_ref, o_ref, *scratch, tq, kt):
    lam = _diff_lambda(dl_ref)
    s_bufs, scratch = scratch[:S_BUFS], scratch[S_BUFS:]
    sc_bufs, scratch = scratch[:S_BUFS], scratch[S_BUFS:]
    p_bufs, pc_bufs = scratch[0:2], scratch[2:4]
    q2_scr, mv_scr, mb_scr, oacc_scr = scratch[4:]
    n_steps = k_ref.shape[1] // kt
    for t in range(N_HEADS + 2):
        ha, hb, hc = t, t - 1, t - 2
        do_a, do_b, do_c = 0 <= ha < N_HEADS, 0 <= hb < N_HEADS, 0 <= hc < N_HEADS
        cols_a = slice(ha * V_DIM, (ha + 1) * V_DIM)
        cols_c = slice(hc * V_DIM, (hc + 1) * V_DIM)
        if do_a:
            q2_scr[...] = _stack_maps(q_ref[0, :, cols_a])
            mv_scr[...] = jnp.full(mv_scr.shape, -jnp.inf, F32)
        if do_c:
            oacc_scr[...] = jnp.zeros(oacc_scr.shape, F32)

        def key_block(k_blk, v_blk, s_a, s_b, p_b, p_c):
            whole = (slice(None), slice(None))
            if do_a:
                s_ref, s_idx = s_a
                kb = k_blk()
                for c0 in range(0, kb.shape[0], MXU_N):
                    s = lax.dot_general(q2_scr[...], kb[c0:c0 + MXU_N], (((1,), (1,)), ((), ())),
                                        preferred_element_type=F32)
                    s_ref[s_idx + (slice(None), slice(c0, c0 + MXU_N))] = s
                    mv_scr[...] = jnp.maximum(mv_scr[...],
                                              jnp.maximum(s[:, :LANES], s[:, LANES:]))
            if do_b:
                s_ref, s_idx = s_b
                p_ref, p_idx = p_b
                n_keys = s_ref.shape[-1]
                for r0 in range(0, 2 * tq, EXP_ROWS):
                    rows = slice(r0, r0 + EXP_ROWS)
                    mb = mb_scr[rows, :]
                    for c0 in range(0, n_keys, LANES):
                        at = (rows, slice(c0, c0 + LANES))
                        p_ref[p_idx + at] = jnp.exp2(s_ref[s_idx + at] - mb).astype(BF16)
            if do_c:
                v = v_blk()
                v1 = jnp.concatenate([v, jnp.ones(v.shape, BF16)], axis=1)
                oacc_scr[...] += jnp.dot(p_c[0][p_c[1] + whole], v1, preferred_element_type=F32)

        def new_keys_step(j, carry):
            keys = pl.ds(pl.multiple_of(j * kt, kt), kt)
            key_block(lambda: k_ref[0, keys, cols_a], lambda: v_ref[0, keys, cols_c],
                      (s_bufs[ha % S_BUFS], (j,)), (s_bufs[hb % S_BUFS], (j,)),
                      (p_bufs[hb % 2], (j,)), (p_bufs[hc % 2], (j,)))
            return carry

        lax.fori_loop(0, n_steps, new_keys_step, 0, unroll=True)
        key_block(lambda: kc_ref[0, :, cols_a], lambda: vc_ref[0, :, cols_c],
                  (sc_bufs[ha % S_BUFS], ()), (sc_bufs[hb % S_BUFS], ()),
                  (pc_bufs[hb % 2], ()), (pc_bufs[hc % 2], ()))
        if do_a:
            m = jnp.max(mv_scr[...], axis=-1, keepdims=True)
            mb_scr[...] = jnp.broadcast_to(m, mb_scr.shape)
        if do_c:
            o = _finish_head(oacc_scr[:, :V_DIM], oacc_scr[:, V_DIM:], lam, sg_ref[...], tq)
            o_ref[0, :, cols_c] = o.astype(o_ref.dtype)


def _attn_call(q, k, v, diff_lambda, subln_g, *, tq, cache=None):
    b, sq, _ = q.shape
    qmap = lambda bi, qi: (bi, qi, 0)
    seqmap = lambda bi, qi: (bi, 0, 0)
    const2 = lambda bi, qi: (0, 0)
    nseq = 1 if cache is not None else SEQS_PER_STEP
    in_specs = [pl.BlockSpec((nseq, tq, D_ATTN), qmap)]
    args = [q, k, v]
    if cache is not None:
        sk = k.shape[1]
        in_specs += [pl.BlockSpec((1, sk, D_ATTN), seqmap, pipeline_mode=pl.Buffered(1))] * 2
        kc, vc = cache
        past = kc.shape[1]
        kt = 1024
        rows = 2 * tq
        body = functools.partial(_attn_pipe_kernel, tq=tq, kt=kt)
        in_specs += [pl.BlockSpec((1, past, D_ATTN), seqmap)] * 2
        args += [kc, vc]
        scratch = ([pltpu.VMEM((sk // kt, rows, kt), F32)] * S_BUFS
                   + [pltpu.VMEM((rows, past), F32)] * S_BUFS
                   + [pltpu.VMEM((sk // kt, rows, kt), BF16)] * 2 + [pltpu.VMEM((rows, past), BF16)] * 2
                   + [pltpu.VMEM((rows, V_DIM), BF16)] + [pltpu.VMEM((rows, LANES), F32)] * 2
                   + [pltpu.VMEM((rows, 2 * V_DIM), F32)])
    else:
        assert sq == tq and b % nseq == 0
        body = functools.partial(_attn_small_kernel, tq=tq)
        in_specs += [pl.BlockSpec((nseq * (k.shape[0] // b), V_DIM), lambda bi, qi: (bi, 0))] * 2
        scratch = []
    in_specs += [pl.BlockSpec(diff_lambda.shape, const2), pl.BlockSpec(subln_g.shape, const2)]
    args += [diff_lambda, subln_g]
    return pl.pallas_call(
        body,
        grid=(b // nseq, sq // tq),
        in_specs=in_specs,
        out_specs=pl.BlockSpec((nseq, tq, D_ATTN), qmap),
        out_shape=jax.ShapeDtypeStruct((b, sq, D_ATTN), BF16),
        scratch_shapes=scratch,
        compiler_params=pltpu.CompilerParams(
            dimension_semantics=("arbitrary", "arbitrary"), vmem_limit_bytes=VMEM_LIMIT),
        name="attn",
    )(*args)


def _scan8(a, b, reverse):
    sub = lax.broadcasted_iota(jnp.int32, a.shape, 1)
    for d in (1, 2, 4):
        if reverse:
            shift, m = SUBLANES - d, sub < SUBLANES - d
        else:
            shift, m = d, sub >= d
        a_s = pltpu.roll(a, shift, axis=1)
        b_s = pltpu.roll(b, shift, axis=1)
        b = jnp.where(m, a * b_s + b, b)
        a = jnp.where(m, a * a_s, a)
    return a, b


def _scan_tile(load_a, load_b, store_h, carry, reverse):
    order = range(SUBLANES - 1, -1, -1) if reverse else range(SUBLANES)
    hs, ps = {}, {}
    h = p = None
    for s in order:
        a, b = load_a(s), load_b(s)
        h = b if h is None else a * h + b
        p = a if p is None else a * p
        hs[s], ps[s] = h, p
    pp, hh = _scan8(p[None], h[None], reverse)
    after = hh[0] + pp[0] * carry
    sub = lax.broadcasted_iota(jnp.int32, after.shape, 0)
    if reverse:
        before = jnp.where(sub == SUBLANES - 1, carry, pltpu.roll(after, SUBLANES - 1, axis=0))
        carry = after[0:1]
    else:
        before = jnp.where(sub == 0, carry, pltpu.roll(after, 1, axis=0))
        carry = after[SUBLANES - 1:SUBLANES]
    for s in order:
        store_h(s, hs[s] + ps[s] * before)
    return carry


def _scan_block(a_ref, b_ref, h_ref, slab, t0, tb, carry, reverse):
    tiles = range(tb // SCAN_ROWS)
    for q in (reversed(tiles) if reverse else tiles):
        def rows(s, q=q):
            return pl.ds(t0 + q * SCAN_ROWS + s, SUBLANES, stride=SUBLANES)

        def store_h(s, h, rows=rows):
            h_ref[slab, rows(s), :] = h

        carry = _scan_tile(lambda s, rows=rows: a_ref[slab, rows(s), :],
                           lambda s, rows=rows: b_ref[slab, rows(s), :],
                           store_h, carry, reverse)
    return carry


def _softplus(x):
    return jnp.maximum(x, 0.0) + jnp.log(1.0 + jnp.exp(-jnp.abs(x)))


def _rglru_kernel(xr_ref, xg_ref, h0_ref, cw_ref, cb_ref, wg_ref, bg_ref, lam_ref,
                  y_ref, hl_ref, xpad, ab, bb, af, hf, *, s, cc, tb):
    nblk = s // tb
    nsub = cc // LANES
    cbase = pl.program_id(1) * nsub
    for j in range(nsub):
        ls = slice(j * LANES, (j + 1) * LANES)
        xpad[0:SUBLANES, ls] = jnp.zeros((SUBLANES, LANES), F32)
        xpad[s + SUBLANES:s + 2 * SUBLANES, ls] = jnp.zeros((SUBLANES, LANES), F32)
        xpad[SUBLANES:s + SUBLANES, ls] = xr_ref[0, :, ls]
        sp_f = (0.5 * RG_C) * _softplus(-lam_ref[0:1, ls])
        sp_b = (0.5 * RG_C) * _softplus(-lam_ref[1:2, ls])
        wg = wg_ref[cbase + j]
        bg = bg_ref[:, ls]
        cw = cw_ref[:, ls]
        cb = cb_ref[:, ls]

        def gate_ab(half_pre_r, half_pre_i, half_sp, half_xc):
            z = jnp.tanh(half_pre_r) * half_sp + half_sp
            ixc = jnp.tanh(half_pre_i) * half_xc + half_xc
            a = jnp.exp2(z * (-LOG2_E))
            u = (1.0 + a * a) * jnp.tanh(z)
            bx = u * lax.rsqrt(jnp.maximum(u, SQRT_FLOOR)) * ixc
            return a, bx

        def fwd_block(t, carry):
            t0 = pl.multiple_of(t * tb, tb)
            xw = xpad[pl.ds(t0, tb + 2 * SUBLANES), ls]
            xc = cb
            for tap in range(CONV_W):
                lo = SUBLANES - CONV_PAD_LEFT + tap
                xc = xc + xw[lo:lo + tb] * cw[tap:tap + 1]
            pre = jnp.dot(xc.astype(BF16), wg, preferred_element_type=F32)
            hxc = 0.5 * xc
            a_f, b_f = gate_ab(pre[:, 0:128] + bg[0:1], pre[:, 128:256] + bg[1:2], sp_f, hxc)
            a_b, b_b = gate_ab(pre[:, 256:384] + bg[2:3], pre[:, 384:512] + bg[3:4], sp_b, hxc)
            ab[j, pl.ds(t0, tb), :] = a_b
            bb[j, pl.ds(t0, tb), :] = b_b
            af[j, pl.ds(t0, tb), :] = a_f
            hf[j, pl.ds(t0, tb), :] = b_f
            return _scan_block(af, hf, hf, j, t0, tb, carry, False)

        hl_ref[0, 0:1, ls] = lax.fori_loop(0, nblk, fwd_block, h0_ref[0, 0:1, ls])

        def bwd_block(tt, carry):
            t0 = pl.multiple_of((nblk - 1 - tt) * tb, tb)
            carry = _scan_block(ab, bb, bb, j, t0, tb, carry, True)
            y = ((hf[j, pl.ds(t0, tb), :] + bb[j, pl.ds(t0, tb), :])
                 * _gelu_tanh(xg_ref[0, pl.ds(t0, tb), ls]))
            y_ref[0, pl.ds(t0, tb), ls] = y.astype(y_ref.dtype)
            return carry

        hl_ref[0, 1:2, ls] = lax.fori_loop(0, nblk, bwd_block, h0_ref[0, 1:2, ls])


def _rglru_call(xr, xg, h0, cw, cb, wg, bg, lam, *, cc):
    b, s, _ = xr.shape
    nch = D_RG // cc
    tb = 256
    seq = pl.BlockSpec((1, s, cc), lambda bi, ci: (bi, 0, ci))
    st = pl.BlockSpec((1, 2, cc), lambda bi, ci: (bi, 0, ci))
    return pl.pallas_call(
        functools.partial(_rglru_kernel, s=s, cc=cc, tb=tb),
        grid=(b, nch),
        in_specs=[seq, seq, st,
                  pl.BlockSpec((CONV_W, cc), lambda bi, ci: (0, ci)),
                  pl.BlockSpec((1, cc), lambda bi, ci: (0, ci)),
                  pl.BlockSpec(wg.shape, lambda bi, ci: (0, 0, 0)),
                  pl.BlockSpec((4, cc), lambda bi, ci: (0, ci)),
                  pl.BlockSpec((2, cc), lambda bi, ci: (0, ci))],
        out_specs=[seq, st],
        out_shape=[jax.ShapeDtypeStruct((b, s, D_RG), BF16),
                   jax.ShapeDtypeStruct((b, 2, D_RG), F32)],
        scratch_shapes=[pltpu.VMEM((s + 2 * SUBLANES, cc), F32)]
        + [pltpu.VMEM((cc // LANES, s, LANES), F32)] * 4,
        compiler_params=pltpu.CompilerParams(
            dimension_semantics=("arbitrary", "arbitrary"), vmem_limit_bytes=VMEM_LIMIT),
        name="rglru",
    )(xr, xg, h0, cw, cb, wg, bg, lam)


def _rope_tables(s):
    pos = np.arange(s)
    row = (pos // GRID_W).astype(np.float32)
    col = (pos % GRID_W).astype(np.float32)
    n_freq = HEAD_DIM // 4
    inv_freq = (ROPE_BASE ** (-np.arange(n_freq, dtype=np.float32) / n_freq)).astype(np.float32)
    p = np.arange(LANES) % HEAD_DIM
    freq = inv_freq[p % n_freq]
    ang = (np.where((p < HEAD_DIM // 2)[None, :], row[:, None], col[:, None])
           * freq[None, :]).astype(np.float32)
    sign = np.where((p % (2 * n_freq)) < n_freq, -1.0, 1.0)
    cos = np.cos(ang.astype(np.float64)).astype(np.float32)
    sin = (np.sin(ang.astype(np.float64)) * sign[None, :]).astype(np.float32)
    return jnp.asarray(cos), jnp.asarray(sin)


def _gate_weights(w_r, w_i):
    def bd(w):
        w = w.reshape(4, 2, RG_BLOCK_W, RG_BLOCK_W)
        z = jnp.zeros_like(w[:, 0])
        top = jnp.concatenate([w[:, 0], z], axis=2)
        bot = jnp.concatenate([z, w[:, 1]], axis=2)
        return jnp.concatenate([top, bot], axis=1)
    w = jnp.concatenate([bd(w_r[0]), bd(w_i[0]), bd(w_r[1]), bd(w_i[1])], axis=2)
    return (0.5 * w).astype(BF16)


def kernel(x_prompt, x_sample, cache_attn_k, cache_attn_v, state_rglru, c, c_ctx, norm_g, w_mod, b_mod, ffn_w_gate, ffn_w_up, ffn_w_down, w_in, w_out, diff_lambda, subln_g, conv_w, conv_b, rg_w_r, rg_b_r, rg_w_i, rg_b_i, rg_lambda, final_g):
    l = 0
    bsz, seq, _ = x_prompt.shape
    dbsz, dseq, _ = x_sample.shape
    past = cache_attn_k.shape[2]

    ng = norm_g[l]
    wgate = ffn_w_gate[l].astype(BF16)
    wup = ffn_w_up[l].astype(BF16)
    wd = ffn_w_down[l].astype(BF16)
    win = w_in[l].astype(BF16)
    wout = w_out[l].astype(BF16)
    wg = _gate_weights(rg_w_r[l], rg_w_i[l])
    bg = 0.5 * jnp.stack([rg_b_r[l, 0], rg_b_i[l, 0], rg_b_r[l, 1], rg_b_i[l, 1]])
    cw = conv_w[l]
    cb = conv_b[l][None, :]
    lam = rg_lambda[l]
    dl = diff_lambda[l]
    sg = subln_g[l][None, :]
    fg = final_g[None, :]

    c8 = jnp.concatenate([c_ctx[None, :], c, jnp.zeros((SUBLANES - 1 - dbsz, D_MODEL), F32)], axis=0)
    mod3 = _mod_call(c8, w_mod[l], b_mod[l][None, :]).reshape(SUBLANES, N_MOD, D_MODEL)

    def layer(x, *, rows_per_mod, mod_base, nb, s, k_ctx, v_ctx, h0, cc, tq, rope_tabs):
        tm = 512
        tm_ffn = 1024
        x1 = _ffn_call(x, mod3, ng, wgate, wup, wd, sub=0, rows_per_mod=rows_per_mod,
                       mod_base=mod_base, tm=tm_ffn)
        q, k, v, xr, xg = _proj_call(x1, mod3, ng, win, rows_per_mod=rows_per_mod,
                                     mod_base=mod_base, tm=tm, rope_tabs=rope_tabs)
        q3 = q.reshape(nb, s, D_ATTN)
        if k_ctx is None:
            o = _attn_call(q3, k, v, dl, sg, tq=tq)
        else:
            o = _attn_call(q3, k.reshape(nb, s, D_ATTN), v.reshape(nb, s, D_ATTN), dl, sg, tq=tq,
                           cache=(k_ctx, v_ctx))
        rg, h_last = _rglru_call(xr.reshape(nb, s, D_RG), xg.reshape(nb, s, D_RG), h0,
                                 cw, cb, wg, bg, lam, cc=cc)
        y = _ffn_call(x1, mod3, ng, wgate, wup, wd, sub=2, rows_per_mod=rows_per_mod,
                      mod_base=mod_base, tm=tm_ffn,
                      mix=(o.reshape(nb * s, D_ATTN), rg.reshape(nb * s, D_RG), wout),
                      final_g=fg)
        return y, k, v, h_last

    yp, k_new, v_new, h_new = layer(
        x_prompt.reshape(bsz * seq, D_MODEL), rows_per_mod=bsz * seq, mod_base=0, nb=bsz, s=seq,
        k_ctx=None, v_ctx=None, h0=jnp.zeros((bsz, 2, D_RG), F32), cc=D_RG, tq=seq,
        rope_tabs=None)
    ys, _, _, _ = layer(
        x_sample.reshape(dbsz * dseq, D_MODEL), rows_per_mod=dseq, mod_base=1, nb=dbsz, s=dseq,
        k_ctx=cache_attn_k[:, l].reshape(dbsz, past, D_ATTN).astype(BF16),
        v_ctx=cache_attn_v[:, l].reshape(dbsz, past, D_ATTN).astype(BF16),
        h0=state_rglru[:, l], cc=LANES, tq=256, rope_tabs=_rope_tables(dseq))

    return (yp.reshape(bsz, seq, D_MODEL),
            ys.reshape(dbsz, dseq, D_MODEL),
            k_new.reshape(bsz, 1, seq, N_HEADS, V_DIM),
            v_new.reshape(bsz, 1, seq, N_HEADS, V_DIM),
            h_new.reshape(bsz, 1, 2, D_RG))
```

```python
import functools
import math

import jax
import jax.numpy as jnp
import numpy as np
from jax import lax
from jax.experimental import pallas as pl
from jax.experimental.pallas import tpu as pltpu

F32 = jnp.float32
BF16 = jnp.bfloat16

D_MODEL = 1024
N_HEADS = 4
HEAD_DIM = 64
V_DIM = 2 * HEAD_DIM
D_ATTN = N_HEADS * V_DIM
D_RG = 512
RG_BLOCK_W = 64
RG_C = 8.0
CONV_W = 4
CONV_PAD_LEFT = 2
D_FF = 2816
N_MOD = 9
GRID_W = 64
ROPE_BASE = 10000.0
EPS = 1e-6
LAM_INIT = 0.8 - 0.6 * math.exp(-0.3 * 0)
LOG2_E = math.log2(math.e)

LANES = 128
SUBLANES = 8
MXU_N = 256
S_BUFS = 2
SEQS_PER_STEP = 2
SQRT_FLOOR = 1e-30
SCAN_ROWS = SUBLANES * SUBLANES
EXP_ROWS = 64
FF_CHUNK = MXU_N
N_FF_CHUNKS = D_FF // FF_CHUNK
VMEM_LIMIT = 56 * 1024 * 1024


def _sigmoid(x):
    return 1.0 / (1.0 + jnp.exp(-x))


def _gelu_tanh(x):
    return 0.5 * x * (1.0 + jnp.tanh(math.sqrt(2.0 / math.pi) * (x + 0.044715 * (x * x * x))))


def _rms(x, g):
    ms = jnp.mean(x * x, axis=-1, keepdims=True)
    return x * lax.rsqrt(ms + EPS) * g


def _mod_kernel(c_ref, w_ref, b_ref, o_ref):
    c = c_ref[...]
    s = (c * _sigmoid(c)).astype(BF16)
    o_ref[...] = jnp.dot(s, w_ref[...].astype(BF16), preferred_element_type=F32) + b_ref[...]


def _mod_call(c8, w_mod, b_mod):
    n = w_mod.shape[1]
    tn = 1536
    return pl.pallas_call(
        _mod_kernel,
        grid=(n // tn,),
        in_specs=[
            pl.BlockSpec((SUBLANES, D_MODEL), lambda j: (0, 0)),
            pl.BlockSpec((D_MODEL, tn), lambda j: (0, j)),
            pl.BlockSpec((1, tn), lambda j: (0, j)),
        ],
        out_specs=pl.BlockSpec((SUBLANES, tn), lambda j: (0, j)),
        out_shape=jax.ShapeDtypeStruct((SUBLANES, n), F32),
        compiler_params=pltpu.CompilerParams(
            dimension_semantics=("arbitrary",), vmem_limit_bytes=VMEM_LIMIT),
        name="mod",
    )(c8, w_mod, b_mod)


def _ffn_kernel(*refs, sub, fuse_mix, final_norm):
    it = iter(refs)
    x_ref = next(it)
    if fuse_mix:
        o_ref_in = next(it)
        rg_ref = next(it)
        wout_ref = next(it)
    mod_ref = next(it)
    ng_ref = next(it)
    wg_ref = next(it)
    wu_ref = next(it)
    wd_ref = next(it)
    fg_ref = next(it) if final_norm else None
    out_ref = next(it)
    acc_ref = out_ref
    h_ref = next(it)
    gu0_ref = next(it)
    gu1_ref = next(it)

    x = x_ref[...]
    if fuse_mix:
        mix = jnp.dot(o_ref_in[...], wout_ref[0:D_ATTN, :], preferred_element_type=F32)
        mix = mix + jnp.dot(rg_ref[...], wout_ref[D_ATTN:, :], preferred_element_type=F32)
        x = x + mod_ref[0, 5:6, :] * mix
    sh = mod_ref[0, 3 * sub:3 * sub + 1, :]
    sc = mod_ref[0, 3 * sub + 1:3 * sub + 2, :]
    gate = mod_ref[0, 3 * sub + 2:3 * sub + 3, :]
    h_ref[...] = (_rms(x, ng_ref[sub:sub + 1, :]) * (1.0 + sc) + sh).astype(BF16)

    def chunk(j):
        return pl.ds(pl.multiple_of(j * FF_CHUNK, FF_CHUNK), FF_CHUNK)

    def gate_up(j):
        h = h_ref[...]
        return jnp.concatenate(
            [jnp.dot(h, wg_ref[0, :, chunk(j)], preferred_element_type=F32),
             jnp.dot(h, wu_ref[0, :, chunk(j)], preferred_element_type=F32)], axis=1)

    def down(j, gu):
        g = gu[:, :FF_CHUNK]
        u = gu[:, FF_CHUNK:]
        a = (g * _sigmoid(g) * u).astype(BF16)
        return jnp.dot(a, wd_ref[0, chunk(j), :], preferred_element_type=F32)

    gu0_ref[...] = gate_up(0)
    gu1_ref[...] = gate_up(1)
    acc_ref[...] = down(0, gu0_ref[...])

    def body(i, carry):
        gu0_ref[...] = gate_up(2 * i + 2)
        acc_ref[...] += down(2 * i + 1, gu1_ref[...])
        gu1_ref[...] = gate_up(2 * i + 3)
        acc_ref[...] += down(2 * i + 2, gu0_ref[...])
        return carry

    assert N_FF_CHUNKS % 2 == 1 and N_FF_CHUNKS >= 5
    lax.fori_loop(0, (N_FF_CHUNKS - 3) // 2, body, 0, unroll=2)
    gu0_ref[...] = gate_up(N_FF_CHUNKS - 1)
    acc_ref[...] += down(N_FF_CHUNKS - 2, gu1_ref[...])
    y = x + (0.5 * gate) * (acc_ref[...] + down(N_FF_CHUNKS - 1, gu0_ref[...]))
    if final_norm:
        y = _rms(y, fg_ref[...])
    out_ref[...] = y


def _ffn_call(x, mod3, ng, wg, wu, wd, *, sub, rows_per_mod, mod_base, tm,
              mix=None, final_g=None):
    t = x.shape[0]
    ffn_idx = sub // 2
    fuse_mix = mix is not None
    final_norm = final_g is not None
    tiles_per_mod = rows_per_mod // tm

    def row_map(i):
        return (i, 0)

    def mod_map(i):
        return (mod_base + i // tiles_per_mod, 0, 0)

    const2 = lambda i: (0, 0)
    ffn_map = lambda i: (ffn_idx, 0, 0)
    in_specs = [pl.BlockSpec((tm, D_MODEL), row_map)]
    args = [x]
    if fuse_mix:
        o, rg, wout = mix
        in_specs += [pl.BlockSpec((tm, D_ATTN), row_map),
                     pl.BlockSpec((tm, D_RG), row_map),
                     pl.BlockSpec(wout.shape, const2, pipeline_mode=pl.Buffered(1))]
        args += [o, rg, wout]
    in_specs += [pl.BlockSpec((1, N_MOD, D_MODEL), mod_map),
                 pl.BlockSpec(ng.shape, const2),
                 pl.BlockSpec((1,) + wg.shape[1:], ffn_map, pipeline_mode=pl.Buffered(1)),
                 pl.BlockSpec((1,) + wu.shape[1:], ffn_map, pipeline_mode=pl.Buffered(1)),
                 pl.BlockSpec((1,) + wd.shape[1:], ffn_map, pipeline_mode=pl.Buffered(1))]
    args += [mod3, ng, wg, wu, wd]
    if final_norm:
        in_specs.append(pl.BlockSpec((1, D_MODEL), const2))
        args.append(final_g)
    return pl.pallas_call(
        functools.partial(_ffn_kernel, sub=sub, fuse_mix=fuse_mix, final_norm=final_norm),
        grid=(t // tm,),
        in_specs=in_specs,
        out_specs=pl.BlockSpec((tm, D_MODEL), row_map),
        out_shape=jax.ShapeDtypeStruct((t, D_MODEL), F32),
        scratch_shapes=[pltpu.VMEM((tm, D_MODEL), BF16),
                        pltpu.VMEM((tm, 2 * FF_CHUNK), F32), pltpu.VMEM((tm, 2 * FF_CHUNK), F32)],
        compiler_params=pltpu.CompilerParams(
            dimension_semantics=("arbitrary",), vmem_limit_bytes=VMEM_LIMIT),
        name="ffn%d" % sub,
    )(*args)


def _rope(x, cos, sin_signed, first_half):
    outs = []
    for cblk in range(x.shape[1] // LANES):
        xs = x[:, cblk * LANES:(cblk + 1) * LANES]
        partner = jnp.where(first_half, pltpu.roll(xs, LANES - 16, axis=1),
                            pltpu.roll(xs, 16, axis=1))
        outs.append(xs * cos + partner * sin_signed)
    return jnp.concatenate(outs, axis=1)


def _proj_kernel(*refs, rope):
    it = iter(refs)
    x_ref = next(it)
    mod_ref = next(it)
    ng_ref = next(it)
    win_ref = next(it)
    if rope:
        cos_ref = next(it)
        sin_ref = next(it)
    q_ref, k_ref, v_ref, xr_ref, xg_ref = it

    x = x_ref[...]
    sh = mod_ref[0, 3:4, :]
    sc = mod_ref[0, 4:5, :]
    h = (_rms(x, ng_ref[1:2, :]) * (1.0 + sc) + sh).astype(BF16)

    def col(j):
        return jnp.dot(h, win_ref[:, j * D_ATTN:(j + 1) * D_ATTN], preferred_element_type=F32)

    q = col(0)
    k = col(1)
    if rope:
        cos = cos_ref[...]
        sin = sin_ref[...]
        lane = lax.broadcasted_iota(jnp.int32, (1, LANES), 1)
        first_half = (lane % 32) < 16
        q = _rope(q, cos, sin, first_half)
        k = _rope(k, cos, sin, first_half)
    q_ref[...] = (q * (HEAD_DIM ** -0.5 * LOG2_E)).astype(q_ref.dtype)
    v = col(2)
    if rope:
        k_ref[...] = k.astype(k_ref.dtype)
        v_ref[...] = v.astype(v_ref.dtype)
    else:
        tm = x.shape[0]
        for hd in range(N_HEADS):
            k_ref[pl.ds(hd, tm, stride=N_HEADS), :] = k[:, hd * V_DIM:(hd + 1) * V_DIM]
            v_ref[pl.ds(hd, tm, stride=N_HEADS), :] = v[:, hd * V_DIM:(hd + 1) * V_DIM]
    xr_ref[...] = col(3)
    xg_ref[...] = col(4)


def _proj_call(x, mod3, ng, win, *, rows_per_mod, mod_base, tm, rope_tabs=None):
    t = x.shape[0]
    rope = rope_tabs is not None
    tiles_per_mod = rows_per_mod // tm
    row_map = lambda i: (i, 0)
    const2 = lambda i: (0, 0)
    in_specs = [pl.BlockSpec((tm, D_MODEL), row_map),
                pl.BlockSpec((1, N_MOD, D_MODEL), lambda i: (mod_base + i // tiles_per_mod, 0, 0)),
                pl.BlockSpec(ng.shape, const2),
                pl.BlockSpec(win.shape, const2, pipeline_mode=pl.Buffered(1))]
    args = [x, mod3, ng, win]
    if rope:
        cos, sin = rope_tabs
        tiles_per_seq = cos.shape[0] // tm
        tab_map = lambda i: (i % tiles_per_seq, 0)
        in_specs += [pl.BlockSpec((tm, LANES), tab_map), pl.BlockSpec((tm, LANES), tab_map)]
        args += [cos, sin]
    half = pl.BlockSpec((tm, D_ATTN), row_map)
    if rope:
        kv_spec, kv_shape = half, jax.ShapeDtypeStruct((t, D_ATTN), BF16)
    else:
        kv_spec = pl.BlockSpec((tm * N_HEADS, V_DIM), row_map)
        kv_shape = jax.ShapeDtypeStruct((t * N_HEADS, V_DIM), F32)
    return pl.pallas_call(
        functools.partial(_proj_kernel, rope=rope),
        grid=(t // tm,),
        in_specs=in_specs,
        out_specs=[half, kv_spec, kv_spec, half, half],
        out_shape=[jax.ShapeDtypeStruct((t, D_ATTN), BF16),
                   kv_shape,
                   kv_shape,
                   jax.ShapeDtypeStruct((t, D_RG), F32),
                   jax.ShapeDtypeStruct((t, D_RG), F32)],
        compiler_params=pltpu.CompilerParams(
            dimension_semantics=("arbitrary",), vmem_limit_bytes=VMEM_LIMIT),
        name="proj",
    )(*args)


def _diff_lambda(dl_ref):
    dl = dl_ref[...]
    return (jnp.exp(jnp.sum(dl[0:1] * dl[1:2], axis=-1, keepdims=True))
            - jnp.exp(jnp.sum(dl[2:3] * dl[3:4], axis=-1, keepdims=True)) + LAM_INIT)


def _stack_maps(q):
    map0 = lax.broadcasted_iota(jnp.int32, (1, V_DIM), 1) < HEAD_DIM
    zero = jnp.zeros_like(q)
    return jnp.concatenate([jnp.where(map0, q, zero), jnp.where(map0, zero, q)], axis=0)


def _finish_head(o2, l, lam, sg, tq):
    o = o2[:tq] * (1.0 / l[:tq]) - o2[tq:] * (lam / l[tq:])
    return _rms(o, sg) * (1.0 - LAM_INIT)


def _attn_small_kernel(q_ref, k_ref, v_ref, dl_ref, sg_ref, o_ref, *, tq):
    lam = _diff_lambda(dl_ref)
    nseq = q_ref.shape[0]
    sk = k_ref.shape[0] // (N_HEADS * nseq)
    ones = jnp.ones((sk, V_DIM), BF16)
    for bi in range(nseq):
        for hd in range(N_HEADS):
            cols = slice(hd * V_DIM, (hd + 1) * V_DIM)
            head_rows = pl.ds(bi * sk * N_HEADS + hd, sk, stride=N_HEADS)
            q2 = _stack_maps(q_ref[bi, :, cols])
            k = k_ref[head_rows, :].astype(BF16)
            s = lax.dot_general(q2, k, (((1,), (1,)), ((), ())), preferred_element_type=F32)
            e = jnp.exp2(s - jnp.max(s, axis=-1, keepdims=True))
            v1 = jnp.concatenate([v_ref[head_rows, :].astype(BF16), ones], axis=1)
            o2 = jnp.dot(e.astype(BF16), v1, preferred_element_type=F32)
            o = _finish_head(o2[:, :V_DIM], o2[:, V_DIM:], lam, sg_ref[...], tq)
            o_ref[bi, :, cols] = o.astype(o_ref.dtype)


def _attn_pipe_kernel(q_ref, k_ref, v_ref, kc_ref, vc_ref, dl_ref, sg_ref, o_ref, *scratch, tq, kt):
    lam = _diff_lambda(dl_ref)
    s_bufs, scratch = scratch[:S_BUFS], scratch[S_BUFS:]
    sc_bufs, scratch = scratch[:S_BUFS], scratch[S_BUFS:]
    p_bufs, pc_bufs = scratch[0:2], scratch[2:4]
    q2_scr, mv_scr, mb_scr, oacc_scr = scratch[4:]
    n_steps = k_ref.shape[1] // kt
    for t in range(N_HEADS + 2):
        ha, hb, hc = t, t - 1, t - 2
        do_a, do_b, do_c = 0 <= ha < N_HEADS, 0 <= hb < N_HEADS, 0 <= hc < N_HEADS
        cols_a = slice(ha * V_DIM, (ha + 1) * V_DIM)
        cols_c = slice(hc * V_DIM, (hc + 1) * V_DIM)
        if do_a:
            q2_scr[...] = _stack_maps(q_ref[0, :, cols_a])
            mv_scr[...] = jnp.full(mv_scr.shape, -jnp.inf, F32)
        if do_c:
            oacc_scr[...] = jnp.zeros(oacc_scr.shape, F32)

        def key_block(k_blk, v_blk, s_a, s_b, p_b, p_c):
            whole = (slice(None), slice(None))
            if do_a:
                s_ref, s_idx = s_a
                kb = k_blk()
                for c0 in range(0, kb.shape[0], MXU_N):
                    s = lax.dot_general(q2_scr[...], kb[c0:c0 + MXU_N], (((1,), (1,)), ((), ())),
                                        preferred_element_type=F32)
                    s_ref[s_idx + (slice(None), slice(c0, c0 + MXU_N))] = s
                    mv_scr[...] = jnp.maximum(mv_scr[...],
                                              jnp.maximum(s[:, :LANES], s[:, LANES:]))
            if do_b:
                s_ref, s_idx = s_b
                p_ref, p_idx = p_b
                n_keys = s_ref.shape[-1]
                for r0 in range(0, 2 * tq, EXP_ROWS):
                    rows = slice(r0, r0 + EXP_ROWS)
                    mb = mb_scr[rows, :]
                    for c0 in range(0, n_keys, LANES):
                        at = (rows, slice(c0, c0 + LANES))
                        p_ref[p_idx + at] = jnp.exp2(s_ref[s_idx + at] - mb).astype(BF16)
            if do_c:
                v = v_blk()
                v1 = jnp.concatenate([v, jnp.ones(v.shape, BF16)], axis=1)
                oacc_scr[...] += jnp.dot(p_c[0][p_c[1] + whole], v1, preferred_element_type=F32)

        def new_keys_step(j, carry):
            keys = pl.ds(pl.multiple_of(j * kt, kt), kt)
            key_block(lambda: k_ref[0, keys, cols_a], lambda: v_ref[0, keys, cols_c],
                      (s_bufs[ha % S_BUFS], (j,)), (s_bufs[hb % S_BUFS], (j,)),
                      (p_bufs[hb % 2], (j,)), (p_bufs[hc % 2], (j,)))
            return carry

        lax.fori_loop(0, n_steps, new_keys_step, 0, unroll=True)
        key_block(lambda: kc_ref[0, :, cols_a], lambda: vc_ref[0, :, cols_c],
                  (sc_bufs[ha % S_BUFS], ()), (sc_bufs[hb % S_BUFS], ()),
                  (pc_bufs[hb % 2], ()), (pc_bufs[hc % 2], ()))
        if do_a:
            m = jnp.max(mv_scr[...], axis=-1, keepdims=True)
            mb_scr[...] = jnp.broadcast_to(m, mb_scr.shape)
        if do_c:
            o = _finish_head(oacc_scr[:, :V_DIM], oacc_scr[:, V_DIM:], lam, sg_ref[...], tq)
            o_ref[0, :, cols_c] = o.astype(o_ref.dtype)


def _attn_call(q, k, v, diff_lambda, subln_g, *, tq, cache=None):
    b, sq, _ = q.shape
    qmap = lambda bi, qi: (bi, qi, 0)
    seqmap = lambda bi, qi: (bi, 0, 0)
    const2 = lambda bi, qi: (0, 0)
    nseq = 1 if cache is not None else SEQS_PER_STEP
    in_specs = [pl.BlockSpec((nseq, tq, D_ATTN), qmap)]
    args = [q, k, v]
    if cache is not None:
        sk = k.shape[1]
        in_specs += [pl.BlockSpec((1, sk, D_ATTN), seqmap, pipeline_mode=pl.Buffered(1))] * 2
        kc, vc = cache
        past = kc.shape[1]
        kt = 1024
        rows = 2 * tq
        body = functools.partial(_attn_pipe_kernel, tq=tq, kt=kt)
        in_specs += [pl.BlockSpec((1, past, D_ATTN), seqmap)] * 2
        args += [kc, vc]
        scratch = ([pltpu.VMEM((sk // kt, rows, kt), F32)] * S_BUFS
                   + [pltpu.VMEM((rows, past), F32)] * S_BUFS
                   + [pltpu.VMEM((sk // kt, rows, kt), BF16)] * 2 + [pltpu.VMEM((rows, past), BF16)] * 2
                   + [pltpu.VMEM((rows, V_DIM), BF16)] + [pltpu.VMEM((rows, LANES), F32)] * 2
                   + [pltpu.VMEM((rows, 2 * V_DIM), F32)])
    else:
        assert sq == tq and b % nseq == 0
        body = functools.partial(_attn_small_kernel, tq=tq)
        in_specs += [pl.BlockSpec((nseq * (k.shape[0] // b), V_DIM), lambda bi, qi: (bi, 0))] * 2
        scratch = []
    in_specs += [pl.BlockSpec(diff_lambda.shape, const2), pl.BlockSpec(subln_g.shape, const2)]
    args += [diff_lambda, subln_g]
    return pl.pallas_call(
        body,
        grid=(b // nseq, sq // tq),
        in_specs=in_specs,
        out_specs=pl.BlockSpec((nseq, tq, D_ATTN), qmap),
        out_shape=jax.ShapeDtypeStruct((b, sq, D_ATTN), BF16),
        scratch_shapes=scratch,
        compiler_params=pltpu.CompilerParams(
            dimension_semantics=("arbitrary", "arbitrary"), vmem_limit_bytes=VMEM_LIMIT),
        name="attn",
    )(*args)


def _scan8(a, b, reverse):
    sub = lax.broadcasted_iota(jnp.int32, a.shape, 1)
    for d in (1, 2, 4):
        if reverse:
            shift, m = SUBLANES - d, sub < SUBLANES - d
        else:
            shift, m = d, sub >= d
        a_s = pltpu.roll(a, shift, axis=1)
        b_s = pltpu.roll(b, shift, axis=1)
        b = jnp.where(m, a * b_s + b, b)
        a = jnp.where(m, a * a_s, a)
    return a, b


def _scan_tile(load_a, load_b, store_h, carry, reverse):
    order = range(SUBLANES - 1, -1, -1) if reverse else range(SUBLANES)
    hs, ps = {}, {}
    h = p = None
    for s in order:
        a, b = load_a(s), load_b(s)
        h = b if h is None else a * h + b
        p = a if p is None else a * p
        hs[s], ps[s] = h, p
    pp, hh = _scan8(p[None], h[None], reverse)
    after = hh[0] + pp[0] * carry
    sub = lax.broadcasted_iota(jnp.int32, after.shape, 0)
    if reverse:
        before = jnp.where(sub == SUBLANES - 1, carry, pltpu.roll(after, SUBLANES - 1, axis=0))
        carry = after[0:1]
    else:
        before = jnp.where(sub == 0, carry, pltpu.roll(after, 1, axis=0))
        carry = after[SUBLANES - 1:SUBLANES]
    for s in order:
        store_h(s, hs[s] + ps[s] * before)
    return carry


def _tile_rows(t0, q, s):
    return pl.ds(t0 + q * SCAN_ROWS + s, SUBLANES, stride=SUBLANES)


def _scan_block(load_a, load_b, h_ref, slab, t0, tb, carry, reverse):
    tiles = range(tb // SCAN_ROWS)
    for q in (reversed(tiles) if reverse else tiles):
        def store_h(s, h, q=q):
            h_ref[slab, _tile_rows(t0, q, s), :] = h

        carry = _scan_tile(functools.partial(load_a, q), functools.partial(load_b, q),
                           store_h, carry, reverse)
    return carry


def _softplus(x):
    return jnp.maximum(x, 0.0) + jnp.log(1.0 + jnp.exp(-jnp.abs(x)))


def _rglru_kernel(xr_ref, xg_ref, h0_ref, cw_ref, cb_ref, wg_ref, bg_ref, lam_ref,
                  y_ref, hl_ref, xpad, ab, bb, hf, *, s, cc, tb):
    nblk = s // tb
    nsub = cc // LANES
    ntile = tb // SCAN_ROWS
    cbase = pl.program_id(1) * nsub
    lanes = [slice(j * LANES, (j + 1) * LANES) for j in range(nsub)]

    def gate_ab(half_pre_r, half_pre_i, half_sp, half_xc):
        z = jnp.tanh(half_pre_r) * half_sp + half_sp
        ixc = jnp.tanh(half_pre_i) * half_xc + half_xc
        a = jnp.exp2(z * (-LOG2_E))
        u = (1.0 + a * a) * jnp.tanh(z)
        bx = u * lax.rsqrt(jnp.maximum(u, SQRT_FLOOR)) * ixc
        return a, bx

    params = []
    for j, ls in enumerate(lanes):
        xpad[j, 0:SUBLANES, :] = jnp.zeros((SUBLANES, LANES), F32)
        xpad[j, s + SUBLANES:s + 2 * SUBLANES, :] = jnp.zeros((SUBLANES, LANES), F32)
        xpad[j, SUBLANES:s + SUBLANES, :] = xr_ref[0, :, ls]
        params.append(dict(
            sp_f=(0.5 * RG_C) * _softplus(-lam_ref[0:1, ls]),
            sp_b=(0.5 * RG_C) * _softplus(-lam_ref[1:2, ls]),
            wg=wg_ref[cbase + j],
            bg=bg_ref[:, ls], cw=cw_ref[:, ls], cb=cb_ref[:, ls]))

    def fwd_lanes(j, t0, carry):
        p = params[j]
        cw, bg = p["cw"], p["bg"]
        xcs = []
        for q in range(ntile):
            xs = {m: xpad[j, _tile_rows(t0 + SUBLANES, q, m), :]
                  for m in range(-CONV_PAD_LEFT, SUBLANES + CONV_W - 1 - CONV_PAD_LEFT)}
            for sv in range(SUBLANES):
                acc = p["cb"]
                for tap in range(CONV_W):
                    acc = acc + xs[sv + tap - CONV_PAD_LEFT] * cw[tap:tap + 1]
                xcs.append(acc)
        xc = jnp.concatenate(xcs, axis=0)
        pre = jnp.dot(xc.astype(BF16), p["wg"], preferred_element_type=F32)
        hxc = 0.5 * xc
        a_f, b_f = gate_ab(pre[:, 0:128] + bg[0:1], pre[:, 128:256] + bg[1:2], p["sp_f"], hxc)
        a_b, b_b = gate_ab(pre[:, 256:384] + bg[2:3], pre[:, 384:512] + bg[3:4], p["sp_b"], hxc)
        ab[j, pl.ds(t0, tb), :] = a_b
        bb[j, pl.ds(t0, tb), :] = b_b

        def vreg_of(x):
            return lambda q, sv: x[q * SCAN_ROWS + sv * SUBLANES:
                                   q * SCAN_ROWS + (sv + 1) * SUBLANES]

        return _scan_block(vreg_of(a_f), vreg_of(b_f), hf, j, t0, tb, carry, False)

    def bwd_lanes(j, t0, carry):
        ls = lanes[j]

        def vreg_of(ref):
            return lambda q, sv: ref[j, pl.ds(t0 + q * SCAN_ROWS + sv * SUBLANES, SUBLANES), :]

        carry = _scan_block(vreg_of(ab), vreg_of(bb), bb, j, t0, tb, carry, True)
        y = ((hf[j, pl.ds(t0, tb), :] + bb[j, pl.ds(t0, tb), :])
             * _gelu_tanh(xg_ref[0, pl.ds(t0, tb), ls]))
        y_ref[0, pl.ds(t0, tb), ls] = y.astype(y_ref.dtype)
        return carry

    def fwd_block(t, carries):
        t0 = pl.multiple_of(t * tb, tb)
        return tuple(fwd_lanes(j, t0, carries[j]) for j in range(nsub))

    def bwd_block(tt, carries):
        t0 = pl.multiple_of((nblk - 1 - tt) * tb, tb)
        return tuple(bwd_lanes(j, t0, carries[j]) for j in range(nsub))

    ends = lax.fori_loop(0, nblk, fwd_block, tuple(h0_ref[0, 0:1, ls] for ls in lanes))
    for ls, end in zip(lanes, ends):
        hl_ref[0, 0:1, ls] = end
    ends = lax.fori_loop(0, nblk, bwd_block, tuple(h0_ref[0, 1:2, ls] for ls in lanes))
    for ls, end in zip(lanes, ends):
        hl_ref[0, 1:2, ls] = end


def _rglru_call(xr, xg, h0, cw, cb, wg, bg, lam, *, cc):
    b, s, _ = xr.shape
    nch = D_RG // cc
    tb = 256
    seq = pl.BlockSpec((1, s, cc), lambda bi, ci: (bi, 0, ci))
    st = pl.BlockSpec((1, 2, cc), lambda bi, ci: (bi, 0, ci))
    return pl.pallas_call(
        functools.partial(_rglru_kernel, s=s, cc=cc, tb=tb),
        grid=(b, nch),
        in_specs=[seq, seq, st,
                  pl.BlockSpec((CONV_W, cc), lambda bi, ci: (0, ci)),
                  pl.BlockSpec((1, cc), lambda bi, ci: (0, ci)),
                  pl.BlockSpec(wg.shape, lambda bi, ci: (0, 0, 0)),
                  pl.BlockSpec((4, cc), lambda bi, ci: (0, ci)),
                  pl.BlockSpec((2, cc), lambda bi, ci: (0, ci))],
        out_specs=[seq, st],
        out_shape=[jax.ShapeDtypeStruct((b, s, D_RG), BF16),
                   jax.ShapeDtypeStruct((b, 2, D_RG), F32)],
        scratch_shapes=[pltpu.VMEM((cc // LANES, s + 2 * SUBLANES, LANES), F32)]
        + [pltpu.VMEM((cc // LANES, s, LANES), F32)] * 3,
        compiler_params=pltpu.CompilerParams(
            dimension_semantics=("arbitrary", "arbitrary"), vmem_limit_bytes=VMEM_LIMIT),
        name="rglru",
    )(xr, xg, h0, cw, cb, wg, bg, lam)


def _rope_tables(s):
    pos = np.arange(s)
    row = (pos // GRID_W).astype(np.float32)
    col = (pos % GRID_W).astype(np.float32)
    n_freq = HEAD_DIM // 4
    inv_freq = (ROPE_BASE ** (-np.arange(n_freq, dtype=np.float32) / n_freq)).astype(np.float32)
    p = np.arange(LANES) % HEAD_DIM
    freq = inv_freq[p % n_freq]
    ang = (np.where((p < HEAD_DIM // 2)[None, :], row[:, None], col[:, None])
           * freq[None, :]).astype(np.float32)
    sign = np.where((p % (2 * n_freq)) < n_freq, -1.0, 1.0)
    cos = np.cos(ang.astype(np.float64)).astype(np.float32)
    sin = (np.sin(ang.astype(np.float64)) * sign[None, :]).astype(np.float32)
    return jnp.asarray(cos), jnp.asarray(sin)


def _gate_weights(w_r, w_i):
    def bd(w):
        w = w.reshape(4, 2, RG_BLOCK_W, RG_BLOCK_W)
        z = jnp.zeros_like(w[:, 0])
        top = jnp.concatenate([w[:, 0], z], axis=2)
        bot = jnp.concatenate([z, w[:, 1]], axis=2)
        return jnp.concatenate([top, bot], axis=1)
    w = jnp.concatenate([bd(w_r[0]), bd(w_i[0]), bd(w_r[1]), bd(w_i[1])], axis=2)
    return (0.5 * w).astype(BF16)


def kernel(x_prompt, x_sample, cache_attn_k, cache_attn_v, state_rglru, c, c_ctx, norm_g, w_mod, b_mod, ffn_w_gate, ffn_w_up, ffn_w_down, w_in, w_out, diff_lambda, subln_g, conv_w, conv_b, rg_w_r, rg_b_r, rg_w_i, rg_b_i, rg_lambda, final_g):
    l = 0
    bsz, seq, _ = x_prompt.shape
    dbsz, dseq, _ = x_sample.shape
    past = cache_attn_k.shape[2]

    ng = norm_g[l]
    wgate = ffn_w_gate[l].astype(BF16)
    wup = ffn_w_up[l].astype(BF16)
    wd = ffn_w_down[l].astype(BF16)
    win = w_in[l].astype(BF16)
    wout = w_out[l].astype(BF16)
    wg = _gate_weights(rg_w_r[l], rg_w_i[l])
    bg = 0.5 * jnp.stack([rg_b_r[l, 0], rg_b_i[l, 0], rg_b_r[l, 1], rg_b_i[l, 1]])
    cw = conv_w[l]
    cb = conv_b[l][None, :]
    lam = rg_lambda[l]
    dl = diff_lambda[l]
    sg = subln_g[l][None, :]
    fg = final_g[None, :]

    c8 = jnp.concatenate([c_ctx[None, :], c, jnp.zeros((SUBLANES - 1 - dbsz, D_MODEL), F32)], axis=0)
    mod3 = _mod_call(c8, w_mod[l], b_mod[l][None, :]).reshape(SUBLANES, N_MOD, D_MODEL)

    def layer(x, *, rows_per_mod, mod_base, nb, s, k_ctx, v_ctx, h0, cc, tq, rope_tabs):
        tm = 512
        tm_ffn = 1024
        x1 = _ffn_call(x, mod3, ng, wgate, wup, wd, sub=0, rows_per_mod=rows_per_mod,
                       mod_base=mod_base, tm=tm_ffn)
        q, k, v, xr, xg = _proj_call(x1, mod3, ng, win, rows_per_mod=rows_per_mod,
                                     mod_base=mod_base, tm=tm, rope_tabs=rope_tabs)
        q3 = q.reshape(nb, s, D_ATTN)
        if k_ctx is None:
            o = _attn_call(q3, k, v, dl, sg, tq=tq)
        else:
            o = _attn_call(q3, k.reshape(nb, s, D_ATTN), v.reshape(nb, s, D_ATTN), dl, sg, tq=tq,
                           cache=(k_ctx, v_ctx))
        rg, h_last = _rglru_call(xr.reshape(nb, s, D_RG), xg.reshape(nb, s, D_RG), h0,
                                 cw, cb, wg, bg, lam, cc=cc)
        y = _ffn_call(x1, mod3, ng, wgate, wup, wd, sub=2, rows_per_mod=rows_per_mod,
                      mod_base=mod_base, tm=tm_ffn,
                      mix=(o.reshape(nb * s, D_ATTN), rg.reshape(nb * s, D_RG), wout),
                      final_g=fg)
        return y, k, v, h_last

    yp, k_new, v_new, h_new = layer(
        x_prompt.reshape(bsz * seq, D_MODEL), rows_per_mod=bsz * seq, mod_base=0, nb=bsz, s=seq,
        k_ctx=None, v_ctx=None, h0=jnp.zeros((bsz, 2, D_RG), F32), cc=D_RG, tq=seq,
        rope_tabs=None)
    ys, _, _, _ = layer(
        x_sample.reshape(dbsz * dseq, D_MODEL), rows_per_mod=dseq, mod_base=1, nb=dbsz, s=dseq,
        k_ctx=cache_attn_k[:, l].reshape(dbsz, past, D_ATTN).astype(BF16),
        v_ctx=cache_attn_v[:, l].reshape(dbsz, past, D_ATTN).astype(BF16),
        h0=state_rglru[:, l], cc=2 * LANES, tq=256, rope_tabs=_rope_tables(dseq))

    return (yp.reshape(bsz, seq, D_MODEL),
            ys.reshape(dbsz, dseq, D_MODEL),
            k_new.reshape(bsz, 1, seq, N_HEADS, V_DIM),
            v_new.reshape(bsz, 1, seq, N_HEADS, V_DIM),
            h_new.reshape(bsz, 1, 2, D_RG))
```

```python
import functools
import math

import jax
import jax.numpy as jnp
import numpy as np
from jax import lax
from jax.experimental import pallas as pl
from jax.experimental.pallas import tpu as pltpu

F32 = jnp.float32
BF16 = jnp.bfloat16

D_MODEL = 1024
N_HEADS = 4
HEAD_DIM = 64
V_DIM = 2 * HEAD_DIM
D_ATTN = N_HEADS * V_DIM
D_RG = 512
RG_BLOCK_W = 64
RG_C = 8.0
CONV_W = 4
CONV_PAD_LEFT = 2
D_FF = 2816
N_MOD = 9
GRID_W = 64
ROPE_BASE = 10000.0
EPS = 1e-6
LAM_INIT = 0.8 - 0.6 * math.exp(-0.3 * 0)
LOG2_E = math.log2(math.e)

LANES = 128
SUBLANES = 8
MXU_N = 256
S_BUFS = 2
SEQS_PER_STEP = 2
SQRT_FLOOR = 1e-30
SCAN_ROWS = SUBLANES * SUBLANES
EXP_ROWS = 64
FF_CHUNK = MXU_N
N_FF_CHUNKS = D_FF // FF_CHUNK
VMEM_LIMIT = 56 * 1024 * 1024


def _sigmoid(x):
    return 1.0 / (1.0 + jnp.exp(-x))


def _gelu_tanh(x):
    return 0.5 * x * (1.0 + jnp.tanh(math.sqrt(2.0 / math.pi) * (x + 0.044715 * (x * x * x))))


def _rms(x, g):
    ms = jnp.mean(x * x, axis=-1, keepdims=True)
    return x * lax.rsqrt(ms + EPS) * g


def _mod_kernel(c_ref, w_ref, b_ref, o_ref):
    c = c_ref[...]
    s = (c * _sigmoid(c)).astype(BF16)
    o_ref[...] = jnp.dot(s, w_ref[...].astype(BF16), preferred_element_type=F32) + b_ref[...]


def _mod_call(c8, w_mod, b_mod):
    n = w_mod.shape[1]
    tn = 1536
    return pl.pallas_call(
        _mod_kernel,
        grid=(n // tn,),
        in_specs=[
            pl.BlockSpec((SUBLANES, D_MODEL), lambda j: (0, 0)),
            pl.BlockSpec((D_MODEL, tn), lambda j: (0, j)),
            pl.BlockSpec((1, tn), lambda j: (0, j)),
        ],
        out_specs=pl.BlockSpec((SUBLANES, tn), lambda j: (0, j)),
        out_shape=jax.ShapeDtypeStruct((SUBLANES, n), F32),
        compiler_params=pltpu.CompilerParams(
            dimension_semantics=("arbitrary",), vmem_limit_bytes=VMEM_LIMIT),
        name="mod",
    )(c8, w_mod, b_mod)


def _ffn_kernel(*refs, sub, fuse_mix, final_norm, n_casts):
    it = iter(refs)
    x_ref = next(it)
    if fuse_mix:
        o_ref_in = next(it)
        rg_ref = next(it)
        wout_ref = next(it)
    mod_ref = next(it)
    ng_ref = next(it)
    wg_ref = next(it)
    wu_ref = next(it)
    wd_ref = next(it)
    fg_ref = next(it) if final_norm else None
    cast_in = [next(it) for _ in range(n_casts)]
    out_ref = next(it)
    for src in cast_in:
        dst = next(it)
        dst[...] = src[...].astype(dst.dtype)
    acc_ref = out_ref
    h_ref = next(it)
    gu0_ref = next(it)
    gu1_ref = next(it)

    x = x_ref[...]
    if fuse_mix:
        mix = jnp.dot(o_ref_in[...], wout_ref[0:D_ATTN, :], preferred_element_type=F32)
        mix = mix + jnp.dot(rg_ref[...], wout_ref[D_ATTN:, :], preferred_element_type=F32)
        x = x + mod_ref[0, 5:6, :] * mix
    sh = mod_ref[0, 3 * sub:3 * sub + 1, :]
    sc = mod_ref[0, 3 * sub + 1:3 * sub + 2, :]
    gate = mod_ref[0, 3 * sub + 2:3 * sub + 3, :]
    h_ref[...] = (_rms(x, ng_ref[sub:sub + 1, :]) * (1.0 + sc) + sh).astype(BF16)

    def chunk(j):
        return pl.ds(pl.multiple_of(j * FF_CHUNK, FF_CHUNK), FF_CHUNK)

    def gate_up(j):
        h = h_ref[...]
        return jnp.concatenate(
            [jnp.dot(h, wg_ref[:, chunk(j)], preferred_element_type=F32),
             jnp.dot(h, wu_ref[:, chunk(j)], preferred_element_type=F32)], axis=1)

    def down(j, gu):
        g = gu[:, :FF_CHUNK]
        u = gu[:, FF_CHUNK:]
        a = (g * _sigmoid(g) * u).astype(BF16)
        return jnp.dot(a, wd_ref[chunk(j), :], preferred_element_type=F32)

    gu0_ref[...] = gate_up(0)
    gu1_ref[...] = gate_up(1)
    acc_ref[...] = down(0, gu0_ref[...])

    def body(i, carry):
        gu0_ref[...] = gate_up(2 * i + 2)
        acc_ref[...] += down(2 * i + 1, gu1_ref[...])
        gu1_ref[...] = gate_up(2 * i + 3)
        acc_ref[...] += down(2 * i + 2, gu0_ref[...])
        return carry

    assert N_FF_CHUNKS % 2 == 1 and N_FF_CHUNKS >= 5
    lax.fori_loop(0, (N_FF_CHUNKS - 3) // 2, body, 0, unroll=2)
    gu0_ref[...] = gate_up(N_FF_CHUNKS - 1)
    acc_ref[...] += down(N_FF_CHUNKS - 2, gu1_ref[...])
    y = x + (0.5 * gate) * (acc_ref[...] + down(N_FF_CHUNKS - 1, gu0_ref[...]))
    if final_norm:
        y = _rms(y, fg_ref[...])
    out_ref[...] = y


def _ffn_call(x, mod3, ng, wg, wu, wd, *, sub, rows_per_mod, mod_base, tm,
              mix=None, final_g=None, casts=()):
    t = x.shape[0]
    steps = t // tm
    fuse_mix = mix is not None
    final_norm = final_g is not None
    tiles_per_mod = rows_per_mod // tm

    def row_map(i):
        return (i, 0)

    def mod_map(i):
        return (mod_base + i // tiles_per_mod, 0, 0)

    const2 = lambda i: (0, 0)
    in_specs = [pl.BlockSpec((tm, D_MODEL), row_map)]
    args = [x]
    if fuse_mix:
        o, rg, wout = mix
        in_specs += [pl.BlockSpec((tm, D_ATTN), row_map),
                     pl.BlockSpec((tm, D_RG), row_map),
                     pl.BlockSpec(wout.shape, const2, pipeline_mode=pl.Buffered(1))]
        args += [o, rg, wout]
    in_specs += [pl.BlockSpec((1, N_MOD, D_MODEL), mod_map),
                 pl.BlockSpec(ng.shape, const2),
                 pl.BlockSpec(wg.shape, const2, pipeline_mode=pl.Buffered(1)),
                 pl.BlockSpec(wu.shape, const2, pipeline_mode=pl.Buffered(1)),
                 pl.BlockSpec(wd.shape, const2, pipeline_mode=pl.Buffered(1))]
    args += [mod3, ng, wg, wu, wd]
    if final_norm:
        in_specs.append(pl.BlockSpec((1, D_MODEL), const2))
        args.append(final_g)
    out_specs = [pl.BlockSpec((tm, D_MODEL), row_map)]
    out_shape = [jax.ShapeDtypeStruct((t, D_MODEL), F32)]
    for arr, lead in casts:
        rows, cols = arr.shape[-2:]
        rb = rows // steps
        assert rows % steps == 0 and rb % (2 * SUBLANES) == 0
        in_specs.append(pl.BlockSpec((None,) * len(lead) + (rb, cols),
                                     lambda i, lead=lead: lead + (i, 0)))
        args.append(arr)
        out_specs.append(pl.BlockSpec((rb, cols), row_map))
        out_shape.append(jax.ShapeDtypeStruct((rows, cols), BF16))
    outs = pl.pallas_call(
        functools.partial(_ffn_kernel, sub=sub, fuse_mix=fuse_mix, final_norm=final_norm,
                          n_casts=len(casts)),
        grid=(steps,),
        in_specs=in_specs,
        out_specs=out_specs,
        out_shape=out_shape,
        scratch_shapes=[pltpu.VMEM((tm, D_MODEL), BF16),
                        pltpu.VMEM((tm, 2 * FF_CHUNK), F32), pltpu.VMEM((tm, 2 * FF_CHUNK), F32)],
        compiler_params=pltpu.CompilerParams(
            dimension_semantics=("arbitrary",), vmem_limit_bytes=VMEM_LIMIT),
        name="ffn%d" % sub,
    )(*args)
    return outs[0], list(outs[1:])


def _rope(x, cos, sin_signed, first_half):
    outs = []
    for cblk in range(x.shape[1] // LANES):
        xs = x[:, cblk * LANES:(cblk + 1) * LANES]
        partner = jnp.where(first_half, pltpu.roll(xs, LANES - 16, axis=1),
                            pltpu.roll(xs, 16, axis=1))
        outs.append(xs * cos + partner * sin_signed)
    return jnp.concatenate(outs, axis=1)


def _proj_kernel(*refs, rope):
    it = iter(refs)
    x_ref = next(it)
    mod_ref = next(it)
    ng_ref = next(it)
    win_ref = next(it)
    if rope:
        cos_ref = next(it)
        sin_ref = next(it)
    q_ref, k_ref, v_ref, xr_ref, xg_ref = it

    x = x_ref[...]
    sh = mod_ref[0, 3:4, :]
    sc = mod_ref[0, 4:5, :]
    h = (_rms(x, ng_ref[1:2, :]) * (1.0 + sc) + sh).astype(BF16)

    def col(j):
        return jnp.dot(h, win_ref[:, j * D_ATTN:(j + 1) * D_ATTN], preferred_element_type=F32)

    q = col(0)
    k = col(1)
    if rope:
        cos = cos_ref[...]
        sin = sin_ref[...]
        lane = lax.broadcasted_iota(jnp.int32, (1, LANES), 1)
        first_half = (lane % 32) < 16
        q = _rope(q, cos, sin, first_half)
        k = _rope(k, cos, sin, first_half)
    q_ref[...] = (q * (HEAD_DIM ** -0.5 * LOG2_E)).astype(q_ref.dtype)
    v = col(2)
    if rope:
        k_ref[...] = k.astype(k_ref.dtype)
        v_ref[...] = v.astype(v_ref.dtype)
    else:
        tm = x.shape[0]
        for hd in range(N_HEADS):
            k_ref[pl.ds(hd, tm, stride=N_HEADS), :] = k[:, hd * V_DIM:(hd + 1) * V_DIM]
            v_ref[pl.ds(hd, tm, stride=N_HEADS), :] = v[:, hd * V_DIM:(hd + 1) * V_DIM]
    xr_ref[...] = col(3)
    xg_ref[...] = col(4)


def _proj_call(x, mod3, ng, win, *, rows_per_mod, mod_base, tm, rope_tabs=None):
    t = x.shape[0]
    rope = rope_tabs is not None
    tiles_per_mod = rows_per_mod // tm
    row_map = lambda i: (i, 0)
    const2 = lambda i: (0, 0)
    in_specs = [pl.BlockSpec((tm, D_MODEL), row_map),
                pl.BlockSpec((1, N_MOD, D_MODEL), lambda i: (mod_base + i // tiles_per_mod, 0, 0)),
                pl.BlockSpec(ng.shape, const2),
                pl.BlockSpec(win.shape, const2, pipeline_mode=pl.Buffered(1))]
    args = [x, mod3, ng, win]
    if rope:
        cos, sin = rope_tabs
        tiles_per_seq = cos.shape[0] // tm
        tab_map = lambda i: (i % tiles_per_seq, 0)
        in_specs += [pl.BlockSpec((tm, LANES), tab_map), pl.BlockSpec((tm, LANES), tab_map)]
        args += [cos, sin]
    half = pl.BlockSpec((tm, D_ATTN), row_map)
    if rope:
        kv_spec, kv_shape = half, jax.ShapeDtypeStruct((t, D_ATTN), BF16)
    else:
        kv_spec = pl.BlockSpec((tm * N_HEADS, V_DIM), row_map)
        kv_shape = jax.ShapeDtypeStruct((t * N_HEADS, V_DIM), F32)
    return pl.pallas_call(
        functools.partial(_proj_kernel, rope=rope),
        grid=(t // tm,),
        in_specs=in_specs,
        out_specs=[half, kv_spec, kv_spec, half, half],
        out_shape=[jax.ShapeDtypeStruct((t, D_ATTN), BF16),
                   kv_shape,
                   kv_shape,
                   jax.ShapeDtypeStruct((t, D_RG), F32),
                   jax.ShapeDtypeStruct((t, D_RG), F32)],
        compiler_params=pltpu.CompilerParams(
            dimension_semantics=("arbitrary",), vmem_limit_bytes=VMEM_LIMIT),
        name="proj",
    )(*args)


def _diff_lambda(dl_ref):
    dl = dl_ref[...]
    return (jnp.exp(jnp.sum(dl[0:1] * dl[1:2], axis=-1, keepdims=True))
            - jnp.exp(jnp.sum(dl[2:3] * dl[3:4], axis=-1, keepdims=True)) + LAM_INIT)


def _stack_maps(q):
    map0 = lax.broadcasted_iota(jnp.int32, (1, V_DIM), 1) < HEAD_DIM
    zero = jnp.zeros_like(q)
    return jnp.concatenate([jnp.where(map0, q, zero), jnp.where(map0, zero, q)], axis=0)


def _finish_head(o2, l, lam, sg, tq):
    o = o2[:tq] * (1.0 / l[:tq]) - o2[tq:] * (lam / l[tq:])
    return _rms(o, sg) * (1.0 - LAM_INIT)


def _attn_small_kernel(q_ref, k_ref, v_ref, dl_ref, sg_ref, o_ref, *, tq):
    lam = _diff_lambda(dl_ref)
    nseq = q_ref.shape[0]
    sk = k_ref.shape[0] // (N_HEADS * nseq)
    ones = jnp.ones((sk, V_DIM), BF16)
    for bi in range(nseq):
        for hd in range(N_HEADS):
            cols = slice(hd * V_DIM, (hd + 1) * V_DIM)
            head_rows = pl.ds(bi * sk * N_HEADS + hd, sk, stride=N_HEADS)
            q2 = _stack_maps(q_ref[bi, :, cols])
            k = k_ref[head_rows, :].astype(BF16)
            s = lax.dot_general(q2, k, (((1,), (1,)), ((), ())), preferred_element_type=F32)
            e = jnp.exp2(s - jnp.max(s, axis=-1, keepdims=True))
            v1 = jnp.concatenate([v_ref[head_rows, :].astype(BF16), ones], axis=1)
            o2 = jnp.dot(e.astype(BF16), v1, preferred_element_type=F32)
            o = _finish_head(o2[:, :V_DIM], o2[:, V_DIM:], lam, sg_ref[...], tq)
            o_ref[bi, :, cols] = o.astype(o_ref.dtype)


def _attn_pipe_kernel(q_ref, k_ref, v_ref, kc_ref, vc_ref, dl_ref, sg_ref, o_ref, *scratch, tq, kt):
    lam = _diff_lambda(dl_ref)
    s_bufs, scratch = scratch[:S_BUFS], scratch[S_BUFS:]
    sc_bufs, scratch = scratch[:S_BUFS], scratch[S_BUFS:]
    p_bufs, pc_bufs = scratch[0:2], scratch[2:4]
    q2_scr, mv_scr, mb_scr, oacc_scr = scratch[4:]
    n_steps = k_ref.shape[1] // kt
    for t in range(N_HEADS + 2):
        ha, hb, hc = t, t - 1, t - 2
        do_a, do_b, do_c = 0 <= ha < N_HEADS, 0 <= hb < N_HEADS, 0 <= hc < N_HEADS
        cols_a = slice(ha * V_DIM, (ha + 1) * V_DIM)
        cols_c = slice(hc * V_DIM, (hc + 1) * V_DIM)
        if do_a:
            q2_scr[...] = _stack_maps(q_ref[0, :, cols_a])
            mv_scr[...] = jnp.full(mv_scr.shape, -jnp.inf, F32)
        if do_c:
            oacc_scr[...] = jnp.zeros(oacc_scr.shape, F32)

        def key_block(k_blk, v_blk, s_a, s_b, p_b, p_c):
            whole = (slice(None), slice(None))
            if do_a:
                s_ref, s_idx = s_a
                kb = k_blk()
                for c0 in range(0, kb.shape[0], MXU_N):
                    s = lax.dot_general(q2_scr[...], kb[c0:c0 + MXU_N], (((1,), (1,)), ((), ())),
                                        preferred_element_type=F32)
                    s_ref[s_idx + (slice(None), slice(c0, c0 + MXU_N))] = s
                    mv_scr[...] = jnp.maximum(mv_scr[...],
                                              jnp.maximum(s[:, :LANES], s[:, LANES:]))
            if do_b:
                s_ref, s_idx = s_b
                p_ref, p_idx = p_b
                n_keys = s_ref.shape[-1]
                for r0 in range(0, 2 * tq, EXP_ROWS):
                    rows = slice(r0, r0 + EXP_ROWS)
                    mb = mb_scr[rows, :]
                    for c0 in range(0, n_keys, LANES):
                        at = (rows, slice(c0, c0 + LANES))
                        p_ref[p_idx + at] = jnp.exp2(s_ref[s_idx + at] - mb).astype(BF16)
            if do_c:
                v = v_blk()
                v1 = jnp.concatenate([v, jnp.ones(v.shape, BF16)], axis=1)
                oacc_scr[...] += jnp.dot(p_c[0][p_c[1] + whole], v1, preferred_element_type=F32)

        def new_keys_step(j, carry):
            keys = pl.ds(pl.multiple_of(j * kt, kt), kt)
            key_block(lambda: k_ref[0, keys, cols_a], lambda: v_ref[0, keys, cols_c],
                      (s_bufs[ha % S_BUFS], (j,)), (s_bufs[hb % S_BUFS], (j,)),
                      (p_bufs[hb % 2], (j,)), (p_bufs[hc % 2], (j,)))
            return carry

        lax.fori_loop(0, n_steps, new_keys_step, 0, unroll=True)
        key_block(lambda: kc_ref[0, :, cols_a], lambda: vc_ref[0, :, cols_c],
                  (sc_bufs[ha % S_BUFS], ()), (sc_bufs[hb % S_BUFS], ()),
                  (pc_bufs[hb % 2], ()), (pc_bufs[hc % 2], ()))
        if do_a:
            m = jnp.max(mv_scr[...], axis=-1, keepdims=True)
            mb_scr[...] = jnp.broadcast_to(m, mb_scr.shape)
        if do_c:
            o = _finish_head(oacc_scr[:, :V_DIM], oacc_scr[:, V_DIM:], lam, sg_ref[...], tq)
            o_ref[0, :, cols_c] = o.astype(o_ref.dtype)


def _attn_call(q, k, v, diff_lambda, subln_g, *, tq, cache=None):
    b, sq, _ = q.shape
    qmap = lambda bi, qi: (bi, qi, 0)
    seqmap = lambda bi, qi: (bi, 0, 0)
    const2 = lambda bi, qi: (0, 0)
    nseq = 1 if cache is not None else SEQS_PER_STEP
    in_specs = [pl.BlockSpec((nseq, tq, D_ATTN), qmap)]
    args = [q, k, v]
    if cache is not None:
        sk = k.shape[1]
        in_specs += [pl.BlockSpec((1, sk, D_ATTN), seqmap, pipeline_mode=pl.Buffered(1))] * 2
        kc, vc = cache
        past = kc.shape[1]
        kt = 1024
        rows = 2 * tq
        body = functools.partial(_attn_pipe_kernel, tq=tq, kt=kt)
        in_specs += [pl.BlockSpec((1, past, D_ATTN), seqmap)] * 2
        args += [kc, vc]
        scratch = ([pltpu.VMEM((sk // kt, rows, kt), F32)] * S_BUFS
                   + [pltpu.VMEM((rows, past), F32)] * S_BUFS
                   + [pltpu.VMEM((sk // kt, rows, kt), BF16)] * 2 + [pltpu.VMEM((rows, past), BF16)] * 2
                   + [pltpu.VMEM((rows, V_DIM), BF16)] + [pltpu.VMEM((rows, LANES), F32)] * 2
                   + [pltpu.VMEM((rows, 2 * V_DIM), F32)])
    else:
        assert sq == tq and b % nseq == 0
        body = functools.partial(_attn_small_kernel, tq=tq)
        in_specs += [pl.BlockSpec((nseq * (k.shape[0] // b), V_DIM), lambda bi, qi: (bi, 0))] * 2
        scratch = []
    in_specs += [pl.BlockSpec(diff_lambda.shape, const2), pl.BlockSpec(subln_g.shape, const2)]
    args += [diff_lambda, subln_g]
    return pl.pallas_call(
        body,
        grid=(b // nseq, sq // tq),
        in_specs=in_specs,
        out_specs=pl.BlockSpec((nseq, tq, D_ATTN), qmap),
        out_shape=jax.ShapeDtypeStruct((b, sq, D_ATTN), BF16),
        scratch_shapes=scratch,
        compiler_params=pltpu.CompilerParams(
            dimension_semantics=("arbitrary", "arbitrary"), vmem_limit_bytes=VMEM_LIMIT),
        name="attn",
    )(*args)


def _scan8(a, b, reverse):
    sub = lax.broadcasted_iota(jnp.int32, a.shape, 1)
    for d in (1, 2, 4):
        if reverse:
            shift, m = SUBLANES - d, sub < SUBLANES - d
        else:
            shift, m = d, sub >= d
        a_s = pltpu.roll(a, shift, axis=1)
        b_s = pltpu.roll(b, shift, axis=1)
        b = jnp.where(m, a * b_s + b, b)
        a = jnp.where(m, a * a_s, a)
    return a, b


def _scan_tile(load_a, load_b, store_h, carry, reverse):
    order = range(SUBLANES - 1, -1, -1) if reverse else range(SUBLANES)
    hs, ps = {}, {}
    h = p = None
    for s in order:
        a, b = load_a(s), load_b(s)
        h = b if h is None else a * h + b
        p = a if p is None else a * p
        hs[s], ps[s] = h, p
    pp, hh = _scan8(p[None], h[None], reverse)
    after = hh[0] + pp[0] * carry
    sub = lax.broadcasted_iota(jnp.int32, after.shape, 0)
    if reverse:
        before = jnp.where(sub == SUBLANES - 1, carry, pltpu.roll(after, SUBLANES - 1, axis=0))
        carry = after[0:1]
    else:
        before = jnp.where(sub == 0, carry, pltpu.roll(after, 1, axis=0))
        carry = after[SUBLANES - 1:SUBLANES]
    for s in order:
        store_h(s, hs[s] + ps[s] * before)
    return carry


def _tile_rows(t0, q, s):
    return pl.ds(t0 + q * SCAN_ROWS + s, SUBLANES, stride=SUBLANES)


def _scan_block(load_a, load_b, h_ref, slab, t0, tb, carry, reverse):
    tiles = range(tb // SCAN_ROWS)
    for q in (reversed(tiles) if reverse else tiles):
        def store_h(s, h, q=q):
            h_ref[slab, _tile_rows(t0, q, s), :] = h

        carry = _scan_tile(functools.partial(load_a, q), functools.partial(load_b, q),
                           store_h, carry, reverse)
    return carry


def _softplus(x):
    return jnp.maximum(x, 0.0) + jnp.log(1.0 + jnp.exp(-jnp.abs(x)))


def _rglru_kernel(xr_ref, xg_ref, h0_ref, cw_ref, cb_ref, wg_ref, bg_ref, lam_ref,
                  y_ref, hl_ref, xpad, ab, bb, hf, *, s, cc, tb):
    nblk = s // tb
    nsub = cc // LANES
    ntile = tb // SCAN_ROWS
    cbase = pl.program_id(1) * nsub
    lanes = [slice(j * LANES, (j + 1) * LANES) for j in range(nsub)]

    def gate_ab(half_pre_r, half_pre_i, half_sp, half_xc):
        z = jnp.tanh(half_pre_r) * half_sp + half_sp
        ixc = jnp.tanh(half_pre_i) * half_xc + half_xc
        a = jnp.exp2(z * (-LOG2_E))
        u = (1.0 + a * a) * jnp.tanh(z)
        bx = u * lax.rsqrt(jnp.maximum(u, SQRT_FLOOR)) * ixc
        return a, bx

    params = []
    for j, ls in enumerate(lanes):
        xpad[j, 0:SUBLANES, :] = jnp.zeros((SUBLANES, LANES), F32)
        xpad[j, s + SUBLANES:s + 2 * SUBLANES, :] = jnp.zeros((SUBLANES, LANES), F32)
        xpad[j, SUBLANES:s + SUBLANES, :] = xr_ref[0, :, ls]
        params.append(dict(
            sp_f=(0.5 * RG_C) * _softplus(-lam_ref[0:1, ls]),
            sp_b=(0.5 * RG_C) * _softplus(-lam_ref[1:2, ls]),
            wg=wg_ref[cbase + j],
            bg=bg_ref[:, ls], cw=cw_ref[:, ls], cb=cb_ref[:, ls]))

    def fwd_lanes(j, t0, carry):
        p = params[j]
        cw, bg = p["cw"], p["bg"]
        xcs = []
        for q in range(ntile):
            xs = {m: xpad[j, _tile_rows(t0 + SUBLANES, q, m), :]
                  for m in range(-CONV_PAD_LEFT, SUBLANES + CONV_W - 1 - CONV_PAD_LEFT)}
            for sv in range(SUBLANES):
                acc = p["cb"]
                for tap in range(CONV_W):
                    acc = acc + xs[sv + tap - CONV_PAD_LEFT] * cw[tap:tap + 1]
                xcs.append(acc)
        xc = jnp.concatenate(xcs, axis=0)
        pre = jnp.dot(xc.astype(BF16), p["wg"], preferred_element_type=F32)
        hxc = 0.5 * xc
        a_f, b_f = gate_ab(pre[:, 0:128] + bg[0:1], pre[:, 128:256] + bg[1:2], p["sp_f"], hxc)
        a_b, b_b = gate_ab(pre[:, 256:384] + bg[2:3], pre[:, 384:512] + bg[3:4], p["sp_b"], hxc)
        ab[j, pl.ds(t0, tb), :] = a_b
        bb[j, pl.ds(t0, tb), :] = b_b

        def vreg_of(x):
            return lambda q, sv: x[q * SCAN_ROWS + sv * SUBLANES:
                                   q * SCAN_ROWS + (sv + 1) * SUBLANES]

        return _scan_block(vreg_of(a_f), vreg_of(b_f), hf, j, t0, tb, carry, False)

    def bwd_lanes(j, t0, carry):
        ls = lanes[j]

        def vreg_of(ref):
            return lambda q, sv: ref[j, pl.ds(t0 + q * SCAN_ROWS + sv * SUBLANES, SUBLANES), :]

        carry = _scan_block(vreg_of(ab), vreg_of(bb), bb, j, t0, tb, carry, True)
        y = ((hf[j, pl.ds(t0, tb), :] + bb[j, pl.ds(t0, tb), :])
             * _gelu_tanh(xg_ref[0, pl.ds(t0, tb), ls]))
        y_ref[0, pl.ds(t0, tb), ls] = y.astype(y_ref.dtype)
        return carry

    def fwd_block(t, carries):
        t0 = pl.multiple_of(t * tb, tb)
        return tuple(fwd_lanes(j, t0, carries[j]) for j in range(nsub))

    def bwd_block(tt, carries):
        t0 = pl.multiple_of((nblk - 1 - tt) * tb, tb)
        return tuple(bwd_lanes(j, t0, carries[j]) for j in range(nsub))

    ends = lax.fori_loop(0, nblk, fwd_block, tuple(h0_ref[0, 0:1, ls] for ls in lanes))
    for ls, end in zip(lanes, ends):
        hl_ref[0, 0:1, ls] = end
    ends = lax.fori_loop(0, nblk, bwd_block, tuple(h0_ref[0, 1:2, ls] for ls in lanes))
    for ls, end in zip(lanes, ends):
        hl_ref[0, 1:2, ls] = end


def _rglru_call(xr, xg, h0, cw, cb, wg, bg, lam, *, cc):
    b, s, _ = xr.shape
    nch = D_RG // cc
    tb = 256
    seq = pl.BlockSpec((1, s, cc), lambda bi, ci: (bi, 0, ci))
    st = pl.BlockSpec((1, 2, cc), lambda bi, ci: (bi, 0, ci))
    return pl.pallas_call(
        functools.partial(_rglru_kernel, s=s, cc=cc, tb=tb),
        grid=(b, nch),
        in_specs=[seq, seq, st,
                  pl.BlockSpec((CONV_W, cc), lambda bi, ci: (0, ci)),
                  pl.BlockSpec((1, cc), lambda bi, ci: (0, ci)),
                  pl.BlockSpec(wg.shape, lambda bi, ci: (0, 0, 0)),
                  pl.BlockSpec((4, cc), lambda bi, ci: (0, ci)),
                  pl.BlockSpec((2, cc), lambda bi, ci: (0, ci))],
        out_specs=[seq, st],
        out_shape=[jax.ShapeDtypeStruct((b, s, D_RG), BF16),
                   jax.ShapeDtypeStruct((b, 2, D_RG), F32)],
        scratch_shapes=[pltpu.VMEM((cc // LANES, s + 2 * SUBLANES, LANES), F32)]
        + [pltpu.VMEM((cc // LANES, s, LANES), F32)] * 3,
        compiler_params=pltpu.CompilerParams(
            dimension_semantics=("arbitrary", "arbitrary"), vmem_limit_bytes=VMEM_LIMIT),
        name="rglru",
    )(xr, xg, h0, cw, cb, wg, bg, lam)


def _rope_tables(s):
    pos = np.arange(s)
    row = (pos // GRID_W).astype(np.float32)
    col = (pos % GRID_W).astype(np.float32)
    n_freq = HEAD_DIM // 4
    inv_freq = (ROPE_BASE ** (-np.arange(n_freq, dtype=np.float32) / n_freq)).astype(np.float32)
    p = np.arange(LANES) % HEAD_DIM
    freq = inv_freq[p % n_freq]
    ang = (np.where((p < HEAD_DIM // 2)[None, :], row[:, None], col[:, None])
           * freq[None, :]).astype(np.float32)
    sign = np.where((p % (2 * n_freq)) < n_freq, -1.0, 1.0)
    cos = np.cos(ang.astype(np.float64)).astype(np.float32)
    sin = (np.sin(ang.astype(np.float64)) * sign[None, :]).astype(np.float32)
    return jnp.asarray(cos), jnp.asarray(sin)


def _gate_weights(w_r, w_i):
    def bd(w):
        w = w.reshape(4, 2, RG_BLOCK_W, RG_BLOCK_W)
        z = jnp.zeros_like(w[:, 0])
        top = jnp.concatenate([w[:, 0], z], axis=2)
        bot = jnp.concatenate([z, w[:, 1]], axis=2)
        return jnp.concatenate([top, bot], axis=1)
    w = jnp.concatenate([bd(w_r[0]), bd(w_i[0]), bd(w_r[1]), bd(w_i[1])], axis=2)
    return (0.5 * w).astype(BF16)


def kernel(x_prompt, x_sample, cache_attn_k, cache_attn_v, state_rglru, c, c_ctx, norm_g, w_mod, b_mod, ffn_w_gate, ffn_w_up, ffn_w_down, w_in, w_out, diff_lambda, subln_g, conv_w, conv_b, rg_w_r, rg_b_r, rg_w_i, rg_b_i, rg_lambda, final_g):
    l = 0
    bsz, seq, _ = x_prompt.shape
    dbsz, dseq, _ = x_sample.shape
    past = cache_attn_k.shape[2]

    ng = norm_g[l]
    ffn0_w = [w[l, 0].astype(BF16) for w in (ffn_w_gate, ffn_w_up, ffn_w_down)]
    win = w_in[l].astype(BF16)
    wg = _gate_weights(rg_w_r[l], rg_w_i[l])
    bg = 0.5 * jnp.stack([rg_b_r[l, 0], rg_b_i[l, 0], rg_b_r[l, 1], rg_b_i[l, 1]])
    cw = conv_w[l]
    cb = conv_b[l][None, :]
    lam = rg_lambda[l]
    dl = diff_lambda[l]
    sg = subln_g[l][None, :]
    fg = final_g[None, :]
    tm = 1024

    c8 = jnp.concatenate([c_ctx[None, :], c, jnp.zeros((SUBLANES - 1 - dbsz, D_MODEL), F32)], axis=0)
    mod3 = _mod_call(c8, w_mod[l], b_mod[l][None, :]).reshape(SUBLANES, N_MOD, D_MODEL)

    def mixers(x, *, rows_per_mod, mod_base, nb, s, k_ctx, v_ctx, h0, cc, tq, rope_tabs, casts):
        x1, cast = _ffn_call(x, mod3, ng, *ffn0_w, sub=0, rows_per_mod=rows_per_mod,
                             mod_base=mod_base, tm=tm, casts=casts)
        q, k, v, xr, xg = _proj_call(x1, mod3, ng, win, rows_per_mod=rows_per_mod,
                                     mod_base=mod_base, tm=tm, rope_tabs=rope_tabs)
        q3 = q.reshape(nb, s, D_ATTN)
        if k_ctx is None:
            o = _attn_call(q3, k, v, dl, sg, tq=tq)
        else:
            o = _attn_call(q3, k.reshape(nb, s, D_ATTN), v.reshape(nb, s, D_ATTN), dl, sg, tq=tq,
                           cache=(k_ctx, v_ctx))
        rg, h_last = _rglru_call(xr.reshape(nb, s, D_RG), xg.reshape(nb, s, D_RG), h0,
                                 cw, cb, wg, bg, lam, cc=cc)
        return x1, o.reshape(nb * s, D_ATTN), rg.reshape(nb * s, D_RG), k, v, h_last, cast

    x1p, op, rgp, k_new, v_new, h_new, (wgate1, wup1) = mixers(
        x_prompt.reshape(bsz * seq, D_MODEL), rows_per_mod=bsz * seq, mod_base=0, nb=bsz, s=seq,
        k_ctx=None, v_ctx=None, h0=jnp.zeros((bsz, 2, D_RG), F32), cc=D_RG, tq=seq,
        rope_tabs=None, casts=[(ffn_w_gate, (l, 1)), (ffn_w_up, (l, 1))])
    x1s, os_, rgs, _, _, _, (wdown1, wout) = mixers(
        x_sample.reshape(dbsz * dseq, D_MODEL), rows_per_mod=dseq, mod_base=1, nb=dbsz, s=dseq,
        k_ctx=cache_attn_k[:, l].reshape(dbsz, past, D_ATTN).astype(BF16),
        v_ctx=cache_attn_v[:, l].reshape(dbsz, past, D_ATTN).astype(BF16),
        h0=state_rglru[:, l], cc=2 * LANES, tq=256, rope_tabs=_rope_tables(dseq),
        casts=[(ffn_w_down, (l, 1)), (w_out, (l,))])

    def second_ffn(x1, o, rg, *, rows_per_mod, mod_base):
        y, _ = _ffn_call(x1, mod3, ng, wgate1, wup1, wdown1, sub=2, rows_per_mod=rows_per_mod,
                         mod_base=mod_base, tm=tm, mix=(o, rg, wout), final_g=fg)
        return y

    yp = second_ffn(x1p, op, rgp, rows_per_mod=bsz * seq, mod_base=0)
    ys = second_ffn(x1s, os_, rgs, rows_per_mod=dseq, mod_base=1)

    return (yp.reshape(bsz, seq, D_MODEL),
            ys.reshape(dbsz, dseq, D_MODEL),
            k_new.reshape(bsz, 1, seq, N_HEADS, V_DIM),
            v_new.reshape(bsz, 1, seq, N_HEADS, V_DIM),
            h_new.reshape(bsz, 1, 2, D_RG))
```

```python
import functools
import math

import jax
import jax.numpy as jnp
import numpy as np
from jax import lax
from jax.experimental import pallas as pl
from jax.experimental.pallas import tpu as pltpu

F32 = jnp.float32
BF16 = jnp.bfloat16

D_MODEL = 1024
N_HEADS = 4
HEAD_DIM = 64
V_DIM = 2 * HEAD_DIM
D_ATTN = N_HEADS * V_DIM
D_RG = 512
RG_BLOCK_W = 64
RG_C = 8.0
CONV_W = 4
CONV_PAD_LEFT = 2
D_FF = 2816
N_MOD = 9
GRID_W = 64
ROPE_BASE = 10000.0
EPS = 1e-6
LAM_INIT = 0.8 - 0.6 * math.exp(-0.3 * 0)
LOG2_E = math.log2(math.e)

LANES = 128
SUBLANES = 8
MXU_N = 256
S_BUFS = 2
SEQS_PER_STEP = 2
SQRT_FLOOR = 1e-30
SCAN_ROWS = SUBLANES * SUBLANES
EXP_ROWS = 64
FF_CHUNK = MXU_N
N_FF_CHUNKS = D_FF // FF_CHUNK
VMEM_LIMIT = 56 * 1024 * 1024


def _sigmoid(x):
    return 1.0 / (1.0 + jnp.exp(-x))


def _gelu_tanh(x):
    return 0.5 * x * (1.0 + jnp.tanh(math.sqrt(2.0 / math.pi) * (x + 0.044715 * (x * x * x))))


def _rms(x, g):
    ms = jnp.mean(x * x, axis=-1, keepdims=True)
    return x * lax.rsqrt(ms + EPS) * g


def _mod_kernel(c_ref, w_ref, b_ref, o_ref):
    c = c_ref[...]
    s = (c * _sigmoid(c)).astype(BF16)
    o_ref[...] = jnp.dot(s, w_ref[...].astype(BF16), preferred_element_type=F32) + b_ref[...]


def _mod_call(c8, w_mod, b_mod):
    n = w_mod.shape[1]
    tn = 1536
    return pl.pallas_call(
        _mod_kernel,
        grid=(n // tn,),
        in_specs=[
            pl.BlockSpec((SUBLANES, D_MODEL), lambda j: (0, 0)),
            pl.BlockSpec((D_MODEL, tn), lambda j: (0, j)),
            pl.BlockSpec((1, tn), lambda j: (0, j)),
        ],
        out_specs=pl.BlockSpec((SUBLANES, tn), lambda j: (0, j)),
        out_shape=jax.ShapeDtypeStruct((SUBLANES, n), F32),
        compiler_params=pltpu.CompilerParams(
            dimension_semantics=("arbitrary",), vmem_limit_bytes=VMEM_LIMIT),
        name="mod",
    )(c8, w_mod, b_mod)


def _ffn_kernel(*refs, sub, fuse_mix, final_norm, n_casts):
    it = iter(refs)
    x_ref = next(it)
    if fuse_mix:
        o_ref_in = next(it)
        rg_ref = next(it)
        wout_ref = next(it)
    mod_ref = next(it)
    ng_ref = next(it)
    wg_ref = next(it)
    wu_ref = next(it)
    wd_ref = next(it)
    fg_ref = next(it) if final_norm else None
    cast_in = [next(it) for _ in range(n_casts)]
    out_ref = next(it)
    for src in cast_in:
        dst = next(it)
        dst[...] = src[...].astype(dst.dtype)
    acc_ref = out_ref
    h_ref = next(it)
    gu0_ref = next(it)
    gu1_ref = next(it)

    x = x_ref[...]
    if fuse_mix:
        mix = jnp.dot(o_ref_in[...], wout_ref[0:D_ATTN, :], preferred_element_type=F32)
        mix = mix + jnp.dot(rg_ref[...], wout_ref[D_ATTN:, :], preferred_element_type=F32)
        x = x + mod_ref[0, 5:6, :] * mix
    sh = mod_ref[0, 3 * sub:3 * sub + 1, :]
    sc = mod_ref[0, 3 * sub + 1:3 * sub + 2, :]
    gate = mod_ref[0, 3 * sub + 2:3 * sub + 3, :]
    h_ref[...] = (_rms(x, ng_ref[sub:sub + 1, :]) * (1.0 + sc) + sh).astype(BF16)

    def chunk(j):
        return pl.ds(pl.multiple_of(j * FF_CHUNK, FF_CHUNK), FF_CHUNK)

    def gate_up(j):
        h = h_ref[...]
        g = jnp.dot(h, wg_ref[:, chunk(j)], preferred_element_type=F32)
        u = jnp.dot(h, wu_ref[:, chunk(j)], preferred_element_type=F32)
        return (g * _sigmoid(g) * u).astype(BF16)

    def down(j, a):
        return jnp.dot(a, wd_ref[chunk(j), :], preferred_element_type=F32)

    gu0_ref[...] = gate_up(0)
    gu1_ref[...] = gate_up(1)
    acc_ref[...] = down(0, gu0_ref[...])

    def body(i, carry):
        gu0_ref[...] = gate_up(2 * i + 2)
        acc_ref[...] += down(2 * i + 1, gu1_ref[...])
        gu1_ref[...] = gate_up(2 * i + 3)
        acc_ref[...] += down(2 * i + 2, gu0_ref[...])
        return carry

    assert N_FF_CHUNKS % 2 == 1 and N_FF_CHUNKS >= 5
    lax.fori_loop(0, (N_FF_CHUNKS - 3) // 2, body, 0, unroll=2)
    gu0_ref[...] = gate_up(N_FF_CHUNKS - 1)
    acc_ref[...] += down(N_FF_CHUNKS - 2, gu1_ref[...])
    y = x + (0.5 * gate) * (acc_ref[...] + down(N_FF_CHUNKS - 1, gu0_ref[...]))
    if final_norm:
        y = _rms(y, fg_ref[...])
    out_ref[...] = y


def _ffn_call(x, mod3, ng, wg, wu, wd, *, sub, rows_per_mod, mod_base, tm,
              mix=None, final_g=None, casts=()):
    t = x.shape[0]
    steps = t // tm
    fuse_mix = mix is not None
    final_norm = final_g is not None
    tiles_per_mod = rows_per_mod // tm

    def row_map(i):
        return (i, 0)

    def mod_map(i):
        return (mod_base + i // tiles_per_mod, 0, 0)

    const2 = lambda i: (0, 0)
    in_specs = [pl.BlockSpec((tm, D_MODEL), row_map)]
    args = [x]
    if fuse_mix:
        o, rg, wout = mix
        in_specs += [pl.BlockSpec((tm, D_ATTN), row_map),
                     pl.BlockSpec((tm, D_RG), row_map),
                     pl.BlockSpec(wout.shape, const2, pipeline_mode=pl.Buffered(1))]
        args += [o, rg, wout]
    in_specs += [pl.BlockSpec((1, N_MOD, D_MODEL), mod_map),
                 pl.BlockSpec(ng.shape, const2),
                 pl.BlockSpec(wg.shape, const2, pipeline_mode=pl.Buffered(1)),
                 pl.BlockSpec(wu.shape, const2, pipeline_mode=pl.Buffered(1)),
                 pl.BlockSpec(wd.shape, const2, pipeline_mode=pl.Buffered(1))]
    args += [mod3, ng, wg, wu, wd]
    if final_norm:
        in_specs.append(pl.BlockSpec((1, D_MODEL), const2))
        args.append(final_g)
    out_specs = [pl.BlockSpec((tm, D_MODEL), row_map)]
    out_shape = [jax.ShapeDtypeStruct((t, D_MODEL), F32)]
    for arr, lead in casts:
        rows, cols = arr.shape[-2:]
        rb = rows // steps
        assert rows % steps == 0 and rb % (2 * SUBLANES) == 0
        in_specs.append(pl.BlockSpec((None,) * len(lead) + (rb, cols),
                                     lambda i, lead=lead: lead + (i, 0)))
        args.append(arr)
        out_specs.append(pl.BlockSpec((rb, cols), row_map))
        out_shape.append(jax.ShapeDtypeStruct((rows, cols), BF16))
    outs = pl.pallas_call(
        functools.partial(_ffn_kernel, sub=sub, fuse_mix=fuse_mix, final_norm=final_norm,
                          n_casts=len(casts)),
        grid=(steps,),
        in_specs=in_specs,
        out_specs=out_specs,
        out_shape=out_shape,
        scratch_shapes=[pltpu.VMEM((tm, D_MODEL), BF16),
                        pltpu.VMEM((tm, FF_CHUNK), BF16), pltpu.VMEM((tm, FF_CHUNK), BF16)],
        compiler_params=pltpu.CompilerParams(
            dimension_semantics=("arbitrary",), vmem_limit_bytes=VMEM_LIMIT),
        name="ffn%d" % sub,
    )(*args)
    return outs[0], list(outs[1:])


def _rope(x, cos, sin_signed, first_half):
    outs = []
    for cblk in range(x.shape[1] // LANES):
        xs = x[:, cblk * LANES:(cblk + 1) * LANES]
        partner = jnp.where(first_half, pltpu.roll(xs, LANES - 16, axis=1),
                            pltpu.roll(xs, 16, axis=1))
        outs.append(xs * cos + partner * sin_signed)
    return jnp.concatenate(outs, axis=1)


def _proj_kernel(*refs, rope):
    it = iter(refs)
    x_ref = next(it)
    mod_ref = next(it)
    ng_ref = next(it)
    win_ref = next(it)
    if rope:
        cos_ref = next(it)
        sin_ref = next(it)
    q_ref, k_ref, v_ref, xr_ref, xg_ref = it

    x = x_ref[...]
    sh = mod_ref[0, 3:4, :]
    sc = mod_ref[0, 4:5, :]
    h = (_rms(x, ng_ref[1:2, :]) * (1.0 + sc) + sh).astype(BF16)

    def col(j):
        return jnp.dot(h, win_ref[:, j * D_ATTN:(j + 1) * D_ATTN], preferred_element_type=F32)

    q = col(0)
    k = col(1)
    if rope:
        cos = cos_ref[...]
        sin = sin_ref[...]
        lane = lax.broadcasted_iota(jnp.int32, (1, LANES), 1)
        first_half = (lane % 32) < 16
        q = _rope(q, cos, sin, first_half)
        k = _rope(k, cos, sin, first_half)
    q_ref[...] = (q * (HEAD_DIM ** -0.5 * LOG2_E)).astype(q_ref.dtype)
    v = col(2)
    if rope:
        k_ref[...] = k.astype(k_ref.dtype)
        v_ref[...] = v.astype(v_ref.dtype)
    else:
        tm = x.shape[0]
        for hd in range(N_HEADS):
            k_ref[pl.ds(hd, tm, stride=N_HEADS), :] = k[:, hd * V_DIM:(hd + 1) * V_DIM]
            v_ref[pl.ds(hd, tm, stride=N_HEADS), :] = v[:, hd * V_DIM:(hd + 1) * V_DIM]
    xr_ref[...] = col(3)
    xg_ref[...] = col(4)


def _proj_call(x, mod3, ng, win, *, rows_per_mod, mod_base, tm, rope_tabs=None):
    t = x.shape[0]
    rope = rope_tabs is not None
    tiles_per_mod = rows_per_mod // tm
    row_map = lambda i: (i, 0)
    const2 = lambda i: (0, 0)
    in_specs = [pl.BlockSpec((tm, D_MODEL), row_map),
                pl.BlockSpec((1, N_MOD, D_MODEL), lambda i: (mod_base + i // tiles_per_mod, 0, 0)),
                pl.BlockSpec(ng.shape, const2),
                pl.BlockSpec(win.shape, const2, pipeline_mode=pl.Buffered(1))]
    args = [x, mod3, ng, win]
    if rope:
        cos, sin = rope_tabs
        tiles_per_seq = cos.shape[0] // tm
        tab_map = lambda i: (i % tiles_per_seq, 0)
        in_specs += [pl.BlockSpec((tm, LANES), tab_map), pl.BlockSpec((tm, LANES), tab_map)]
        args += [cos, sin]
    half = pl.BlockSpec((tm, D_ATTN), row_map)
    if rope:
        kv_spec, kv_shape = half, jax.ShapeDtypeStruct((t, D_ATTN), BF16)
    else:
        kv_spec = pl.BlockSpec((tm * N_HEADS, V_DIM), row_map)
        kv_shape = jax.ShapeDtypeStruct((t * N_HEADS, V_DIM), F32)
    return pl.pallas_call(
        functools.partial(_proj_kernel, rope=rope),
        grid=(t // tm,),
        in_specs=in_specs,
        out_specs=[half, kv_spec, kv_spec, half, half],
        out_shape=[jax.ShapeDtypeStruct((t, D_ATTN), BF16),
                   kv_shape,
                   kv_shape,
                   jax.ShapeDtypeStruct((t, D_RG), F32),
                   jax.ShapeDtypeStruct((t, D_RG), F32)],
        compiler_params=pltpu.CompilerParams(
            dimension_semantics=("arbitrary",), vmem_limit_bytes=VMEM_LIMIT),
        name="proj",
    )(*args)


def _diff_lambda(dl_ref):
    dl = dl_ref[...]
    return (jnp.exp(jnp.sum(dl[0:1] * dl[1:2], axis=-1, keepdims=True))
            - jnp.exp(jnp.sum(dl[2:3] * dl[3:4], axis=-1, keepdims=True)) + LAM_INIT)


def _stack_maps(q):
    map0 = lax.broadcasted_iota(jnp.int32, (1, V_DIM), 1) < HEAD_DIM
    zero = jnp.zeros_like(q)
    return jnp.concatenate([jnp.where(map0, q, zero), jnp.where(map0, zero, q)], axis=0)


def _finish_head(o2, l, lam, sg, tq):
    o = o2[:tq] * (1.0 / l[:tq]) - o2[tq:] * (lam / l[tq:])
    return _rms(o, sg) * (1.0 - LAM_INIT)


def _attn_small_kernel(q_ref, k_ref, v_ref, dl_ref, sg_ref, o_ref, *, tq):
    lam = _diff_lambda(dl_ref)
    nseq = q_ref.shape[0]
    sk = k_ref.shape[0] // (N_HEADS * nseq)
    ones = jnp.ones((sk, V_DIM), BF16)
    for bi in range(nseq):
        for hd in range(N_HEADS):
            cols = slice(hd * V_DIM, (hd + 1) * V_DIM)
            head_rows = pl.ds(bi * sk * N_HEADS + hd, sk, stride=N_HEADS)
            q2 = _stack_maps(q_ref[bi, :, cols])
            k = k_ref[head_rows, :].astype(BF16)
            s = lax.dot_general(q2, k, (((1,), (1,)), ((), ())), preferred_element_type=F32)
            e = jnp.exp2(s - jnp.max(s, axis=-1, keepdims=True))
            v1 = jnp.concatenate([v_ref[head_rows, :].astype(BF16), ones], axis=1)
            o2 = jnp.dot(e.astype(BF16), v1, preferred_element_type=F32)
            o = _finish_head(o2[:, :V_DIM], o2[:, V_DIM:], lam, sg_ref[...], tq)
            o_ref[bi, :, cols] = o.astype(o_ref.dtype)


def _attn_pipe_kernel(q_ref, k_ref, v_ref, kc_ref, vc_ref, dl_ref, sg_ref, o_ref, *scratch, tq, kt):
    lam = _diff_lambda(dl_ref)
    s_bufs, scratch = scratch[:S_BUFS], scratch[S_BUFS:]
    sc_bufs, scratch = scratch[:S_BUFS], scratch[S_BUFS:]
    p_bufs, pc_bufs = scratch[0:2], scratch[2:4]
    q2_scr, mv_scr, mb_scr, oacc_scr = scratch[4:]
    n_steps = k_ref.shape[1] // kt
    for t in range(N_HEADS + 2):
        ha, hb, hc = t, t - 1, t - 2
        do_a, do_b, do_c = 0 <= ha < N_HEADS, 0 <= hb < N_HEADS, 0 <= hc < N_HEADS
        cols_a = slice(ha * V_DIM, (ha + 1) * V_DIM)
        cols_c = slice(hc * V_DIM, (hc + 1) * V_DIM)
        if do_a:
            q2_scr[...] = _stack_maps(q_ref[0, :, cols_a])
            mv_scr[...] = jnp.full(mv_scr.shape, -jnp.inf, F32)
        if do_c:
            oacc_scr[...] = jnp.zeros(oacc_scr.shape, F32)

        def key_block(k_blk, v_blk, s_a, s_b, p_b, p_c):
            whole = (slice(None), slice(None))
            if do_a:
                s_ref, s_idx = s_a
                kb = k_blk()
                for c0 in range(0, kb.shape[0], MXU_N):
                    s = lax.dot_general(q2_scr[...], kb[c0:c0 + MXU_N], (((1,), (1,)), ((), ())),
                                        preferred_element_type=F32)
                    s_ref[s_idx + (slice(None), slice(c0, c0 + MXU_N))] = s
                    mv_scr[...] = jnp.maximum(mv_scr[...],
                                              jnp.maximum(s[:, :LANES], s[:, LANES:]))
            if do_b:
                s_ref, s_idx = s_b
                p_ref, p_idx = p_b
                n_keys = s_ref.shape[-1]
                for r0 in range(0, 2 * tq, EXP_ROWS):
                    rows = slice(r0, r0 + EXP_ROWS)
                    mb = mb_scr[rows, :]
                    for c0 in range(0, n_keys, LANES):
                        at = (rows, slice(c0, c0 + LANES))
                        p_ref[p_idx + at] = jnp.exp2(s_ref[s_idx + at] - mb).astype(BF16)
            if do_c:
                v = v_blk()
                v1 = jnp.concatenate([v, jnp.ones(v.shape, BF16)], axis=1)
                oacc_scr[...] += jnp.dot(p_c[0][p_c[1] + whole], v1, preferred_element_type=F32)

        def new_keys_step(j, carry):
            keys = pl.ds(pl.multiple_of(j * kt, kt), kt)
            key_block(lambda: k_ref[0, keys, cols_a], lambda: v_ref[0, keys, cols_c],
                      (s_bufs[ha % S_BUFS], (j,)), (s_bufs[hb % S_BUFS], (j,)),
                      (p_bufs[hb % 2], (j,)), (p_bufs[hc % 2], (j,)))
            return carry

        lax.fori_loop(0, n_steps, new_keys_step, 0, unroll=True)
        key_block(lambda: kc_ref[0, :, cols_a], lambda: vc_ref[0, :, cols_c],
                  (sc_bufs[ha % S_BUFS], ()), (sc_bufs[hb % S_BUFS], ()),
                  (pc_bufs[hb % 2], ()), (pc_bufs[hc % 2], ()))
        if do_a:
            m = jnp.max(mv_scr[...], axis=-1, keepdims=True)
            mb_scr[...] = jnp.broadcast_to(m, mb_scr.shape)
        if do_c:
            o = _finish_head(oacc_scr[:, :V_DIM], oacc_scr[:, V_DIM:], lam, sg_ref[...], tq)
            o_ref[0, :, cols_c] = o.astype(o_ref.dtype)


def _attn_call(q, k, v, diff_lambda, subln_g, *, tq, cache=None):
    b, sq, _ = q.shape
    qmap = lambda bi, qi: (bi, qi, 0)
    seqmap = lambda bi, qi: (bi, 0, 0)
    const2 = lambda bi, qi: (0, 0)
    nseq = 1 if cache is not None else SEQS_PER_STEP
    in_specs = [pl.BlockSpec((nseq, tq, D_ATTN), qmap)]
    args = [q, k, v]
    if cache is not None:
        sk = k.shape[1]
        in_specs += [pl.BlockSpec((1, sk, D_ATTN), seqmap, pipeline_mode=pl.Buffered(1))] * 2
        kc, vc = cache
        past = kc.shape[1]
        kt = 1024
        rows = 2 * tq
        body = functools.partial(_attn_pipe_kernel, tq=tq, kt=kt)
        in_specs += [pl.BlockSpec((1, past, D_ATTN), seqmap)] * 2
        args += [kc, vc]
        scratch = ([pltpu.VMEM((sk // kt, rows, kt), F32)] * S_BUFS
                   + [pltpu.VMEM((rows, past), F32)] * S_BUFS
                   + [pltpu.VMEM((sk // kt, rows, kt), BF16)] * 2 + [pltpu.VMEM((rows, past), BF16)] * 2
                   + [pltpu.VMEM((rows, V_DIM), BF16)] + [pltpu.VMEM((rows, LANES), F32)] * 2
                   + [pltpu.VMEM((rows, 2 * V_DIM), F32)])
    else:
        assert sq == tq and b % nseq == 0
        body = functools.partial(_attn_small_kernel, tq=tq)
        in_specs += [pl.BlockSpec((nseq * (k.shape[0] // b), V_DIM), lambda bi, qi: (bi, 0))] * 2
        scratch = []
    in_specs += [pl.BlockSpec(diff_lambda.shape, const2), pl.BlockSpec(subln_g.shape, const2)]
    args += [diff_lambda, subln_g]
    return pl.pallas_call(
        body,
        grid=(b // nseq, sq // tq),
        in_specs=in_specs,
        out_specs=pl.BlockSpec((nseq, tq, D_ATTN), qmap),
        out_shape=jax.ShapeDtypeStruct((b, sq, D_ATTN), BF16),
        scratch_shapes=scratch,
        compiler_params=pltpu.CompilerParams(
            dimension_semantics=("arbitrary", "arbitrary"), vmem_limit_bytes=VMEM_LIMIT),
        name="attn",
    )(*args)


def _scan8(a, b, reverse):
    sub = lax.broadcasted_iota(jnp.int32, a.shape, 1)
    for d in (1, 2, 4):
        if reverse:
            shift, m = SUBLANES - d, sub < SUBLANES - d
        else:
            shift, m = d, sub >= d
        a_s = pltpu.roll(a, shift, axis=1)
        b_s = pltpu.roll(b, shift, axis=1)
        b = jnp.where(m, a * b_s + b, b)
        a = jnp.where(m, a * a_s, a)
    return a, b


def _scan_tile(load_a, load_b, store_h, carry, reverse):
    order = range(SUBLANES - 1, -1, -1) if reverse else range(SUBLANES)
    hs, ps = {}, {}
    h = p = None
    for s in order:
        a, b = load_a(s), load_b(s)
        h = b if h is None else a * h + b
        p = a if p is None else a * p
        hs[s], ps[s] = h, p
    pp, hh = _scan8(p[None], h[None], reverse)
    after = hh[0] + pp[0] * carry
    sub = lax.broadcasted_iota(jnp.int32, after.shape, 0)
    if reverse:
        before = jnp.where(sub == SUBLANES - 1, carry, pltpu.roll(after, SUBLANES - 1, axis=0))
        carry = after[0:1]
    else:
        before = jnp.where(sub == 0, carry, pltpu.roll(after, 1, axis=0))
        carry = after[SUBLANES - 1:SUBLANES]
    for s in order:
        store_h(s, hs[s] + ps[s] * before)
    return carry


def _tile_rows(t0, q, s):
    return pl.ds(t0 + q * SCAN_ROWS + s, SUBLANES, stride=SUBLANES)


def _scan_block(load_a, load_b, h_ref, slab, t0, tb, carry, reverse):
    tiles = range(tb // SCAN_ROWS)
    for q in (reversed(tiles) if reverse else tiles):
        def store_h(s, h, q=q):
            h_ref[slab, _tile_rows(t0, q, s), :] = h

        carry = _scan_tile(functools.partial(load_a, q), functools.partial(load_b, q),
                           store_h, carry, reverse)
    return carry


def _softplus(x):
    return jnp.maximum(x, 0.0) + jnp.log(1.0 + jnp.exp(-jnp.abs(x)))


def _rglru_kernel(xr_ref, xg_ref, h0_ref, cw_ref, cb_ref, wg_ref, bg_ref, lam_ref,
                  y_ref, hl_ref, xpad, ab, bb, hf, *, s, cc, tb):
    nblk = s // tb
    nsub = cc // LANES
    ntile = tb // SCAN_ROWS
    cbase = pl.program_id(1) * nsub
    lanes = [slice(j * LANES, (j + 1) * LANES) for j in range(nsub)]

    def gate_ab(half_pre_r, half_pre_i, half_sp, half_xc):
        z = jnp.tanh(half_pre_r) * half_sp + half_sp
        ixc = jnp.tanh(half_pre_i) * half_xc + half_xc
        a = jnp.exp2(z * (-LOG2_E))
        u = (1.0 + a * a) * jnp.tanh(z)
        bx = u * lax.rsqrt(jnp.maximum(u, SQRT_FLOOR)) * ixc
        return a, bx

    params = []
    for j, ls in enumerate(lanes):
        xpad[j, 0:SUBLANES, :] = jnp.zeros((SUBLANES, LANES), F32)
        xpad[j, s + SUBLANES:s + 2 * SUBLANES, :] = jnp.zeros((SUBLANES, LANES), F32)
        xpad[j, SUBLANES:s + SUBLANES, :] = xr_ref[0, :, ls]
        params.append(dict(
            sp_f=(0.5 * RG_C) * _softplus(-lam_ref[0:1, ls]),
            sp_b=(0.5 * RG_C) * _softplus(-lam_ref[1:2, ls]),
            wg=wg_ref[cbase + j],
            bg=bg_ref[:, ls], cw=cw_ref[:, ls], cb=cb_ref[:, ls]))

    def fwd_lanes(j, t0, carry):
        p = params[j]
        cw, bg = p["cw"], p["bg"]
        xcs = []
        for q in range(ntile):
            xs = {m: xpad[j, _tile_rows(t0 + SUBLANES, q, m), :]
                  for m in range(-CONV_PAD_LEFT, SUBLANES + CONV_W - 1 - CONV_PAD_LEFT)}
            for sv in range(SUBLANES):
                acc = p["cb"]
                for tap in range(CONV_W):
                    acc = acc + xs[sv + tap - CONV_PAD_LEFT] * cw[tap:tap + 1]
                xcs.append(acc)
        xc = jnp.concatenate(xcs, axis=0)
        pre = jnp.dot(xc.astype(BF16), p["wg"], preferred_element_type=F32)
        hxc = 0.5 * xc
        a_f, b_f = gate_ab(pre[:, 0:128] + bg[0:1], pre[:, 128:256] + bg[1:2], p["sp_f"], hxc)
        a_b, b_b = gate_ab(pre[:, 256:384] + bg[2:3], pre[:, 384:512] + bg[3:4], p["sp_b"], hxc)
        ab[j, pl.ds(t0, tb), :] = a_b
        bb[j, pl.ds(t0, tb), :] = b_b

        def vreg_of(x):
            return lambda q, sv: x[q * SCAN_ROWS + sv * SUBLANES:
                                   q * SCAN_ROWS + (sv + 1) * SUBLANES]

        return _scan_block(vreg_of(a_f), vreg_of(b_f), hf, j, t0, tb, carry, False)

    def bwd_lanes(j, t0, carry):
        ls = lanes[j]

        def vreg_of(ref):
            return lambda q, sv: ref[j, pl.ds(t0 + q * SCAN_ROWS + sv * SUBLANES, SUBLANES), :]

        carry = _scan_block(vreg_of(ab), vreg_of(bb), bb, j, t0, tb, carry, True)
        y = ((hf[j, pl.ds(t0, tb), :] + bb[j, pl.ds(t0, tb), :])
             * _gelu_tanh(xg_ref[0, pl.ds(t0, tb), ls]))
        y_ref[0, pl.ds(t0, tb), ls] = y.astype(y_ref.dtype)
        return carry

    def fwd_block(t, carries):
        t0 = pl.multiple_of(t * tb, tb)
        return tuple(fwd_lanes(j, t0, carries[j]) for j in range(nsub))

    def bwd_block(tt, carries):
        t0 = pl.multiple_of((nblk - 1 - tt) * tb, tb)
        return tuple(bwd_lanes(j, t0, carries[j]) for j in range(nsub))

    ends = lax.fori_loop(0, nblk, fwd_block, tuple(h0_ref[0, 0:1, ls] for ls in lanes))
    for ls, end in zip(lanes, ends):
        hl_ref[0, 0:1, ls] = end
    ends = lax.fori_loop(0, nblk, bwd_block, tuple(h0_ref[0, 1:2, ls] for ls in lanes))
    for ls, end in zip(lanes, ends):
        hl_ref[0, 1:2, ls] = end


def _rglru_call(xr, xg, h0, cw, cb, wg, bg, lam, *, cc):
    b, s, _ = xr.shape
    nch = D_RG // cc
    tb = 256
    seq = pl.BlockSpec((1, s, cc), lambda bi, ci: (bi, 0, ci))
    st = pl.BlockSpec((1, 2, cc), lambda bi, ci: (bi, 0, ci))
    return pl.pallas_call(
        functools.partial(_rglru_kernel, s=s, cc=cc, tb=tb),
        grid=(b, nch),
        in_specs=[seq, seq, st,
                  pl.BlockSpec((CONV_W, cc), lambda bi, ci: (0, ci)),
                  pl.BlockSpec((1, cc), lambda bi, ci: (0, ci)),
                  pl.BlockSpec(wg.shape, lambda bi, ci: (0, 0, 0)),
                  pl.BlockSpec((4, cc), lambda bi, ci: (0, ci)),
                  pl.BlockSpec((2, cc), lambda bi, ci: (0, ci))],
        out_specs=[seq, st],
        out_shape=[jax.ShapeDtypeStruct((b, s, D_RG), BF16),
                   jax.ShapeDtypeStruct((b, 2, D_RG), F32)],
        scratch_shapes=[pltpu.VMEM((cc // LANES, s + 2 * SUBLANES, LANES), F32)]
        + [pltpu.VMEM((cc // LANES, s, LANES), F32)] * 3,
        compiler_params=pltpu.CompilerParams(
            dimension_semantics=("arbitrary", "arbitrary"), vmem_limit_bytes=VMEM_LIMIT),
        name="rglru",
    )(xr, xg, h0, cw, cb, wg, bg, lam)


def _rope_tables(s):
    pos = np.arange(s)
    row = (pos // GRID_W).astype(np.float32)
    col = (pos % GRID_W).astype(np.float32)
    n_freq = HEAD_DIM // 4
    inv_freq = (ROPE_BASE ** (-np.arange(n_freq, dtype=np.float32) / n_freq)).astype(np.float32)
    p = np.arange(LANES) % HEAD_DIM
    freq = inv_freq[p % n_freq]
    ang = (np.where((p < HEAD_DIM // 2)[None, :], row[:, None], col[:, None])
           * freq[None, :]).astype(np.float32)
    sign = np.where((p % (2 * n_freq)) < n_freq, -1.0, 1.0)
    cos = np.cos(ang.astype(np.float64)).astype(np.float32)
    sin = (np.sin(ang.astype(np.float64)) * sign[None, :]).astype(np.float32)
    return jnp.asarray(cos), jnp.asarray(sin)


def _gate_weights(w_r, w_i):
    def bd(w):
        w = w.reshape(4, 2, RG_BLOCK_W, RG_BLOCK_W)
        z = jnp.zeros_like(w[:, 0])
        top = jnp.concatenate([w[:, 0], z], axis=2)
        bot = jnp.concatenate([z, w[:, 1]], axis=2)
        return jnp.concatenate([top, bot], axis=1)
    w = jnp.concatenate([bd(w_r[0]), bd(w_i[0]), bd(w_r[1]), bd(w_i[1])], axis=2)
    return (0.5 * w).astype(BF16)


def kernel(x_prompt, x_sample, cache_attn_k, cache_attn_v, state_rglru, c, c_ctx, norm_g, w_mod, b_mod, ffn_w_gate, ffn_w_up, ffn_w_down, w_in, w_out, diff_lambda, subln_g, conv_w, conv_b, rg_w_r, rg_b_r, rg_w_i, rg_b_i, rg_lambda, final_g):
    l = 0
    bsz, seq, _ = x_prompt.shape
    dbsz, dseq, _ = x_sample.shape
    past = cache_attn_k.shape[2]

    ng = norm_g[l]
    ffn0_w = [w[l, 0].astype(BF16) for w in (ffn_w_gate, ffn_w_up, ffn_w_down)]
    win = w_in[l].astype(BF16)
    wg = _gate_weights(rg_w_r[l], rg_w_i[l])
    bg = 0.5 * jnp.stack([rg_b_r[l, 0], rg_b_i[l, 0], rg_b_r[l, 1], rg_b_i[l, 1]])
    cw = conv_w[l]
    cb = conv_b[l][None, :]
    lam = rg_lambda[l]
    dl = diff_lambda[l]
    sg = subln_g[l][None, :]
    fg = final_g[None, :]
    tm = 1024

    c8 = jnp.concatenate([c_ctx[None, :], c, jnp.zeros((SUBLANES - 1 - dbsz, D_MODEL), F32)], axis=0)
    mod3 = _mod_call(c8, w_mod[l], b_mod[l][None, :]).reshape(SUBLANES, N_MOD, D_MODEL)

    def mixers(x, *, rows_per_mod, mod_base, nb, s, k_ctx, v_ctx, h0, cc, tq, rope_tabs, casts):
        x1, cast = _ffn_call(x, mod3, ng, *ffn0_w, sub=0, rows_per_mod=rows_per_mod,
                             mod_base=mod_base, tm=tm, casts=casts)
        q, k, v, xr, xg = _proj_call(x1, mod3, ng, win, rows_per_mod=rows_per_mod,
                                     mod_base=mod_base, tm=tm, rope_tabs=rope_tabs)
        q3 = q.reshape(nb, s, D_ATTN)
        if k_ctx is None:
            o = _attn_call(q3, k, v, dl, sg, tq=tq)
        else:
            o = _attn_call(q3, k.reshape(nb, s, D_ATTN), v.reshape(nb, s, D_ATTN), dl, sg, tq=tq,
                           cache=(k_ctx, v_ctx))
        rg, h_last = _rglru_call(xr.reshape(nb, s, D_RG), xg.reshape(nb, s, D_RG), h0,
                                 cw, cb, wg, bg, lam, cc=cc)
        return x1, o.reshape(nb * s, D_ATTN), rg.reshape(nb * s, D_RG), k, v, h_last, cast

    x1p, op, rgp, k_new, v_new, h_new, (wgate1, wup1) = mixers(
        x_prompt.reshape(bsz * seq, D_MODEL), rows_per_mod=bsz * seq, mod_base=0, nb=bsz, s=seq,
        k_ctx=None, v_ctx=None, h0=jnp.zeros((bsz, 2, D_RG), F32), cc=D_RG, tq=seq,
        rope_tabs=None, casts=[(ffn_w_gate, (l, 1)), (ffn_w_up, (l, 1))])
    x1s, os_, rgs, _, _, _, (wdown1, wout) = mixers(
        x_sample.reshape(dbsz * dseq, D_MODEL), rows_per_mod=dseq, mod_base=1, nb=dbsz, s=dseq,
        k_ctx=cache_attn_k[:, l].reshape(dbsz, past, D_ATTN).astype(BF16),
        v_ctx=cache_attn_v[:, l].reshape(dbsz, past, D_ATTN).astype(BF16),
        h0=state_rglru[:, l], cc=2 * LANES, tq=256, rope_tabs=_rope_tables(dseq),
        casts=[(ffn_w_down, (l, 1)), (w_out, (l,))])

    def second_ffn(x1, o, rg, *, rows_per_mod, mod_base):
        y, _ = _ffn_call(x1, mod3, ng, wgate1, wup1, wdown1, sub=2, rows_per_mod=rows_per_mod,
                         mod_base=mod_base, tm=tm, mix=(o, rg, wout), final_g=fg)
        return y

    yp = second_ffn(x1p, op, rgp, rows_per_mod=bsz * seq, mod_base=0)
    ys = second_ffn(x1s, os_, rgs, rows_per_mod=dseq, mod_base=1)

    return (yp.reshape(bsz, seq, D_MODEL),
            ys.reshape(dbsz, dseq, D_MODEL),
            k_new.reshape(bsz, 1, seq, N_HEADS, V_DIM),
            v_new.reshape(bsz, 1, seq, N_HEADS, V_DIM),
            h_new.reshape(bsz, 1, 2, D_RG))
```

```python
import functools
import math

import jax
import jax.numpy as jnp
import numpy as np
from jax import lax
from jax.experimental import pallas as pl
from jax.experimental.pallas import tpu as pltpu

F32 = jnp.float32
BF16 = jnp.bfloat16

D_MODEL = 1024
N_HEADS = 4
HEAD_DIM = 64
V_DIM = 2 * HEAD_DIM
D_ATTN = N_HEADS * V_DIM
D_RG = 512
RG_BLOCK_W = 64
RG_C = 8.0
CONV_W = 4
CONV_PAD_LEFT = 2
D_FF = 2816
N_MOD = 9
GRID_W = 64
ROPE_BASE = 10000.0
EPS = 1e-6
LAM_INIT = 0.8 - 0.6 * math.exp(-0.3 * 0)
LOG2_E = math.log2(math.e)

LANES = 128
SUBLANES = 8
MXU_N = 256
LATENT_TQ = 256
LATENT_CC = 2 * LANES
S_BUFS = 2
SEQS_PER_STEP = 2
SQRT_FLOOR = 1e-30
SCAN_ROWS = SUBLANES * SUBLANES
EXP_ROWS = 64
FF_CHUNK = MXU_N
N_FF_CHUNKS = D_FF // FF_CHUNK
VMEM_LIMIT = 56 * 1024 * 1024


def _sigmoid(x):
    return 1.0 / (1.0 + jnp.exp(-x))


def _gelu_tanh(x):
    return 0.5 * x * (1.0 + jnp.tanh(math.sqrt(2.0 / math.pi) * (x + 0.044715 * (x * x * x))))


def _rms(x, g):
    ms = jnp.mean(x * x, axis=-1, keepdims=True)
    return x * lax.rsqrt(ms + EPS) * g


def _mod_kernel(c_ref, w_ref, b_ref, o_ref):
    c = c_ref[...]
    s = (c * _sigmoid(c)).astype(BF16)
    o_ref[...] = jnp.dot(s, w_ref[...].astype(BF16), preferred_element_type=F32) + b_ref[...]


def _mod_call(c8, w_mod, b_mod):
    n = w_mod.shape[1]
    tn = 1536
    return pl.pallas_call(
        _mod_kernel,
        grid=(n // tn,),
        in_specs=[
            pl.BlockSpec((SUBLANES, D_MODEL), lambda j: (0, 0)),
            pl.BlockSpec((D_MODEL, tn), lambda j: (0, j)),
            pl.BlockSpec((1, tn), lambda j: (0, j)),
        ],
        out_specs=pl.BlockSpec((SUBLANES, tn), lambda j: (0, j)),
        out_shape=jax.ShapeDtypeStruct((SUBLANES, n), F32),
        compiler_params=pltpu.CompilerParams(
            dimension_semantics=("arbitrary",), vmem_limit_bytes=VMEM_LIMIT),
        name="mod",
    )(c8, w_mod, b_mod)


def _ffn_kernel(*refs, sub, fuse_mix, final_norm, n_casts):
    it = iter(refs)
    x_ref = next(it)
    if fuse_mix:
        o_ref_in = next(it)
        rg_ref = next(it)
        wout_ref = next(it)
    mod_ref = next(it)
    ng_ref = next(it)
    wg_ref = next(it)
    wu_ref = next(it)
    wd_ref = next(it)
    fg_ref = next(it) if final_norm else None
    cast_in = [next(it) for _ in range(n_casts)]
    out_ref = next(it)
    for src in cast_in:
        dst = next(it)
        dst[...] = src[...].astype(dst.dtype)
    acc_ref = out_ref
    h_ref = next(it)
    gu0_ref = next(it)
    gu1_ref = next(it)

    x = x_ref[...]
    if fuse_mix:
        mix = jnp.dot(o_ref_in[...], wout_ref[0:D_ATTN, :], preferred_element_type=F32)
        mix = mix + jnp.dot(rg_ref[...], wout_ref[D_ATTN:, :], preferred_element_type=F32)
        x = x + mod_ref[0, 5:6, :] * mix
    sh = mod_ref[0, 3 * sub:3 * sub + 1, :]
    sc = mod_ref[0, 3 * sub + 1:3 * sub + 2, :]
    gate = mod_ref[0, 3 * sub + 2:3 * sub + 3, :]
    h_ref[...] = (_rms(x, ng_ref[sub:sub + 1, :]) * (1.0 + sc) + sh).astype(BF16)

    def chunk(j):
        return pl.ds(pl.multiple_of(j * FF_CHUNK, FF_CHUNK), FF_CHUNK)

    def gate_up(j):
        h = h_ref[...]
        g = jnp.dot(h, wg_ref[:, chunk(j)], preferred_element_type=F32)
        u = jnp.dot(h, wu_ref[:, chunk(j)], preferred_element_type=F32)
        return (g * _sigmoid(g) * u).astype(BF16)

    def down(j, a):
        return jnp.dot(a, wd_ref[chunk(j), :], preferred_element_type=F32)

    gu0_ref[...] = gate_up(0)
    gu1_ref[...] = gate_up(1)
    acc_ref[...] = down(0, gu0_ref[...])

    def body(i, carry):
        gu0_ref[...] = gate_up(2 * i + 2)
        acc_ref[...] += down(2 * i + 1, gu1_ref[...])
        gu1_ref[...] = gate_up(2 * i + 3)
        acc_ref[...] += down(2 * i + 2, gu0_ref[...])
        return carry

    assert N_FF_CHUNKS % 2 == 1 and N_FF_CHUNKS >= 5
    lax.fori_loop(0, (N_FF_CHUNKS - 3) // 2, body, 0, unroll=2)
    gu0_ref[...] = gate_up(N_FF_CHUNKS - 1)
    acc_ref[...] += down(N_FF_CHUNKS - 2, gu1_ref[...])
    y = x + (0.5 * gate) * (acc_ref[...] + down(N_FF_CHUNKS - 1, gu0_ref[...]))
    if final_norm:
        y = _rms(y, fg_ref[...])
    out_ref[...] = y


def _ffn_call(x, mod3, ng, wg, wu, wd, *, sub, rows_per_mod, mod_base, tm,
              mix=None, final_g=None, casts=()):
    t = x.shape[0]
    steps = t // tm
    fuse_mix = mix is not None
    final_norm = final_g is not None
    tiles_per_mod = rows_per_mod // tm

    def row_map(i):
        return (i, 0)

    def mod_map(i):
        return (mod_base + i // tiles_per_mod, 0, 0)

    const2 = lambda i: (0, 0)
    in_specs = [pl.BlockSpec((tm, D_MODEL), row_map)]
    args = [x]
    if fuse_mix:
        o, rg, wout = mix
        in_specs += [pl.BlockSpec((tm, D_ATTN), row_map),
                     pl.BlockSpec((tm, D_RG), row_map),
                     pl.BlockSpec(wout.shape, const2, pipeline_mode=pl.Buffered(1))]
        args += [o, rg, wout]
    in_specs += [pl.BlockSpec((1, N_MOD, D_MODEL), mod_map),
                 pl.BlockSpec(ng.shape, const2),
                 pl.BlockSpec(wg.shape, const2, pipeline_mode=pl.Buffered(1)),
                 pl.BlockSpec(wu.shape, const2, pipeline_mode=pl.Buffered(1)),
                 pl.BlockSpec(wd.shape, const2, pipeline_mode=pl.Buffered(1))]
    args += [mod3, ng, wg, wu, wd]
    if final_norm:
        in_specs.append(pl.BlockSpec((1, D_MODEL), const2))
        args.append(final_g)
    out_specs = [pl.BlockSpec((tm, D_MODEL), row_map)]
    out_shape = [jax.ShapeDtypeStruct((t, D_MODEL), F32)]
    for arr, lead in casts:
        rows, cols = arr.shape[-2:]
        rb = rows // steps
        assert rows % steps == 0 and rb % (2 * SUBLANES) == 0
        in_specs.append(pl.BlockSpec((None,) * len(lead) + (rb, cols),
                                     lambda i, lead=lead: lead + (i, 0)))
        args.append(arr)
        out_specs.append(pl.BlockSpec((rb, cols), row_map))
        out_shape.append(jax.ShapeDtypeStruct((rows, cols), BF16))
    outs = pl.pallas_call(
        functools.partial(_ffn_kernel, sub=sub, fuse_mix=fuse_mix, final_norm=final_norm,
                          n_casts=len(casts)),
        grid=(steps,),
        in_specs=in_specs,
        out_specs=out_specs,
        out_shape=out_shape,
        scratch_shapes=[pltpu.VMEM((tm, D_MODEL), BF16),
                        pltpu.VMEM((tm, FF_CHUNK), BF16), pltpu.VMEM((tm, FF_CHUNK), BF16)],
        compiler_params=pltpu.CompilerParams(
            dimension_semantics=("arbitrary",), vmem_limit_bytes=VMEM_LIMIT),
        name="ffn%d" % sub,
    )(*args)
    return outs[0], list(outs[1:])


def _rope(x, cos, sin_signed, first_half):
    outs = []
    for cblk in range(x.shape[1] // LANES):
        xs = x[:, cblk * LANES:(cblk + 1) * LANES]
        partner = jnp.where(first_half, pltpu.roll(xs, LANES - 16, axis=1),
                            pltpu.roll(xs, 16, axis=1))
        outs.append(xs * cos + partner * sin_signed)
    return jnp.concatenate(outs, axis=1)


def _proj_kernel(*refs, rope):
    it = iter(refs)
    x_ref = next(it)
    mod_ref = next(it)
    ng_ref = next(it)
    win_ref = next(it)
    if rope:
        cos_ref = next(it)
        sin_ref = next(it)
    q_ref, k_ref, v_ref, xr_ref, xg_ref = it

    x = x_ref[...]
    sh = mod_ref[0, 3:4, :]
    sc = mod_ref[0, 4:5, :]
    h = (_rms(x, ng_ref[1:2, :]) * (1.0 + sc) + sh).astype(BF16)

    def col(j):
        return jnp.dot(h, win_ref[:, j * D_ATTN:(j + 1) * D_ATTN], preferred_element_type=F32)

    q = col(0)
    k = col(1)
    if rope:
        cos = cos_ref[...]
        sin = sin_ref[...]
        lane = lax.broadcasted_iota(jnp.int32, (1, LANES), 1)
        first_half = (lane % 32) < 16
        q = _rope(q, cos, sin, first_half)
        k = _rope(k, cos, sin, first_half)
    q_ref[...] = (q * (HEAD_DIM ** -0.5 * LOG2_E)).astype(q_ref.dtype)
    v = col(2)
    if rope:
        k_ref[...] = k.astype(k_ref.dtype)
        v_ref[...] = v.astype(v_ref.dtype)
    else:
        tm = x.shape[0]
        for hd in range(N_HEADS):
            k_ref[pl.ds(hd, tm, stride=N_HEADS), :] = k[:, hd * V_DIM:(hd + 1) * V_DIM]
            v_ref[pl.ds(hd, tm, stride=N_HEADS), :] = v[:, hd * V_DIM:(hd + 1) * V_DIM]
    xr_ref[...] = col(3)
    xg_ref[...] = col(4)


def _proj_call(x, mod3, ng, win, *, rows_per_mod, mod_base, tm, rope_tabs=None):
    t = x.shape[0]
    rope = rope_tabs is not None
    tiles_per_mod = rows_per_mod // tm
    row_map = lambda i: (i, 0)
    const2 = lambda i: (0, 0)
    in_specs = [pl.BlockSpec((tm, D_MODEL), row_map),
                pl.BlockSpec((1, N_MOD, D_MODEL), lambda i: (mod_base + i // tiles_per_mod, 0, 0)),
                pl.BlockSpec(ng.shape, const2),
                pl.BlockSpec(win.shape, const2, pipeline_mode=pl.Buffered(1))]
    args = [x, mod3, ng, win]
    if rope:
        cos, sin = rope_tabs
        tiles_per_seq = cos.shape[0] // tm
        tab_map = lambda i: (i % tiles_per_seq, 0)
        in_specs += [pl.BlockSpec((tm, LANES), tab_map), pl.BlockSpec((tm, LANES), tab_map)]
        args += [cos, sin]
    half = pl.BlockSpec((tm, D_ATTN), row_map)
    if rope:
        kv_spec, kv_shape = half, jax.ShapeDtypeStruct((t, D_ATTN), BF16)
    else:
        kv_spec = pl.BlockSpec((tm * N_HEADS, V_DIM), row_map)
        kv_shape = jax.ShapeDtypeStruct((t * N_HEADS, V_DIM), F32)
    return pl.pallas_call(
        functools.partial(_proj_kernel, rope=rope),
        grid=(t // tm,),
        in_specs=in_specs,
        out_specs=[half, kv_spec, kv_spec, half, half],
        out_shape=[jax.ShapeDtypeStruct((t, D_ATTN), BF16),
                   kv_shape,
                   kv_shape,
                   jax.ShapeDtypeStruct((t, D_RG), F32),
                   jax.ShapeDtypeStruct((t, D_RG), F32)],
        compiler_params=pltpu.CompilerParams(
            dimension_semantics=("arbitrary",), vmem_limit_bytes=VMEM_LIMIT),
        name="proj",
    )(*args)


def _diff_lambda(dl_ref):
    dl = dl_ref[...]
    return (jnp.exp(jnp.sum(dl[0:1] * dl[1:2], axis=-1, keepdims=True))
            - jnp.exp(jnp.sum(dl[2:3] * dl[3:4], axis=-1, keepdims=True)) + LAM_INIT)


def _stack_maps(q):
    map0 = lax.broadcasted_iota(jnp.int32, (1, V_DIM), 1) < HEAD_DIM
    zero = jnp.zeros_like(q)
    return jnp.concatenate([jnp.where(map0, q, zero), jnp.where(map0, zero, q)], axis=0)


def _finish_head(o2, l, lam, sg, tq):
    o = o2[:tq] * (1.0 / l[:tq]) - o2[tq:] * (lam / l[tq:])
    return _rms(o, sg) * (1.0 - LAM_INIT)


def _attn_small_kernel(q_ref, k_ref, v_ref, dl_ref, sg_ref, o_ref, *, tq):
    lam = _diff_lambda(dl_ref)
    nseq = q_ref.shape[0]
    sk = k_ref.shape[0] // (N_HEADS * nseq)
    ones = jnp.ones((sk, V_DIM), BF16)
    for bi in range(nseq):
        for hd in range(N_HEADS):
            cols = slice(hd * V_DIM, (hd + 1) * V_DIM)
            head_rows = pl.ds(bi * sk * N_HEADS + hd, sk, stride=N_HEADS)
            q2 = _stack_maps(q_ref[bi, :, cols])
            k = k_ref[head_rows, :].astype(BF16)
            s = lax.dot_general(q2, k, (((1,), (1,)), ((), ())), preferred_element_type=F32)
            e = jnp.exp2(s - jnp.max(s, axis=-1, keepdims=True))
            v1 = jnp.concatenate([v_ref[head_rows, :].astype(BF16), ones], axis=1)
            o2 = jnp.dot(e.astype(BF16), v1, preferred_element_type=F32)
            o = _finish_head(o2[:, :V_DIM], o2[:, V_DIM:], lam, sg_ref[...], tq)
            o_ref[bi, :, cols] = o.astype(o_ref.dtype)


def _attn_pipe_kernel(q_ref, k_ref, v_ref, kc_ref, vc_ref, dl_ref, sg_ref, o_ref, *scratch, tq, kt):
    lam = _diff_lambda(dl_ref)
    s_bufs, scratch = scratch[:S_BUFS], scratch[S_BUFS:]
    sc_bufs, scratch = scratch[:S_BUFS], scratch[S_BUFS:]
    p_bufs, pc_bufs = scratch[0:2], scratch[2:4]
    q2_scr, mv_scr, mb_scr, oacc_scr = scratch[4:]
    n_steps = k_ref.shape[1] // kt
    for t in range(N_HEADS + 2):
        ha, hb, hc = t, t - 1, t - 2
        do_a, do_b, do_c = 0 <= ha < N_HEADS, 0 <= hb < N_HEADS, 0 <= hc < N_HEADS
        cols_a = slice(ha * V_DIM, (ha + 1) * V_DIM)
        cols_c = slice(hc * V_DIM, (hc + 1) * V_DIM)
        if do_a:
            q2_scr[...] = _stack_maps(q_ref[0, :, cols_a])
            mv_scr[...] = jnp.full(mv_scr.shape, -jnp.inf, F32)
        if do_c:
            oacc_scr[...] = jnp.zeros(oacc_scr.shape, F32)

        def key_block(k_blk, v_blk, s_a, s_b, p_b, p_c):
            whole = (slice(None), slice(None))
            if do_a:
                s_ref, s_idx = s_a
                kb = k_blk()
                for c0 in range(0, kb.shape[0], MXU_N):
                    s = lax.dot_general(q2_scr[...], kb[c0:c0 + MXU_N], (((1,), (1,)), ((), ())),
                                        preferred_element_type=F32)
                    s_ref[s_idx + (slice(None), slice(c0, c0 + MXU_N))] = s
                    mv_scr[...] = jnp.maximum(mv_scr[...],
                                              jnp.maximum(s[:, :LANES], s[:, LANES:]))
            if do_b:
                s_ref, s_idx = s_b
                p_ref, p_idx = p_b
                n_keys = s_ref.shape[-1]
                for r0 in range(0, 2 * tq, EXP_ROWS):
                    rows = slice(r0, r0 + EXP_ROWS)
                    mb = mb_scr[rows, :]
                    for c0 in range(0, n_keys, LANES):
                        at = (rows, slice(c0, c0 + LANES))
                        p_ref[p_idx + at] = jnp.exp2(s_ref[s_idx + at] - mb).astype(BF16)
            if do_c:
                v = v_blk()
                v1 = jnp.concatenate([v, jnp.ones(v.shape, BF16)], axis=1)
                oacc_scr[...] += jnp.dot(p_c[0][p_c[1] + whole], v1, preferred_element_type=F32)

        def new_keys_step(j, carry):
            keys = pl.ds(pl.multiple_of(j * kt, kt), kt)
            key_block(lambda: k_ref[0, keys, cols_a], lambda: v_ref[0, keys, cols_c],
                      (s_bufs[ha % S_BUFS], (j,)), (s_bufs[hb % S_BUFS], (j,)),
                      (p_bufs[hb % 2], (j,)), (p_bufs[hc % 2], (j,)))
            return carry

        lax.fori_loop(0, n_steps, new_keys_step, 0, unroll=True)
        key_block(lambda: kc_ref[0, :, cols_a], lambda: vc_ref[0, :, cols_c],
                  (sc_bufs[ha % S_BUFS], ()), (sc_bufs[hb % S_BUFS], ()),
                  (pc_bufs[hb % 2], ()), (pc_bufs[hc % 2], ()))
        if do_a:
            m = jnp.max(mv_scr[...], axis=-1, keepdims=True)
            mb_scr[...] = jnp.broadcast_to(m, mb_scr.shape)
        if do_c:
            o = _finish_head(oacc_scr[:, :V_DIM], oacc_scr[:, V_DIM:], lam, sg_ref[...], tq)
            o_ref[0, :, cols_c] = o.astype(o_ref.dtype)


def _attn_call(q, k, v, kc, vc, diff_lambda, subln_g, *, tq):
    b, sq, _ = q.shape
    sk = k.shape[1]
    past = kc.shape[1]
    kt = 1024
    rows = 2 * tq
    qmap = lambda bi, qi: (bi, qi, 0)
    seqmap = lambda bi, qi: (bi, 0, 0)
    const2 = lambda bi, qi: (0, 0)
    kv_spec = pl.BlockSpec((1, sk, D_ATTN), seqmap, pipeline_mode=pl.Buffered(1))
    cache_spec = pl.BlockSpec((1, past, D_ATTN), seqmap)
    scratch = ([pltpu.VMEM((sk // kt, rows, kt), F32)] * S_BUFS
               + [pltpu.VMEM((rows, past), F32)] * S_BUFS
               + [pltpu.VMEM((sk // kt, rows, kt), BF16)] * 2 + [pltpu.VMEM((rows, past), BF16)] * 2
               + [pltpu.VMEM((rows, V_DIM), BF16)] + [pltpu.VMEM((rows, LANES), F32)] * 2
               + [pltpu.VMEM((rows, 2 * V_DIM), F32)])
    return pl.pallas_call(
        functools.partial(_attn_pipe_kernel, tq=tq, kt=kt),
        grid=(b, sq // tq),
        in_specs=[pl.BlockSpec((1, tq, D_ATTN), qmap), kv_spec, kv_spec, cache_spec, cache_spec,
                  pl.BlockSpec(diff_lambda.shape, const2), pl.BlockSpec(subln_g.shape, const2)],
        out_specs=pl.BlockSpec((1, tq, D_ATTN), qmap),
        out_shape=jax.ShapeDtypeStruct((b, sq, D_ATTN), BF16),
        scratch_shapes=scratch,
        compiler_params=pltpu.CompilerParams(
            dimension_semantics=("arbitrary", "arbitrary"), vmem_limit_bytes=VMEM_LIMIT),
        name="attn",
    )(q, k, v, kc, vc, diff_lambda, subln_g)


def _scan8(a, b, reverse):
    sub = lax.broadcasted_iota(jnp.int32, a.shape, 1)
    for d in (1, 2, 4):
        if reverse:
            shift, m = SUBLANES - d, sub < SUBLANES - d
        else:
            shift, m = d, sub >= d
        a_s = pltpu.roll(a, shift, axis=1)
        b_s = pltpu.roll(b, shift, axis=1)
        b = jnp.where(m, a * b_s + b, b)
        a = jnp.where(m, a * a_s, a)
    return a, b


def _scan_tile(load_a, load_b, store_h, carry, reverse):
    order = range(SUBLANES - 1, -1, -1) if reverse else range(SUBLANES)
    hs, ps = {}, {}
    h = p = None
    for s in order:
        a, b = load_a(s), load_b(s)
        h = b if h is None else a * h + b
        p = a if p is None else a * p
        hs[s], ps[s] = h, p
    pp, hh = _scan8(p[None], h[None], reverse)
    after = hh[0] + pp[0] * carry
    sub = lax.broadcasted_iota(jnp.int32, after.shape, 0)
    if reverse:
        before = jnp.where(sub == SUBLANES - 1, carry, pltpu.roll(after, SUBLANES - 1, axis=0))
        carry = after[0:1]
    else:
        before = jnp.where(sub == 0, carry, pltpu.roll(after, 1, axis=0))
        carry = after[SUBLANES - 1:SUBLANES]
    for s in order:
        store_h(s, hs[s] + ps[s] * before)
    return carry


def _tile_rows(t0, q, s):
    return pl.ds(t0 + q * SCAN_ROWS + s, SUBLANES, stride=SUBLANES)


def _scan_block(load_a, load_b, h_ref, slab, t0, tb, carry, reverse):
    tiles = range(tb // SCAN_ROWS)
    for q in (reversed(tiles) if reverse else tiles):
        def store_h(s, h, q=q):
            h_ref[slab, _tile_rows(t0, q, s), :] = h

        carry = _scan_tile(functools.partial(load_a, q), functools.partial(load_b, q),
                           store_h, carry, reverse)
    return carry


def _softplus(x):
    return jnp.maximum(x, 0.0) + jnp.log(1.0 + jnp.exp(-jnp.abs(x)))


def _rglru_seqs(xr_ref, xg_ref, h0_ref, cw_ref, cb_ref, wg_ref, bg_ref, lam_ref,
                y_ref, hl_ref, xpad, ab, bb, hf, *, s, cc, tb, cbase, nseq):
    nblk = s // tb
    nsub = cc // LANES
    ntile = tb // SCAN_ROWS
    lanes = [slice(j * LANES, (j + 1) * LANES) for j in range(nsub)]
    units = [(bi, j, bi * nsub + j) for bi in range(nseq) for j in range(nsub)]

    def gate_ab(half_pre_r, half_pre_i, half_sp, half_xc):
        z = jnp.tanh(half_pre_r) * half_sp + half_sp
        ixc = jnp.tanh(half_pre_i) * half_xc + half_xc
        a = jnp.exp2(z * (-LOG2_E))
        u = (1.0 + a * a) * jnp.tanh(z)
        bx = u * lax.rsqrt(jnp.maximum(u, SQRT_FLOOR)) * ixc
        return a, bx

    for bi, j, slab in units:
        xpad[slab, 0:SUBLANES, :] = jnp.zeros((SUBLANES, LANES), F32)
        xpad[slab, s + SUBLANES:s + 2 * SUBLANES, :] = jnp.zeros((SUBLANES, LANES), F32)
        xpad[slab, SUBLANES:s + SUBLANES, :] = xr_ref[bi, :, lanes[j]]
    params = []
    for j, ls in enumerate(lanes):
        params.append(dict(
            sp_f=(0.5 * RG_C) * _softplus(-lam_ref[0:1, ls]),
            sp_b=(0.5 * RG_C) * _softplus(-lam_ref[1:2, ls]),
            wg=wg_ref[cbase + j],
            bg=bg_ref[:, ls], cw=cw_ref[:, ls], cb=cb_ref[:, ls]))

    def fwd_lanes(unit, t0, carry):
        bi, j, slab = unit
        p = params[j]
        cw, bg = p["cw"], p["bg"]
        xcs = []
        for q in range(ntile):
            xs = {m: xpad[slab, _tile_rows(t0 + SUBLANES, q, m), :]
                  for m in range(-CONV_PAD_LEFT, SUBLANES + CONV_W - 1 - CONV_PAD_LEFT)}
            for sv in range(SUBLANES):
                acc = p["cb"]
                for tap in range(CONV_W):
                    acc = acc + xs[sv + tap - CONV_PAD_LEFT] * cw[tap:tap + 1]
                xcs.append(acc)
        xc = jnp.concatenate(xcs, axis=0)
        pre = jnp.dot(xc.astype(BF16), p["wg"], preferred_element_type=F32)
        hxc = 0.5 * xc
        a_f, b_f = gate_ab(pre[:, 0:128] + bg[0:1], pre[:, 128:256] + bg[1:2], p["sp_f"], hxc)
        a_b, b_b = gate_ab(pre[:, 256:384] + bg[2:3], pre[:, 384:512] + bg[3:4], p["sp_b"], hxc)
        ab[slab, pl.ds(t0, tb), :] = a_b
        bb[slab, pl.ds(t0, tb), :] = b_b

        def vreg_of(x):
            return lambda q, sv: x[q * SCAN_ROWS + sv * SUBLANES:
                                   q * SCAN_ROWS + (sv + 1) * SUBLANES]

        return _scan_block(vreg_of(a_f), vreg_of(b_f), hf, slab, t0, tb, carry, False)

    def bwd_lanes(unit, t0, carry):
        bi, j, slab = unit
        ls = lanes[j]

        def vreg_of(ref):
            return lambda q, sv: ref[slab, pl.ds(t0 + q * SCAN_ROWS + sv * SUBLANES, SUBLANES), :]

        carry = _scan_block(vreg_of(ab), vreg_of(bb), bb, slab, t0, tb, carry, True)
        y = ((hf[slab, pl.ds(t0, tb), :] + bb[slab, pl.ds(t0, tb), :])
             * _gelu_tanh(xg_ref[bi, pl.ds(t0, tb), ls]))
        y_ref[bi, pl.ds(t0, tb), ls] = y.astype(y_ref.dtype)
        return carry

    def fwd_block(t, carries):
        t0 = 0 if nblk == 1 else pl.multiple_of(t * tb, tb)
        return tuple(fwd_lanes(u, t0, c) for u, c in zip(units, carries))

    def bwd_block(tt, carries):
        t0 = 0 if nblk == 1 else pl.multiple_of((nblk - 1 - tt) * tb, tb)
        return tuple(bwd_lanes(u, t0, c) for u, c in zip(units, carries))

    for row, block in ((0, fwd_block), (1, bwd_block)):
        init = tuple(h0_ref[bi, row:row + 1, lanes[j]] for bi, j, _ in units)
        ends = block(0, init) if nblk == 1 else lax.fori_loop(0, nblk, block, init)
        for (bi, j, _), end in zip(units, ends):
            hl_ref[bi, row:row + 1, lanes[j]] = end


def _rglru_kernel(*refs, s, cc, tb):
    _rglru_seqs(*refs, s=s, cc=cc, tb=tb, cbase=pl.program_id(1) * (cc // LANES), nseq=1)


def _ctx_mixers_kernel(q_ref, k_ref, v_ref, dl_ref, sg_ref, *rglru_refs, tq, s, cc, tb, nseq):
    o_ref = rglru_refs[8]
    _attn_small_kernel(q_ref, k_ref, v_ref, dl_ref, sg_ref, o_ref, tq=tq)
    _rglru_seqs(*rglru_refs[:8], *rglru_refs[9:], s=s, cc=cc, tb=tb, cbase=0, nseq=nseq)


def _rglru_call(xr, xg, h0, cw, cb, wg, bg, lam, *, cc):
    b, s, _ = xr.shape
    nch = D_RG // cc
    tb = 256
    seq = pl.BlockSpec((1, s, cc), lambda bi, ci: (bi, 0, ci))
    st = pl.BlockSpec((1, 2, cc), lambda bi, ci: (bi, 0, ci))
    return pl.pallas_call(
        functools.partial(_rglru_kernel, s=s, cc=cc, tb=tb),
        grid=(b, nch),
        in_specs=[seq, seq, st,
                  pl.BlockSpec((CONV_W, cc), lambda bi, ci: (0, ci)),
                  pl.BlockSpec((1, cc), lambda bi, ci: (0, ci)),
                  pl.BlockSpec(wg.shape, lambda bi, ci: (0, 0, 0)),
                  pl.BlockSpec((4, cc), lambda bi, ci: (0, ci)),
                  pl.BlockSpec((2, cc), lambda bi, ci: (0, ci))],
        out_specs=[seq, st],
        out_shape=[jax.ShapeDtypeStruct((b, s, D_RG), BF16),
                   jax.ShapeDtypeStruct((b, 2, D_RG), F32)],
        scratch_shapes=[pltpu.VMEM((cc // LANES, s + 2 * SUBLANES, LANES), F32)]
        + [pltpu.VMEM((cc // LANES, s, LANES), F32)] * 3,
        compiler_params=pltpu.CompilerParams(
            dimension_semantics=("arbitrary", "arbitrary"), vmem_limit_bytes=VMEM_LIMIT),
        name="rglru",
    )(xr, xg, h0, cw, cb, wg, bg, lam)


def _ctx_mixers_call(q, k, v, diff_lambda, subln_g, xr, xg, h0, cw, cb, wg, bg, lam):
    b, s, _ = q.shape
    nseq = SEQS_PER_STEP
    assert b % nseq == 0 and s % SCAN_ROWS == 0
    seq = pl.BlockSpec((nseq, s, D_ATTN), lambda i: (i, 0, 0))
    kv = pl.BlockSpec((nseq * (k.shape[0] // b), V_DIM), lambda i: (i, 0))
    st = pl.BlockSpec((nseq, 2, D_RG), lambda i: (i, 0, 0))
    whole = lambda a: pl.BlockSpec(a.shape, lambda i: (0,) * a.ndim)
    slabs = nseq * (D_RG // LANES)
    return pl.pallas_call(
        functools.partial(_ctx_mixers_kernel, tq=s, s=s, cc=D_RG, tb=s, nseq=nseq),
        grid=(b // nseq,),
        in_specs=[seq, kv, kv, whole(diff_lambda), whole(subln_g), seq, seq, st,
                  whole(cw), whole(cb), whole(wg), whole(bg), whole(lam)],
        out_specs=[seq, seq, st],
        out_shape=[jax.ShapeDtypeStruct((b, s, D_ATTN), BF16),
                   jax.ShapeDtypeStruct((b, s, D_RG), BF16),
                   jax.ShapeDtypeStruct((b, 2, D_RG), F32)],
        scratch_shapes=[pltpu.VMEM((slabs, s + 2 * SUBLANES, LANES), F32)]
        + [pltpu.VMEM((slabs, s, LANES), F32)] * 3,
        compiler_params=pltpu.CompilerParams(
            dimension_semantics=("arbitrary",), vmem_limit_bytes=VMEM_LIMIT),
        name="ctx_mixers",
    )(q, k, v, diff_lambda, subln_g, xr, xg, h0, cw, cb, wg, bg, lam)


def _rope_tables(s):
    pos = np.arange(s)
    row = (pos // GRID_W).astype(np.float32)
    col = (pos % GRID_W).astype(np.float32)
    n_freq = HEAD_DIM // 4
    inv_freq = (ROPE_BASE ** (-np.arange(n_freq, dtype=np.float32) / n_freq)).astype(np.float32)
    p = np.arange(LANES) % HEAD_DIM
    freq = inv_freq[p % n_freq]
    ang = (np.where((p < HEAD_DIM // 2)[None, :], row[:, None], col[:, None])
           * freq[None, :]).astype(np.float32)
    sign = np.where((p % (2 * n_freq)) < n_freq, -1.0, 1.0)
    cos = np.cos(ang.astype(np.float64)).astype(np.float32)
    sin = (np.sin(ang.astype(np.float64)) * sign[None, :]).astype(np.float32)
    return jnp.asarray(cos), jnp.asarray(sin)


def _gate_weights(w_r, w_i):
    def bd(w):
        w = w.reshape(4, 2, RG_BLOCK_W, RG_BLOCK_W)
        z = jnp.zeros_like(w[:, 0])
        top = jnp.concatenate([w[:, 0], z], axis=2)
        bot = jnp.concatenate([z, w[:, 1]], axis=2)
        return jnp.concatenate([top, bot], axis=1)
    w = jnp.concatenate([bd(w_r[0]), bd(w_i[0]), bd(w_r[1]), bd(w_i[1])], axis=2)
    return (0.5 * w).astype(BF16)


def kernel(x_prompt, x_sample, cache_attn_k, cache_attn_v, state_rglru, c, c_ctx, norm_g, w_mod, b_mod, ffn_w_gate, ffn_w_up, ffn_w_down, w_in, w_out, diff_lambda, subln_g, conv_w, conv_b, rg_w_r, rg_b_r, rg_w_i, rg_b_i, rg_lambda, final_g):
    l = 0
    bsz, seq, _ = x_prompt.shape
    dbsz, dseq, _ = x_sample.shape
    past = cache_attn_k.shape[2]

    ng = norm_g[l]
    ffn0_w = [w[l, 0].astype(BF16) for w in (ffn_w_gate, ffn_w_up, ffn_w_down)]
    win = w_in[l].astype(BF16)
    wg = _gate_weights(rg_w_r[l], rg_w_i[l])
    bg = 0.5 * jnp.stack([rg_b_r[l, 0], rg_b_i[l, 0], rg_b_r[l, 1], rg_b_i[l, 1]])
    cw = conv_w[l]
    cb = conv_b[l][None, :]
    lam = rg_lambda[l]
    dl = diff_lambda[l]
    sg = subln_g[l][None, :]
    fg = final_g[None, :]
    tm = 1024

    c8 = jnp.concatenate([c_ctx[None, :], c, jnp.zeros((SUBLANES - 1 - dbsz, D_MODEL), F32)], axis=0)
    mod3 = _mod_call(c8, w_mod[l], b_mod[l][None, :]).reshape(SUBLANES, N_MOD, D_MODEL)

    def mixers(x, *, rows_per_mod, mod_base, nb, s, h0, casts, k_ctx=None, v_ctx=None,
               rope_tabs=None):
        x1, cast = _ffn_call(x, mod3, ng, *ffn0_w, sub=0, rows_per_mod=rows_per_mod,
                             mod_base=mod_base, tm=tm, casts=casts)
        q, k, v, xr, xg = _proj_call(x1, mod3, ng, win, rows_per_mod=rows_per_mod,
                                     mod_base=mod_base, tm=tm, rope_tabs=rope_tabs)
        q3 = q.reshape(nb, s, D_ATTN)
        xr3 = xr.reshape(nb, s, D_RG)
        xg3 = xg.reshape(nb, s, D_RG)
        if k_ctx is None:
            o, rg, h_last = _ctx_mixers_call(q3, k, v, dl, sg, xr3, xg3, h0, cw, cb, wg, bg, lam)
        else:
            o = _attn_call(q3, k.reshape(nb, s, D_ATTN), v.reshape(nb, s, D_ATTN), k_ctx, v_ctx,
                           dl, sg, tq=LATENT_TQ)
            rg, h_last = _rglru_call(xr3, xg3, h0, cw, cb, wg, bg, lam, cc=LATENT_CC)
        return x1, o.reshape(nb * s, D_ATTN), rg.reshape(nb * s, D_RG), k, v, h_last, cast

    x1p, op, rgp, k_new, v_new, h_new, (wgate1, wup1) = mixers(
        x_prompt.reshape(bsz * seq, D_MODEL), rows_per_mod=bsz * seq, mod_base=0, nb=bsz, s=seq,
        h0=jnp.zeros((bsz, 2, D_RG), F32), casts=[(ffn_w_gate, (l, 1)), (ffn_w_up, (l, 1))])
    x1s, os_, rgs, _, _, _, (wdown1, wout) = mixers(
        x_sample.reshape(dbsz * dseq, D_MODEL), rows_per_mod=dseq, mod_base=1, nb=dbsz, s=dseq,
        k_ctx=cache_attn_k[:, l].reshape(dbsz, past, D_ATTN).astype(BF16),
        v_ctx=cache_attn_v[:, l].reshape(dbsz, past, D_ATTN).astype(BF16),
        h0=state_rglru[:, l], rope_tabs=_rope_tables(dseq),
        casts=[(ffn_w_down, (l, 1)), (w_out, (l,))])

    def second_ffn(x1, o, rg, *, rows_per_mod, mod_base):
        y, _ = _ffn_call(x1, mod3, ng, wgate1, wup1, wdown1, sub=2, rows_per_mod=rows_per_mod,
                         mod_base=mod_base, tm=tm, mix=(o, rg, wout), final_g=fg)
        return y

    yp = second_ffn(x1p, op, rgp, rows_per_mod=bsz * seq, mod_base=0)
    ys = second_ffn(x1s, os_, rgs, rows_per_mod=dseq, mod_base=1)

    return (yp.reshape(bsz, seq, D_MODEL),
            ys.reshape(dbsz, dseq, D_MODEL),
            k_new.reshape(bsz, 1, seq, N_HEADS, V_DIM),
            v_new.reshape(bsz, 1, seq, N_HEADS, V_DIM),
            h_new.reshape(bsz, 1, 2, D_RG))
```

```python
import functools
import math

import jax
import jax.numpy as jnp
import numpy as np
from jax import lax
from jax.experimental import pallas as pl
from jax.experimental.pallas import tpu as pltpu

F32 = jnp.float32
BF16 = jnp.bfloat16

D_MODEL = 1024
N_HEADS = 4
HEAD_DIM = 64
V_DIM = 2 * HEAD_DIM
D_ATTN = N_HEADS * V_DIM
D_RG = 512
RG_BLOCK_W = 64
RG_C = 8.0
CONV_W = 4
CONV_PAD_LEFT = 2
D_FF = 2816
N_MOD = 9
GRID_W = 64
ROPE_BASE = 10000.0
EPS = 1e-6
LAM_INIT = 0.8 - 0.6 * math.exp(-0.3 * 0)
LOG2_E = math.log2(math.e)

LANES = 128
SUBLANES = 8
MXU_N = 256
LATENT_TQ = 256
LATENT_CC = 2 * LANES
S_BUFS = 2
SEQS_PER_STEP = 2
SQRT_FLOOR = 1e-30
SCAN_ROWS = SUBLANES * SUBLANES
EXP_ROWS = 64
FF_CHUNK = MXU_N
N_FF_CHUNKS = D_FF // FF_CHUNK
VMEM_LIMIT = 56 * 1024 * 1024


def _sigmoid(x):
    return 1.0 / (1.0 + jnp.exp(-x))


def _gelu_tanh(x):
    return 0.5 * x * (1.0 + jnp.tanh(math.sqrt(2.0 / math.pi) * (x + 0.044715 * (x * x * x))))


def _rms(x, g):
    ms = jnp.mean(x * x, axis=-1, keepdims=True)
    return x * lax.rsqrt(ms + EPS) * g


def _mod_kernel(c_ref, w_ref, b_ref, o_ref):
    c = c_ref[...]
    s = (c * _sigmoid(c)).astype(BF16)
    o_ref[...] = jnp.dot(s, w_ref[...].astype(BF16), preferred_element_type=F32) + b_ref[...]


def _mod_call(c8, w_mod, b_mod):
    n = w_mod.shape[1]
    tn = 1536
    return pl.pallas_call(
        _mod_kernel,
        grid=(n // tn,),
        in_specs=[
            pl.BlockSpec((SUBLANES, D_MODEL), lambda j: (0, 0)),
            pl.BlockSpec((D_MODEL, tn), lambda j: (0, j)),
            pl.BlockSpec((1, tn), lambda j: (0, j)),
        ],
        out_specs=pl.BlockSpec((SUBLANES, tn), lambda j: (0, j)),
        out_shape=jax.ShapeDtypeStruct((SUBLANES, n), F32),
        compiler_params=pltpu.CompilerParams(
            dimension_semantics=("arbitrary",), vmem_limit_bytes=VMEM_LIMIT),
        name="mod",
    )(c8, w_mod, b_mod)


def _ffn_kernel(*refs, sub, fuse_mix, final_norm, n_casts):
    it = iter(refs)
    x_ref = next(it)
    if fuse_mix:
        o_ref_in = next(it)
        rg_ref = next(it)
        wout_ref = next(it)
    mod_ref = next(it)
    ng_ref = next(it)
    wg_ref = next(it)
    wu_ref = next(it)
    wd_ref = next(it)
    fg_ref = next(it) if final_norm else None
    cast_in = [next(it) for _ in range(n_casts)]
    out_ref = next(it)
    for src in cast_in:
        dst = next(it)
        dst[...] = src[...].astype(dst.dtype)
    acc_ref = out_ref
    h_ref = next(it)
    a0_ref = next(it)
    a1_ref = next(it)

    x = x_ref[...]
    if fuse_mix:
        mix = jnp.dot(o_ref_in[...], wout_ref[0:D_ATTN, :], preferred_element_type=F32)
        mix = mix + jnp.dot(rg_ref[...], wout_ref[D_ATTN:, :], preferred_element_type=F32)
        x = x + mod_ref[0, 5:6, :] * mix
    sh = mod_ref[0, 3 * sub:3 * sub + 1, :]
    sc = mod_ref[0, 3 * sub + 1:3 * sub + 2, :]
    gate = mod_ref[0, 3 * sub + 2:3 * sub + 3, :]
    h_ref[...] = (_rms(x, ng_ref[sub:sub + 1, :]) * (1.0 + sc) + sh).astype(BF16)

    def chunk(j):
        return pl.ds(pl.multiple_of(j * FF_CHUNK, FF_CHUNK), FF_CHUNK)

    def gate_up(j):
        h = h_ref[...]
        g = jnp.dot(h, wg_ref[:, chunk(j)], preferred_element_type=F32)
        u = jnp.dot(h, wu_ref[:, chunk(j)], preferred_element_type=F32)
        return (g * _sigmoid(g) * u).astype(BF16)

    def down(j, a):
        return jnp.dot(a, wd_ref[chunk(j), :], preferred_element_type=F32)

    a0_ref[...] = gate_up(0)
    a1_ref[...] = gate_up(1)
    acc_ref[...] = down(0, a0_ref[...])

    def body(i, carry):
        a0_ref[...] = gate_up(2 * i + 2)
        acc_ref[...] += down(2 * i + 1, a1_ref[...])
        a1_ref[...] = gate_up(2 * i + 3)
        acc_ref[...] += down(2 * i + 2, a0_ref[...])
        return carry

    assert N_FF_CHUNKS % 2 == 1 and N_FF_CHUNKS >= 5
    lax.fori_loop(0, (N_FF_CHUNKS - 3) // 2, body, 0, unroll=2)
    a0_ref[...] = gate_up(N_FF_CHUNKS - 1)
    acc_ref[...] += down(N_FF_CHUNKS - 2, a1_ref[...])
    y = x + (0.5 * gate) * (acc_ref[...] + down(N_FF_CHUNKS - 1, a0_ref[...]))
    if final_norm:
        y = _rms(y, fg_ref[...])
    out_ref[...] = y


def _ffn_call(x, mod3, ng, wg, wu, wd, *, sub, rows_per_mod, mod_base, tm,
              mix=None, final_g=None, casts=()):
    t = x.shape[0]
    steps = t // tm
    fuse_mix = mix is not None
    final_norm = final_g is not None
    tiles_per_mod = rows_per_mod // tm

    def row_map(i):
        return (i, 0)

    def mod_map(i):
        return (mod_base + i // tiles_per_mod, 0, 0)

    const2 = lambda i: (0, 0)
    in_specs = [pl.BlockSpec((tm, D_MODEL), row_map)]
    args = [x]
    if fuse_mix:
        o, rg, wout = mix
        in_specs += [pl.BlockSpec((tm, D_ATTN), row_map),
                     pl.BlockSpec((tm, D_RG), row_map),
                     pl.BlockSpec(wout.shape, const2, pipeline_mode=pl.Buffered(1))]
        args += [o, rg, wout]
    in_specs += [pl.BlockSpec((1, N_MOD, D_MODEL), mod_map),
                 pl.BlockSpec(ng.shape, const2),
                 pl.BlockSpec(wg.shape, const2, pipeline_mode=pl.Buffered(1)),
                 pl.BlockSpec(wu.shape, const2, pipeline_mode=pl.Buffered(1)),
                 pl.BlockSpec(wd.shape, const2, pipeline_mode=pl.Buffered(1))]
    args += [mod3, ng, wg, wu, wd]
    if final_norm:
        in_specs.append(pl.BlockSpec((1, D_MODEL), const2))
        args.append(final_g)
    out_specs = [pl.BlockSpec((tm, D_MODEL), row_map)]
    out_shape = [jax.ShapeDtypeStruct((t, D_MODEL), F32)]
    for arr, lead in casts:
        rows, cols = arr.shape[-2:]
        rb = rows // steps
        assert rows % steps == 0 and rb % (2 * SUBLANES) == 0
        in_specs.append(pl.BlockSpec((None,) * len(lead) + (rb, cols),
                                     lambda i, lead=lead: lead + (i, 0)))
        args.append(arr)
        out_specs.append(pl.BlockSpec((rb, cols), row_map))
        out_shape.append(jax.ShapeDtypeStruct((rows, cols), BF16))
    outs = pl.pallas_call(
        functools.partial(_ffn_kernel, sub=sub, fuse_mix=fuse_mix, final_norm=final_norm,
                          n_casts=len(casts)),
        grid=(steps,),
        in_specs=in_specs,
        out_specs=out_specs,
        out_shape=out_shape,
        scratch_shapes=[pltpu.VMEM((tm, D_MODEL), BF16),
                        pltpu.VMEM((tm, FF_CHUNK), BF16), pltpu.VMEM((tm, FF_CHUNK), BF16)],
        compiler_params=pltpu.CompilerParams(
            dimension_semantics=("arbitrary",), vmem_limit_bytes=VMEM_LIMIT),
        name="ffn%d" % sub,
    )(*args)
    return outs[0], list(outs[1:])


def _rope(x, cos, sin_signed, first_half):
    outs = []
    for cblk in range(x.shape[1] // LANES):
        xs = x[:, cblk * LANES:(cblk + 1) * LANES]
        partner = jnp.where(first_half, pltpu.roll(xs, LANES - 16, axis=1),
                            pltpu.roll(xs, 16, axis=1))
        outs.append(xs * cos + partner * sin_signed)
    return jnp.concatenate(outs, axis=1)


def _proj_kernel(*refs, rope):
    it = iter(refs)
    x_ref = next(it)
    mod_ref = next(it)
    ng_ref = next(it)
    win_ref = next(it)
    if rope:
        cos_ref = next(it)
        sin_ref = next(it)
    q_ref, k_ref, v_ref, xr_ref, xg_ref = it

    x = x_ref[...]
    sh = mod_ref[0, 3:4, :]
    sc = mod_ref[0, 4:5, :]
    h = (_rms(x, ng_ref[1:2, :]) * (1.0 + sc) + sh).astype(BF16)

    def col(j):
        return jnp.dot(h, win_ref[:, j * D_ATTN:(j + 1) * D_ATTN], preferred_element_type=F32)

    q = col(0)
    k = col(1)
    if rope:
        cos = cos_ref[...]
        sin = sin_ref[...]
        lane = lax.broadcasted_iota(jnp.int32, (1, LANES), 1)
        first_half = (lane % 32) < 16
        q = _rope(q, cos, sin, first_half)
        k = _rope(k, cos, sin, first_half)
    q_ref[...] = (q * (HEAD_DIM ** -0.5 * LOG2_E)).astype(q_ref.dtype)
    v = col(2)
    if rope:
        k_ref[...] = k.astype(k_ref.dtype)
        v_ref[...] = v.astype(v_ref.dtype)
    else:
        tm = x.shape[0]
        for hd in range(N_HEADS):
            k_ref[pl.ds(hd, tm, stride=N_HEADS), :] = k[:, hd * V_DIM:(hd + 1) * V_DIM]
            v_ref[pl.ds(hd, tm, stride=N_HEADS), :] = v[:, hd * V_DIM:(hd + 1) * V_DIM]
    xr_ref[...] = col(3)
    xg_ref[...] = col(4)


def _proj_call(x, mod3, ng, win, *, rows_per_mod, mod_base, tm, rope_tabs=None):
    t = x.shape[0]
    rope = rope_tabs is not None
    tiles_per_mod = rows_per_mod // tm
    row_map = lambda i: (i, 0)
    const2 = lambda i: (0, 0)
    in_specs = [pl.BlockSpec((tm, D_MODEL), row_map),
                pl.BlockSpec((1, N_MOD, D_MODEL), lambda i: (mod_base + i // tiles_per_mod, 0, 0)),
                pl.BlockSpec(ng.shape, const2),
                pl.BlockSpec(win.shape, const2, pipeline_mode=pl.Buffered(1))]
    args = [x, mod3, ng, win]
    if rope:
        cos, sin = rope_tabs
        tiles_per_seq = cos.shape[0] // tm
        tab_map = lambda i: (i % tiles_per_seq, 0)
        in_specs += [pl.BlockSpec((tm, LANES), tab_map), pl.BlockSpec((tm, LANES), tab_map)]
        args += [cos, sin]
    half = pl.BlockSpec((tm, D_ATTN), row_map)
    if rope:
        kv_spec, kv_shape = half, jax.ShapeDtypeStruct((t, D_ATTN), BF16)
    else:
        kv_spec = pl.BlockSpec((tm * N_HEADS, V_DIM), row_map)
        kv_shape = jax.ShapeDtypeStruct((t * N_HEADS, V_DIM), F32)
    return pl.pallas_call(
        functools.partial(_proj_kernel, rope=rope),
        grid=(t // tm,),
        in_specs=in_specs,
        out_specs=[half, kv_spec, kv_spec, half, half],
        out_shape=[jax.ShapeDtypeStruct((t, D_ATTN), BF16),
                   kv_shape,
                   kv_shape,
                   jax.ShapeDtypeStruct((t, D_RG), F32),
                   jax.ShapeDtypeStruct((t, D_RG), F32)],
        compiler_params=pltpu.CompilerParams(
            dimension_semantics=("arbitrary",), vmem_limit_bytes=VMEM_LIMIT),
        name="proj",
    )(*args)


def _diff_lambda(dl_ref):
    dl = dl_ref[...]
    return (jnp.exp(jnp.sum(dl[0:1] * dl[1:2], axis=-1, keepdims=True))
            - jnp.exp(jnp.sum(dl[2:3] * dl[3:4], axis=-1, keepdims=True)) + LAM_INIT)


def _stack_maps(q):
    map0 = lax.broadcasted_iota(jnp.int32, (1, V_DIM), 1) < HEAD_DIM
    zero = jnp.zeros_like(q)
    return jnp.concatenate([jnp.where(map0, q, zero), jnp.where(map0, zero, q)], axis=0)


def _finish_head(o2, l, lam, sg, tq):
    o = o2[:tq] * (1.0 / l[:tq]) - o2[tq:] * (lam / l[tq:])
    return _rms(o, sg) * (1.0 - LAM_INIT)


def _attn_small_kernel(q_ref, k_ref, v_ref, dl_ref, sg_ref, o_ref, *, tq):
    lam = _diff_lambda(dl_ref)
    nseq = q_ref.shape[0]
    sk = k_ref.shape[0] // (N_HEADS * nseq)
    ones = jnp.ones((sk, V_DIM), BF16)
    for bi in range(nseq):
        for hd in range(N_HEADS):
            cols = slice(hd * V_DIM, (hd + 1) * V_DIM)
            head_rows = pl.ds(bi * sk * N_HEADS + hd, sk, stride=N_HEADS)
            q2 = _stack_maps(q_ref[bi, :, cols])
            k = k_ref[head_rows, :].astype(BF16)
            s = lax.dot_general(q2, k, (((1,), (1,)), ((), ())), preferred_element_type=F32)
            e = jnp.exp2(s - jnp.max(s, axis=-1, keepdims=True))
            v1 = jnp.concatenate([v_ref[head_rows, :].astype(BF16), ones], axis=1)
            o2 = jnp.dot(e.astype(BF16), v1, preferred_element_type=F32)
            o = _finish_head(o2[:, :V_DIM], o2[:, V_DIM:], lam, sg_ref[...], tq)
            o_ref[bi, :, cols] = o.astype(o_ref.dtype)


def _attn_pipe_kernel(q_ref, k_ref, v_ref, kc_ref, vc_ref, dl_ref, sg_ref, o_ref, *scratch, tq, kt):
    lam = _diff_lambda(dl_ref)
    s_bufs, scratch = scratch[:S_BUFS], scratch[S_BUFS:]
    sc_bufs, scratch = scratch[:S_BUFS], scratch[S_BUFS:]
    p_bufs, pc_bufs = scratch[0:2], scratch[2:4]
    q2_scr, mv_scr, mb_scr, oacc_scr = scratch[4:]
    n_steps = k_ref.shape[1] // kt
    for t in range(N_HEADS + 2):
        ha, hb, hc = t, t - 1, t - 2
        do_a, do_b, do_c = 0 <= ha < N_HEADS, 0 <= hb < N_HEADS, 0 <= hc < N_HEADS
        cols_a = slice(ha * V_DIM, (ha + 1) * V_DIM)
        cols_c = slice(hc * V_DIM, (hc + 1) * V_DIM)
        if do_a:
            q2_scr[...] = _stack_maps(q_ref[0, :, cols_a])
            mv_scr[...] = jnp.full(mv_scr.shape, -jnp.inf, F32)
        if do_c:
            oacc_scr[...] = jnp.zeros(oacc_scr.shape, F32)

        def key_block(k_blk, v_blk, s_a, s_b, p_b, p_c):
            whole = (slice(None), slice(None))
            if do_a:
                s_ref, s_idx = s_a
                kb = k_blk()
                for c0 in range(0, kb.shape[0], MXU_N):
                    s = lax.dot_general(q2_scr[...], kb[c0:c0 + MXU_N], (((1,), (1,)), ((), ())),
                                        preferred_element_type=F32)
                    s_ref[s_idx + (slice(None), slice(c0, c0 + MXU_N))] = s
                    mv_scr[...] = jnp.maximum(mv_scr[...],
                                              jnp.maximum(s[:, :LANES], s[:, LANES:]))
            if do_b:
                s_ref, s_idx = s_b
                p_ref, p_idx = p_b
                n_keys = s_ref.shape[-1]
                for r0 in range(0, 2 * tq, EXP_ROWS):
                    rows = slice(r0, r0 + EXP_ROWS)
                    mb = mb_scr[rows, :]
                    for c0 in range(0, n_keys, LANES):
                        at = (rows, slice(c0, c0 + LANES))
                        p_ref[p_idx + at] = jnp.exp2(s_ref[s_idx + at] - mb).astype(BF16)
            if do_c:
                v = v_blk()
                v1 = jnp.concatenate([v, jnp.ones(v.shape, BF16)], axis=1)
                oacc_scr[...] += jnp.dot(p_c[0][p_c[1] + whole], v1, preferred_element_type=F32)

        def new_keys_step(j, carry):
            keys = pl.ds(pl.multiple_of(j * kt, kt), kt)
            key_block(lambda: k_ref[0, keys, cols_a], lambda: v_ref[0, keys, cols_c],
                      (s_bufs[ha % S_BUFS], (j,)), (s_bufs[hb % S_BUFS], (j,)),
                      (p_bufs[hb % 2], (j,)), (p_bufs[hc % 2], (j,)))
            return carry

        lax.fori_loop(0, n_steps, new_keys_step, 0, unroll=True)
        key_block(lambda: kc_ref[0, :, cols_a], lambda: vc_ref[0, :, cols_c],
                  (sc_bufs[ha % S_BUFS], ()), (sc_bufs[hb % S_BUFS], ()),
                  (pc_bufs[hb % 2], ()), (pc_bufs[hc % 2], ()))
        if do_a:
            m = jnp.max(mv_scr[...], axis=-1, keepdims=True)
            mb_scr[...] = jnp.broadcast_to(m, mb_scr.shape)
        if do_c:
            o = _finish_head(oacc_scr[:, :V_DIM], oacc_scr[:, V_DIM:], lam, sg_ref[...], tq)
            o_ref[0, :, cols_c] = o.astype(o_ref.dtype)


def _attn_call(q, k, v, kc, vc, diff_lambda, subln_g, *, tq):
    b, sq, _ = q.shape
    sk = k.shape[1]
    past = kc.shape[1]
    kt = 1024
    rows = 2 * tq
    qmap = lambda bi, qi: (bi, qi, 0)
    seqmap = lambda bi, qi: (bi, 0, 0)
    const2 = lambda bi, qi: (0, 0)
    kv_spec = pl.BlockSpec((1, sk, D_ATTN), seqmap, pipeline_mode=pl.Buffered(1))
    cache_spec = pl.BlockSpec((1, past, D_ATTN), seqmap)
    scratch = ([pltpu.VMEM((sk // kt, rows, kt), F32)] * S_BUFS
               + [pltpu.VMEM((rows, past), F32)] * S_BUFS
               + [pltpu.VMEM((sk // kt, rows, kt), BF16)] * 2 + [pltpu.VMEM((rows, past), BF16)] * 2
               + [pltpu.VMEM((rows, V_DIM), BF16)] + [pltpu.VMEM((rows, LANES), F32)] * 2
               + [pltpu.VMEM((rows, 2 * V_DIM), F32)])
    return pl.pallas_call(
        functools.partial(_attn_pipe_kernel, tq=tq, kt=kt),
        grid=(b, sq // tq),
        in_specs=[pl.BlockSpec((1, tq, D_ATTN), qmap), kv_spec, kv_spec, cache_spec, cache_spec,
                  pl.BlockSpec(diff_lambda.shape, const2), pl.BlockSpec(subln_g.shape, const2)],
        out_specs=pl.BlockSpec((1, tq, D_ATTN), qmap),
        out_shape=jax.ShapeDtypeStruct((b, sq, D_ATTN), BF16),
        scratch_shapes=scratch,
        compiler_params=pltpu.CompilerParams(
            dimension_semantics=("arbitrary", "arbitrary"), vmem_limit_bytes=VMEM_LIMIT),
        name="attn",
    )(q, k, v, kc, vc, diff_lambda, subln_g)


def _scan8(a, b, reverse):
    sub = lax.broadcasted_iota(jnp.int32, a.shape, 1)
    for d in (1, 2, 4):
        if reverse:
            shift, m = SUBLANES - d, sub < SUBLANES - d
        else:
            shift, m = d, sub >= d
        a_s = pltpu.roll(a, shift, axis=1)
        b_s = pltpu.roll(b, shift, axis=1)
        b = jnp.where(m, a * b_s + b, b)
        a = jnp.where(m, a * a_s, a)
    return a, b


def _scan_tile(load_a, load_b, store_h, carry, reverse):
    order = range(SUBLANES - 1, -1, -1) if reverse else range(SUBLANES)
    hs, ps = {}, {}
    h = p = None
    for s in order:
        a, b = load_a(s), load_b(s)
        h = b if h is None else a * h + b
        p = a if p is None else a * p
        hs[s], ps[s] = h, p
    pp, hh = _scan8(p[None], h[None], reverse)
    after = hh[0] + pp[0] * carry
    sub = lax.broadcasted_iota(jnp.int32, after.shape, 0)
    if reverse:
        before = jnp.where(sub == SUBLANES - 1, carry, pltpu.roll(after, SUBLANES - 1, axis=0))
        carry = after[0:1]
    else:
        before = jnp.where(sub == 0, carry, pltpu.roll(after, 1, axis=0))
        carry = after[SUBLANES - 1:SUBLANES]
    for s in order:
        store_h(s, hs[s] + ps[s] * before)
    return carry


def _tile_rows(t0, q, s):
    return pl.ds(t0 + q * SCAN_ROWS + s, SUBLANES, stride=SUBLANES)


def _scan_block(load_a, load_b, h_ref, slab, t0, tb, carry, reverse):
    tiles = range(tb // SCAN_ROWS)
    for q in (reversed(tiles) if reverse else tiles):
        def store_h(s, h, q=q):
            h_ref[slab, _tile_rows(t0, q, s), :] = h

        carry = _scan_tile(functools.partial(load_a, q), functools.partial(load_b, q),
                           store_h, carry, reverse)
    return carry


def _softplus(x):
    return jnp.maximum(x, 0.0) + jnp.log(1.0 + jnp.exp(-jnp.abs(x)))


def _rglru_seqs(xr_ref, xg_ref, h0_ref, cw_ref, cb_ref, wg_ref, bg_ref, lam_ref,
                y_ref, hl_ref, xpad, ab, bb, hf, *, s, cc, tb, cbase, nseq):
    nblk = s // tb
    nsub = cc // LANES
    ntile = tb // SCAN_ROWS
    lanes = [slice(j * LANES, (j + 1) * LANES) for j in range(nsub)]
    units = [(bi, j, bi * nsub + j) for bi in range(nseq) for j in range(nsub)]

    def gate_ab(half_pre_r, half_pre_i, half_sp, half_xc):
        z = jnp.tanh(half_pre_r) * half_sp + half_sp
        ixc = jnp.tanh(half_pre_i) * half_xc + half_xc
        a = jnp.exp2(z * (-LOG2_E))
        u = (1.0 + a * a) * jnp.tanh(z)
        bx = u * lax.rsqrt(jnp.maximum(u, SQRT_FLOOR)) * ixc
        return a, bx

    for bi, j, slab in units:
        xpad[slab, 0:SUBLANES, :] = jnp.zeros((SUBLANES, LANES), F32)
        xpad[slab, s + SUBLANES:s + 2 * SUBLANES, :] = jnp.zeros((SUBLANES, LANES), F32)
        xpad[slab, SUBLANES:s + SUBLANES, :] = xr_ref[bi, :, lanes[j]]
    params = []
    for j, ls in enumerate(lanes):
        params.append(dict(
            sp_f=(0.5 * RG_C) * _softplus(-lam_ref[0:1, ls]),
            sp_b=(0.5 * RG_C) * _softplus(-lam_ref[1:2, ls]),
            wg=wg_ref[cbase + j],
            bg=bg_ref[:, ls], cw=cw_ref[:, ls], cb=cb_ref[:, ls]))

    def fwd_lanes(unit, t0, carry):
        bi, j, slab = unit
        p = params[j]
        cw, bg = p["cw"], p["bg"]
        xcs = []
        for q in range(ntile):
            xs = {m: xpad[slab, _tile_rows(t0 + SUBLANES, q, m), :]
                  for m in range(-CONV_PAD_LEFT, SUBLANES + CONV_W - 1 - CONV_PAD_LEFT)}
            for sv in range(SUBLANES):
                acc = p["cb"]
                for tap in range(CONV_W):
                    acc = acc + xs[sv + tap - CONV_PAD_LEFT] * cw[tap:tap + 1]
                xcs.append(acc)
        xc = jnp.concatenate(xcs, axis=0)
        pre = jnp.dot(xc.astype(BF16), p["wg"], preferred_element_type=F32)
        hxc = 0.5 * xc
        a_f, b_f = gate_ab(pre[:, 0:128] + bg[0:1], pre[:, 128:256] + bg[1:2], p["sp_f"], hxc)
        a_b, b_b = gate_ab(pre[:, 256:384] + bg[2:3], pre[:, 384:512] + bg[3:4], p["sp_b"], hxc)
        ab[slab, pl.ds(t0, tb), :] = a_b
        bb[slab, pl.ds(t0, tb), :] = b_b

        def vreg_of(x):
            return lambda q, sv: x[q * SCAN_ROWS + sv * SUBLANES:
                                   q * SCAN_ROWS + (sv + 1) * SUBLANES]

        return _scan_block(vreg_of(a_f), vreg_of(b_f), hf, slab, t0, tb, carry, False)

    def bwd_lanes(unit, t0, carry):
        bi, j, slab = unit
        ls = lanes[j]

        def vreg_of(ref):
            return lambda q, sv: ref[slab, pl.ds(t0 + q * SCAN_ROWS + sv * SUBLANES, SUBLANES), :]

        carry = _scan_block(vreg_of(ab), vreg_of(bb), bb, slab, t0, tb, carry, True)
        y = ((hf[slab, pl.ds(t0, tb), :] + bb[slab, pl.ds(t0, tb), :])
             * _gelu_tanh(xg_ref[bi, pl.ds(t0, tb), ls]))
        y_ref[bi, pl.ds(t0, tb), ls] = y.astype(y_ref.dtype)
        return carry

    def fwd_block(t, carries):
        t0 = 0 if nblk == 1 else pl.multiple_of(t * tb, tb)
        return tuple(fwd_lanes(u, t0, c) for u, c in zip(units, carries))

    def bwd_block(tt, carries):
        t0 = 0 if nblk == 1 else pl.multiple_of((nblk - 1 - tt) * tb, tb)
        return tuple(bwd_lanes(u, t0, c) for u, c in zip(units, carries))

    for row, block in ((0, fwd_block), (1, bwd_block)):
        init = tuple(h0_ref[bi, row:row + 1, lanes[j]] for bi, j, _ in units)
        ends = block(0, init) if nblk == 1 else lax.fori_loop(0, nblk, block, init)
        for (bi, j, _), end in zip(units, ends):
            hl_ref[bi, row:row + 1, lanes[j]] = end


def _rglru_kernel(*refs, s, cc, tb):
    _rglru_seqs(*refs, s=s, cc=cc, tb=tb, cbase=pl.program_id(1) * (cc // LANES), nseq=1)


def _ctx_mixers_kernel(q_ref, k_ref, v_ref, dl_ref, sg_ref, *rglru_refs, tq, s, cc, tb, nseq):
    o_ref = rglru_refs[8]
    _attn_small_kernel(q_ref, k_ref, v_ref, dl_ref, sg_ref, o_ref, tq=tq)
    _rglru_seqs(*rglru_refs[:8], *rglru_refs[9:], s=s, cc=cc, tb=tb, cbase=0, nseq=nseq)


def _rglru_call(xr, xg, h0, cw, cb, wg, bg, lam, *, cc):
    b, s, _ = xr.shape
    nch = D_RG // cc
    tb = 256
    seq = pl.BlockSpec((1, s, cc), lambda bi, ci: (bi, 0, ci))
    st = pl.BlockSpec((1, 2, cc), lambda bi, ci: (bi, 0, ci))
    return pl.pallas_call(
        functools.partial(_rglru_kernel, s=s, cc=cc, tb=tb),
        grid=(b, nch),
        in_specs=[seq, seq, st,
                  pl.BlockSpec((CONV_W, cc), lambda bi, ci: (0, ci)),
                  pl.BlockSpec((1, cc), lambda bi, ci: (0, ci)),
                  pl.BlockSpec(wg.shape, lambda bi, ci: (0, 0, 0)),
                  pl.BlockSpec((4, cc), lambda bi, ci: (0, ci)),
                  pl.BlockSpec((2, cc), lambda bi, ci: (0, ci))],
        out_specs=[seq, st],
        out_shape=[jax.ShapeDtypeStruct((b, s, D_RG), BF16),
                   jax.ShapeDtypeStruct((b, 2, D_RG), F32)],
        scratch_shapes=[pltpu.VMEM((cc // LANES, s + 2 * SUBLANES, LANES), F32)]
        + [pltpu.VMEM((cc // LANES, s, LANES), F32)] * 3,
        compiler_params=pltpu.CompilerParams(
            dimension_semantics=("arbitrary", "arbitrary"), vmem_limit_bytes=VMEM_LIMIT),
        name="rglru",
    )(xr, xg, h0, cw, cb, wg, bg, lam)


def _ctx_mixers_call(q, k, v, diff_lambda, subln_g, xr, xg, h0, cw, cb, wg, bg, lam):
    b, s, _ = q.shape
    nseq = SEQS_PER_STEP
    assert b % nseq == 0 and s % SCAN_ROWS == 0
    seq = pl.BlockSpec((nseq, s, D_ATTN), lambda i: (i, 0, 0))
    kv = pl.BlockSpec((nseq * (k.shape[0] // b), V_DIM), lambda i: (i, 0))
    st = pl.BlockSpec((nseq, 2, D_RG), lambda i: (i, 0, 0))
    whole = lambda a: pl.BlockSpec(a.shape, lambda i: (0,) * a.ndim)
    slabs = nseq * (D_RG // LANES)
    return pl.pallas_call(
        functools.partial(_ctx_mixers_kernel, tq=s, s=s, cc=D_RG, tb=s, nseq=nseq),
        grid=(b // nseq,),
        in_specs=[seq, kv, kv, whole(diff_lambda), whole(subln_g), seq, seq, st,
                  whole(cw), whole(cb), whole(wg), whole(bg), whole(lam)],
        out_specs=[seq, seq, st],
        out_shape=[jax.ShapeDtypeStruct((b, s, D_ATTN), BF16),
                   jax.ShapeDtypeStruct((b, s, D_RG), BF16),
                   jax.ShapeDtypeStruct((b, 2, D_RG), F32)],
        scratch_shapes=[pltpu.VMEM((slabs, s + 2 * SUBLANES, LANES), F32)]
        + [pltpu.VMEM((slabs, s, LANES), F32)] * 3,
        compiler_params=pltpu.CompilerParams(
            dimension_semantics=("arbitrary",), vmem_limit_bytes=VMEM_LIMIT),
        name="ctx_mixers",
    )(q, k, v, diff_lambda, subln_g, xr, xg, h0, cw, cb, wg, bg, lam)


def _rope_tables(s):
    pos = np.arange(s)
    row = (pos // GRID_W).astype(np.float32)
    col = (pos % GRID_W).astype(np.float32)
    n_freq = HEAD_DIM // 4
    inv_freq = (ROPE_BASE ** (-np.arange(n_freq, dtype=np.float32) / n_freq)).astype(np.float32)
    p = np.arange(LANES) % HEAD_DIM
    freq = inv_freq[p % n_freq]
    ang = (np.where((p < HEAD_DIM // 2)[None, :], row[:, None], col[:, None])
           * freq[None, :]).astype(np.float32)
    sign = np.where((p % (2 * n_freq)) < n_freq, -1.0, 1.0)
    cos = np.cos(ang.astype(np.float64)).astype(np.float32)
    sin = (np.sin(ang.astype(np.float64)) * sign[None, :]).astype(np.float32)
    return jnp.asarray(cos), jnp.asarray(sin)


def _gate_weights(w_r, w_i):
    def bd(w):
        w = w.reshape(4, 2, RG_BLOCK_W, RG_BLOCK_W)
        z = jnp.zeros_like(w[:, 0])
        top = jnp.concatenate([w[:, 0], z], axis=2)
        bot = jnp.concatenate([z, w[:, 1]], axis=2)
        return jnp.concatenate([top, bot], axis=1)
    w = jnp.concatenate([bd(w_r[0]), bd(w_i[0]), bd(w_r[1]), bd(w_i[1])], axis=2)
    return (0.5 * w).astype(BF16)


def kernel(x_prompt, x_sample, cache_attn_k, cache_attn_v, state_rglru, c, c_ctx, norm_g, w_mod, b_mod, ffn_w_gate, ffn_w_up, ffn_w_down, w_in, w_out, diff_lambda, subln_g, conv_w, conv_b, rg_w_r, rg_b_r, rg_w_i, rg_b_i, rg_lambda, final_g):
    l = 0
    bsz, seq, _ = x_prompt.shape
    dbsz, dseq, _ = x_sample.shape
    past = cache_attn_k.shape[2]

    ng = norm_g[l]
    ffn0_w = [w[l, 0].astype(BF16) for w in (ffn_w_gate, ffn_w_up, ffn_w_down)]
    win = w_in[l].astype(BF16)
    wg = _gate_weights(rg_w_r[l], rg_w_i[l])
    bg = 0.5 * jnp.stack([rg_b_r[l, 0], rg_b_i[l, 0], rg_b_r[l, 1], rg_b_i[l, 1]])
    cw = conv_w[l]
    cb = conv_b[l][None, :]
    lam = rg_lambda[l]
    dl = diff_lambda[l]
    sg = subln_g[l][None, :]
    fg = final_g[None, :]
    tm = 1024

    c8 = jnp.concatenate([c_ctx[None, :], c, jnp.zeros((SUBLANES - 1 - dbsz, D_MODEL), F32)], axis=0)
    mod3 = _mod_call(c8, w_mod[l], b_mod[l][None, :]).reshape(SUBLANES, N_MOD, D_MODEL)

    def mixers(x, *, rows_per_mod, mod_base, nb, s, h0, casts, k_ctx=None, v_ctx=None,
               rope_tabs=None):
        x1, cast = _ffn_call(x, mod3, ng, *ffn0_w, sub=0, rows_per_mod=rows_per_mod,
                             mod_base=mod_base, tm=tm, casts=casts)
        q, k, v, xr, xg = _proj_call(x1, mod3, ng, win, rows_per_mod=rows_per_mod,
                                     mod_base=mod_base, tm=tm, rope_tabs=rope_tabs)
        q3 = q.reshape(nb, s, D_ATTN)
        xr3 = xr.reshape(nb, s, D_RG)
        xg3 = xg.reshape(nb, s, D_RG)
        if k_ctx is None:
            o, rg, h_last = _ctx_mixers_call(q3, k, v, dl, sg, xr3, xg3, h0, cw, cb, wg, bg, lam)
        else:
            o = _attn_call(q3, k.reshape(nb, s, D_ATTN), v.reshape(nb, s, D_ATTN), k_ctx, v_ctx,
                           dl, sg, tq=LATENT_TQ)
            rg, h_last = _rglru_call(xr3, xg3, h0, cw, cb, wg, bg, lam, cc=LATENT_CC)
        return x1, o.reshape(nb * s, D_ATTN), rg.reshape(nb * s, D_RG), k, v, h_last, cast

    x1p, op, rgp, k_new, v_new, h_new, (wgate1, wup1) = mixers(
        x_prompt.reshape(bsz * seq, D_MODEL), rows_per_mod=bsz * seq, mod_base=0, nb=bsz, s=seq,
        h0=jnp.zeros((bsz, 2, D_RG), F32), casts=[(ffn_w_gate, (l, 1)), (ffn_w_up, (l, 1))])
    x1s, os_, rgs, _, _, _, (wdown1, wout) = mixers(
        x_sample.reshape(dbsz * dseq, D_MODEL), rows_per_mod=dseq, mod_base=1, nb=dbsz, s=dseq,
        k_ctx=cache_attn_k[:, l].reshape(dbsz, past, D_ATTN).astype(BF16),
        v_ctx=cache_attn_v[:, l].reshape(dbsz, past, D_ATTN).astype(BF16),
        h0=state_rglru[:, l], rope_tabs=_rope_tables(dseq),
        casts=[(ffn_w_down, (l, 1)), (w_out, (l,))])

    def second_ffn(x1, o, rg, *, rows_per_mod, mod_base):
        y, _ = _ffn_call(x1, mod3, ng, wgate1, wup1, wdown1, sub=2, rows_per_mod=rows_per_mod,
                         mod_base=mod_base, tm=tm, mix=(o, rg, wout), final_g=fg)
        return y

    yp = second_ffn(x1p, op, rgp, rows_per_mod=bsz * seq, mod_base=0)
    ys = second_ffn(x1s, os_, rgs, rows_per_mod=dseq, mod_base=1)

    return (yp.reshape(bsz, seq, D_MODEL),
            ys.reshape(dbsz, dseq, D_MODEL),
            k_new.reshape(bsz, 1, seq, N_HEADS, V_DIM),
            v_new.reshape(bsz, 1, seq, N_HEADS, V_DIM),
            h_new.reshape(bsz, 1, 2, D_RG))
```

```python
import functools
import math

import jax
import jax.numpy as jnp
import numpy as np
from jax import lax
from jax.experimental import pallas as pl
from jax.experimental.pallas import tpu as pltpu

F32 = jnp.float32
BF16 = jnp.bfloat16

D_MODEL = 1024
N_HEADS = 4
HEAD_DIM = 64
V_DIM = 2 * HEAD_DIM
D_ATTN = N_HEADS * V_DIM
D_RG = 512
RG_BLOCK_W = 64
RG_C = 8.0
CONV_W = 4
CONV_PAD_LEFT = 2
D_FF = 2816
N_MOD = 9
GRID_W = 64
ROPE_BASE = 10000.0
EPS = 1e-6
LAM_INIT = 0.8 - 0.6 * math.exp(-0.3 * 0)
LOG2_E = math.log2(math.e)

LANES = 128
SUBLANES = 8
MXU_N = 256
LATENT_TQ = 256
LATENT_CC = 2 * LANES
S_BUFS = 2
SEQS_PER_STEP = 4
SQRT_FLOOR = 1e-30
SCAN_ROWS = SUBLANES * SUBLANES
EXP_ROWS = 64
FF_CHUNK = MXU_N
N_FF_CHUNKS = D_FF // FF_CHUNK
VMEM_LIMIT = 56 * 1024 * 1024


def _sigmoid(x):
    return 1.0 / (1.0 + jnp.exp(-x))


def _gelu_tanh(x):
    return 0.5 * x * (1.0 + jnp.tanh(math.sqrt(2.0 / math.pi) * (x + 0.044715 * (x * x * x))))


def _rms(x, g):
    ms = jnp.mean(x * x, axis=-1, keepdims=True)
    return x * lax.rsqrt(ms + EPS) * g


def _mod_kernel(c_ref, w_ref, b_ref, o_ref):
    c = c_ref[...]
    s = (c * _sigmoid(c)).astype(BF16)
    o_ref[...] = jnp.dot(s, w_ref[...].astype(BF16), preferred_element_type=F32) + b_ref[...]


def _mod_call(c8, w_mod, b_mod):
    n = w_mod.shape[1]
    tn = 2304
    return pl.pallas_call(
        _mod_kernel,
        grid=(n // tn,),
        in_specs=[
            pl.BlockSpec((SUBLANES, D_MODEL), lambda j: (0, 0)),
            pl.BlockSpec((D_MODEL, tn), lambda j: (0, j)),
            pl.BlockSpec((1, tn), lambda j: (0, j)),
        ],
        out_specs=pl.BlockSpec((SUBLANES, tn), lambda j: (0, j)),
        out_shape=jax.ShapeDtypeStruct((SUBLANES, n), F32),
        compiler_params=pltpu.CompilerParams(
            dimension_semantics=("arbitrary",), vmem_limit_bytes=VMEM_LIMIT),
        name="mod",
    )(c8, w_mod, b_mod)


def _ffn_kernel(*refs, sub, fuse_mix, final_norm, n_casts):
    it = iter(refs)
    x_ref = next(it)
    if fuse_mix:
        o_ref_in = next(it)
        rg_ref = next(it)
        wout_ref = next(it)
    mod_ref = next(it)
    ng_ref = next(it)
    wg_ref = next(it)
    wu_ref = next(it)
    wd_ref = next(it)
    fg_ref = next(it) if final_norm else None
    cast_in = [next(it) for _ in range(n_casts)]
    out_ref = next(it)
    for src in cast_in:
        dst = next(it)
        dst[...] = src[...].astype(dst.dtype)
    acc_ref = out_ref
    h_ref = next(it)
    a0_ref = next(it)
    a1_ref = next(it)

    x = x_ref[...]
    if fuse_mix:
        mix = jnp.dot(o_ref_in[...], wout_ref[0:D_ATTN, :], preferred_element_type=F32)
        mix = mix + jnp.dot(rg_ref[...], wout_ref[D_ATTN:, :], preferred_element_type=F32)
        x = x + mod_ref[0, 5:6, :] * mix
    sh = mod_ref[0, 3 * sub:3 * sub + 1, :]
    sc = mod_ref[0, 3 * sub + 1:3 * sub + 2, :]
    gate = mod_ref[0, 3 * sub + 2:3 * sub + 3, :]
    h_ref[...] = (_rms(x, ng_ref[sub:sub + 1, :]) * (1.0 + sc) + sh).astype(BF16)

    def chunk(j):
        return pl.ds(pl.multiple_of(j * FF_CHUNK, FF_CHUNK), FF_CHUNK)

    def gate_up(j):
        h = h_ref[...]
        g = jnp.dot(h, wg_ref[:, chunk(j)], preferred_element_type=F32)
        u = jnp.dot(h, wu_ref[:, chunk(j)], preferred_element_type=F32)
        return (g * _sigmoid(g) * u).astype(BF16)

    def down(j, a):
        return jnp.dot(a, wd_ref[chunk(j), :], preferred_element_type=F32)

    a0_ref[...] = gate_up(0)
    a1_ref[...] = gate_up(1)
    acc_ref[...] = down(0, a0_ref[...])

    def body(i, carry):
        a0_ref[...] = gate_up(2 * i + 2)
        acc_ref[...] += down(2 * i + 1, a1_ref[...])
        a1_ref[...] = gate_up(2 * i + 3)
        acc_ref[...] += down(2 * i + 2, a0_ref[...])
        return carry

    assert N_FF_CHUNKS % 2 == 1 and N_FF_CHUNKS >= 5
    lax.fori_loop(0, (N_FF_CHUNKS - 3) // 2, body, 0, unroll=2)
    a0_ref[...] = gate_up(N_FF_CHUNKS - 1)
    acc_ref[...] += down(N_FF_CHUNKS - 2, a1_ref[...])
    y = x + (0.5 * gate) * (acc_ref[...] + down(N_FF_CHUNKS - 1, a0_ref[...]))
    if final_norm:
        y = _rms(y, fg_ref[...])
    out_ref[...] = y


def _ffn_call(x, mod3, ng, wg, wu, wd, *, sub, rows_per_mod, mod_base, tm,
              mix=None, final_g=None, casts=()):
    t = x.shape[0]
    steps = t // tm
    fuse_mix = mix is not None
    final_norm = final_g is not None
    tiles_per_mod = rows_per_mod // tm

    def row_map(i):
        return (i, 0)

    def mod_map(i):
        return (mod_base + i // tiles_per_mod, 0, 0)

    const2 = lambda i: (0, 0)
    in_specs = [pl.BlockSpec((tm, D_MODEL), row_map)]
    args = [x]
    if fuse_mix:
        o, rg, wout = mix
        in_specs += [pl.BlockSpec((tm, D_ATTN), row_map),
                     pl.BlockSpec((tm, D_RG), row_map),
                     pl.BlockSpec(wout.shape, const2, pipeline_mode=pl.Buffered(1))]
        args += [o, rg, wout]
    in_specs += [pl.BlockSpec((1, N_MOD, D_MODEL), mod_map),
                 pl.BlockSpec(ng.shape, const2),
                 pl.BlockSpec(wg.shape, const2, pipeline_mode=pl.Buffered(1)),
                 pl.BlockSpec(wu.shape, const2, pipeline_mode=pl.Buffered(1)),
                 pl.BlockSpec(wd.shape, const2, pipeline_mode=pl.Buffered(1))]
    args += [mod3, ng, wg, wu, wd]
    if final_norm:
        in_specs.append(pl.BlockSpec((1, D_MODEL), const2))
        args.append(final_g)
    out_specs = [pl.BlockSpec((tm, D_MODEL), row_map)]
    out_shape = [jax.ShapeDtypeStruct((t, D_MODEL), F32)]
    for arr, lead in casts:
        rows, cols = arr.shape[-2:]
        rb = rows // steps
        assert rows % steps == 0 and rb % (2 * SUBLANES) == 0
        in_specs.append(pl.BlockSpec((None,) * len(lead) + (rb, cols),
                                     lambda i, lead=lead: lead + (i, 0)))
        args.append(arr)
        out_specs.append(pl.BlockSpec((rb, cols), row_map))
        out_shape.append(jax.ShapeDtypeStruct((rows, cols), BF16))
    outs = pl.pallas_call(
        functools.partial(_ffn_kernel, sub=sub, fuse_mix=fuse_mix, final_norm=final_norm,
                          n_casts=len(casts)),
        grid=(steps,),
        in_specs=in_specs,
        out_specs=out_specs,
        out_shape=out_shape,
        scratch_shapes=[pltpu.VMEM((tm, D_MODEL), BF16),
                        pltpu.VMEM((tm, FF_CHUNK), BF16), pltpu.VMEM((tm, FF_CHUNK), BF16)],
        compiler_params=pltpu.CompilerParams(
            dimension_semantics=("arbitrary",), vmem_limit_bytes=VMEM_LIMIT),
        name="ffn%d" % sub,
    )(*args)
    return outs[0], list(outs[1:])


def _rope(x, cos, sin_signed, first_half):
    outs = []
    for cblk in range(x.shape[1] // LANES):
        xs = x[:, cblk * LANES:(cblk + 1) * LANES]
        partner = jnp.where(first_half, pltpu.roll(xs, LANES - 16, axis=1),
                            pltpu.roll(xs, 16, axis=1))
        outs.append(xs * cos + partner * sin_signed)
    return jnp.concatenate(outs, axis=1)


def _proj_kernel(*refs, rope):
    it = iter(refs)
    x_ref = next(it)
    mod_ref = next(it)
    ng_ref = next(it)
    win_ref = next(it)
    if rope:
        cos_ref = next(it)
        sin_ref = next(it)
    q_ref, k_ref, v_ref, xr_ref, xg_ref = it

    x = x_ref[...]
    sh = mod_ref[0, 3:4, :]
    sc = mod_ref[0, 4:5, :]
    h = (_rms(x, ng_ref[1:2, :]) * (1.0 + sc) + sh).astype(BF16)

    def col(j):
        return jnp.dot(h, win_ref[:, j * D_ATTN:(j + 1) * D_ATTN], preferred_element_type=F32)

    q = col(0)
    k = col(1)
    if rope:
        cos = cos_ref[...]
        sin = sin_ref[...]
        lane = lax.broadcasted_iota(jnp.int32, (1, LANES), 1)
        first_half = (lane % 32) < 16
        q = _rope(q, cos, sin, first_half)
        k = _rope(k, cos, sin, first_half)
    q_ref[...] = (q * (HEAD_DIM ** -0.5 * LOG2_E)).astype(q_ref.dtype)
    v = col(2)
    if rope:
        k_ref[...] = k.astype(k_ref.dtype)
        v_ref[...] = v.astype(v_ref.dtype)
    else:
        tm = x.shape[0]
        for hd in range(N_HEADS):
            k_ref[pl.ds(hd, tm, stride=N_HEADS), :] = k[:, hd * V_DIM:(hd + 1) * V_DIM]
            v_ref[pl.ds(hd, tm, stride=N_HEADS), :] = v[:, hd * V_DIM:(hd + 1) * V_DIM]
    xr_ref[...] = col(3)
    xg_ref[...] = col(4)


def _proj_call(x, mod3, ng, win, *, rows_per_mod, mod_base, tm, rope_tabs=None):
    t = x.shape[0]
    rope = rope_tabs is not None
    tiles_per_mod = rows_per_mod // tm
    row_map = lambda i: (i, 0)
    const2 = lambda i: (0, 0)
    in_specs = [pl.BlockSpec((tm, D_MODEL), row_map),
                pl.BlockSpec((1, N_MOD, D_MODEL), lambda i: (mod_base + i // tiles_per_mod, 0, 0)),
                pl.BlockSpec(ng.shape, const2),
                pl.BlockSpec(win.shape, const2, pipeline_mode=pl.Buffered(1))]
    args = [x, mod3, ng, win]
    if rope:
        cos, sin = rope_tabs
        tiles_per_seq = cos.shape[0] // tm
        tab_map = lambda i: (i % tiles_per_seq, 0)
        in_specs += [pl.BlockSpec((tm, LANES), tab_map), pl.BlockSpec((tm, LANES), tab_map)]
        args += [cos, sin]
    half = pl.BlockSpec((tm, D_ATTN), row_map)
    if rope:
        kv_spec, kv_shape = half, jax.ShapeDtypeStruct((t, D_ATTN), BF16)
    else:
        kv_spec = pl.BlockSpec((tm * N_HEADS, V_DIM), row_map)
        kv_shape = jax.ShapeDtypeStruct((t * N_HEADS, V_DIM), F32)
    return pl.pallas_call(
        functools.partial(_proj_kernel, rope=rope),
        grid=(t // tm,),
        in_specs=in_specs,
        out_specs=[half, kv_spec, kv_spec, half, half],
        out_shape=[jax.ShapeDtypeStruct((t, D_ATTN), BF16),
                   kv_shape,
                   kv_shape,
                   jax.ShapeDtypeStruct((t, D_RG), F32),
                   jax.ShapeDtypeStruct((t, D_RG), F32)],
        compiler_params=pltpu.CompilerParams(
            dimension_semantics=("arbitrary",), vmem_limit_bytes=VMEM_LIMIT),
        name="proj",
    )(*args)


def _diff_lambda(dl_ref):
    dl = dl_ref[...]
    return (jnp.exp(jnp.sum(dl[0:1] * dl[1:2], axis=-1, keepdims=True))
            - jnp.exp(jnp.sum(dl[2:3] * dl[3:4], axis=-1, keepdims=True)) + LAM_INIT)


def _stack_maps(q):
    map0 = lax.broadcasted_iota(jnp.int32, (1, V_DIM), 1) < HEAD_DIM
    zero = jnp.zeros_like(q)
    return jnp.concatenate([jnp.where(map0, q, zero), jnp.where(map0, zero, q)], axis=0)


def _finish_head(o2, l, lam, sg, tq):
    o = o2[:tq] * (1.0 / l[:tq]) - o2[tq:] * (lam / l[tq:])
    return _rms(o, sg) * (1.0 - LAM_INIT)


def _attn_small_kernel(q_ref, k_ref, v_ref, dl_ref, sg_ref, o_ref, *, tq):
    lam = _diff_lambda(dl_ref)
    nseq = q_ref.shape[0]
    sk = k_ref.shape[0] // (N_HEADS * nseq)
    ones = jnp.ones((sk, V_DIM), BF16)
    for bi in range(nseq):
        for hd in range(N_HEADS):
            cols = slice(hd * V_DIM, (hd + 1) * V_DIM)
            head_rows = pl.ds(bi * sk * N_HEADS + hd, sk, stride=N_HEADS)
            q2 = _stack_maps(q_ref[bi, :, cols])
            k = k_ref[head_rows, :].astype(BF16)
            s = lax.dot_general(q2, k, (((1,), (1,)), ((), ())), preferred_element_type=F32)
            e = jnp.exp2(s - jnp.max(s, axis=-1, keepdims=True))
            v1 = jnp.concatenate([v_ref[head_rows, :].astype(BF16), ones], axis=1)
            o2 = jnp.dot(e.astype(BF16), v1, preferred_element_type=F32)
            o = _finish_head(o2[:, :V_DIM], o2[:, V_DIM:], lam, sg_ref[...], tq)
            o_ref[bi, :, cols] = o.astype(o_ref.dtype)


def _attn_pipe_kernel(q_ref, k_ref, v_ref, kc_ref, vc_ref, dl_ref, sg_ref, o_ref, *scratch, tq, kt):
    lam = _diff_lambda(dl_ref)
    s_bufs, scratch = scratch[:S_BUFS], scratch[S_BUFS:]
    sc_bufs, scratch = scratch[:S_BUFS], scratch[S_BUFS:]
    p_bufs, pc_bufs = scratch[0:2], scratch[2:4]
    q2_scr, mv_scr, mb_scr, oacc_scr = scratch[4:]
    n_steps = k_ref.shape[1] // kt
    for t in range(N_HEADS + 2):
        ha, hb, hc = t, t - 1, t - 2
        do_a, do_b, do_c = 0 <= ha < N_HEADS, 0 <= hb < N_HEADS, 0 <= hc < N_HEADS
        cols_a = slice(ha * V_DIM, (ha + 1) * V_DIM)
        cols_c = slice(hc * V_DIM, (hc + 1) * V_DIM)
        if do_a:
            q2_scr[...] = _stack_maps(q_ref[0, :, cols_a])
            mv_scr[...] = jnp.full(mv_scr.shape, -jnp.inf, F32)
        if do_c:
            oacc_scr[...] = jnp.zeros(oacc_scr.shape, F32)

        def key_block(k_blk, v_blk, s_a, s_b, p_b, p_c):
            whole = (slice(None), slice(None))
            if do_a:
                s_ref, s_idx = s_a
                kb = k_blk()
                for c0 in range(0, kb.shape[0], MXU_N):
                    s = lax.dot_general(q2_scr[...], kb[c0:c0 + MXU_N], (((1,), (1,)), ((), ())),
                                        preferred_element_type=F32)
                    s_ref[s_idx + (slice(None), slice(c0, c0 + MXU_N))] = s
                    mv_scr[...] = jnp.maximum(mv_scr[...],
                                              jnp.maximum(s[:, :LANES], s[:, LANES:]))
            if do_b:
                s_ref, s_idx = s_b
                p_ref, p_idx = p_b
                n_keys = s_ref.shape[-1]
                for r0 in range(0, 2 * tq, EXP_ROWS):
                    rows = slice(r0, r0 + EXP_ROWS)
                    mb = mb_scr[rows, :]
                    for c0 in range(0, n_keys, LANES):
                        at = (rows, slice(c0, c0 + LANES))
                        p_ref[p_idx + at] = jnp.exp2(s_ref[s_idx + at] - mb).astype(BF16)
            if do_c:
                v = v_blk()
                v1 = jnp.concatenate([v, jnp.ones(v.shape, BF16)], axis=1)
                oacc_scr[...] += jnp.dot(p_c[0][p_c[1] + whole], v1, preferred_element_type=F32)

        def new_keys_step(j, carry):
            keys = pl.ds(pl.multiple_of(j * kt, kt), kt)
            key_block(lambda: k_ref[0, keys, cols_a], lambda: v_ref[0, keys, cols_c],
                      (s_bufs[ha % S_BUFS], (j,)), (s_bufs[hb % S_BUFS], (j,)),
                      (p_bufs[hb % 2], (j,)), (p_bufs[hc % 2], (j,)))
            return carry

        lax.fori_loop(0, n_steps, new_keys_step, 0, unroll=True)
        key_block(lambda: kc_ref[0, :, cols_a], lambda: vc_ref[0, :, cols_c],
                  (sc_bufs[ha % S_BUFS], ()), (sc_bufs[hb % S_BUFS], ()),
                  (pc_bufs[hb % 2], ()), (pc_bufs[hc % 2], ()))
        if do_a:
            m = jnp.max(mv_scr[...], axis=-1, keepdims=True)
            mb_scr[...] = jnp.broadcast_to(m, mb_scr.shape)
        if do_c:
            o = _finish_head(oacc_scr[:, :V_DIM], oacc_scr[:, V_DIM:], lam, sg_ref[...], tq)
            o_ref[0, :, cols_c] = o.astype(o_ref.dtype)


def _attn_call(q, k, v, kc, vc, diff_lambda, subln_g, *, tq):
    b, sq, _ = q.shape
    sk = k.shape[1]
    past = kc.shape[1]
    kt = 1024
    rows = 2 * tq
    qmap = lambda bi, qi: (bi, qi, 0)
    seqmap = lambda bi, qi: (bi, 0, 0)
    const2 = lambda bi, qi: (0, 0)
    kv_spec = pl.BlockSpec((1, sk, D_ATTN), seqmap, pipeline_mode=pl.Buffered(1))
    cache_spec = pl.BlockSpec((1, past, D_ATTN), seqmap)
    scratch = ([pltpu.VMEM((sk // kt, rows, kt), F32)] * S_BUFS
               + [pltpu.VMEM((rows, past), F32)] * S_BUFS
               + [pltpu.VMEM((sk // kt, rows, kt), BF16)] * 2 + [pltpu.VMEM((rows, past), BF16)] * 2
               + [pltpu.VMEM((rows, V_DIM), BF16)] + [pltpu.VMEM((rows, LANES), F32)] * 2
               + [pltpu.VMEM((rows, 2 * V_DIM), F32)])
    return pl.pallas_call(
        functools.partial(_attn_pipe_kernel, tq=tq, kt=kt),
        grid=(b, sq // tq),
        in_specs=[pl.BlockSpec((1, tq, D_ATTN), qmap), kv_spec, kv_spec, cache_spec, cache_spec,
                  pl.BlockSpec(diff_lambda.shape, const2), pl.BlockSpec(subln_g.shape, const2)],
        out_specs=pl.BlockSpec((1, tq, D_ATTN), qmap),
        out_shape=jax.ShapeDtypeStruct((b, sq, D_ATTN), BF16),
        scratch_shapes=scratch,
        compiler_params=pltpu.CompilerParams(
            dimension_semantics=("arbitrary", "arbitrary"), vmem_limit_bytes=VMEM_LIMIT),
        name="attn",
    )(q, k, v, kc, vc, diff_lambda, subln_g)


def _scan8(a, b, reverse):
    sub = lax.broadcasted_iota(jnp.int32, a.shape, 1)
    for d in (1, 2, 4):
        if reverse:
            shift, m = SUBLANES - d, sub < SUBLANES - d
        else:
            shift, m = d, sub >= d
        a_s = pltpu.roll(a, shift, axis=1)
        b_s = pltpu.roll(b, shift, axis=1)
        b = jnp.where(m, a * b_s + b, b)
        a = jnp.where(m, a * a_s, a)
    return a, b


def _scan_tile(load_a, load_b, store_h, carry, reverse):
    order = range(SUBLANES - 1, -1, -1) if reverse else range(SUBLANES)
    hs, ps = {}, {}
    h = p = None
    for s in order:
        a, b = load_a(s), load_b(s)
        h = b if h is None else a * h + b
        p = a if p is None else a * p
        hs[s], ps[s] = h, p
    pp, hh = _scan8(p[None], h[None], reverse)
    after = hh[0] + pp[0] * carry
    sub = lax.broadcasted_iota(jnp.int32, after.shape, 0)
    if reverse:
        before = jnp.where(sub == SUBLANES - 1, carry, pltpu.roll(after, SUBLANES - 1, axis=0))
        carry = after[0:1]
    else:
        before = jnp.where(sub == 0, carry, pltpu.roll(after, 1, axis=0))
        carry = after[SUBLANES - 1:SUBLANES]
    for s in order:
        store_h(s, hs[s] + ps[s] * before)
    return carry


def _tile_rows(t0, q, s):
    return pl.ds(t0 + q * SCAN_ROWS + s, SUBLANES, stride=SUBLANES)


def _scan_block(load_a, load_b, h_ref, slab, t0, tb, carry, reverse):
    tiles = range(tb // SCAN_ROWS)
    for q in (reversed(tiles) if reverse else tiles):
        def store_h(s, h, q=q):
            h_ref[slab, _tile_rows(t0, q, s), :] = h

        carry = _scan_tile(functools.partial(load_a, q), functools.partial(load_b, q),
                           store_h, carry, reverse)
    return carry


def _softplus(x):
    return jnp.maximum(x, 0.0) + jnp.log(1.0 + jnp.exp(-jnp.abs(x)))


def _rglru_seqs(xr_ref, xg_ref, h0_ref, cw_ref, cb_ref, wg_ref, bg_ref, lam_ref,
                y_ref, hl_ref, xpad, ab, bb, hf, *, s, cc, tb, cbase, nseq):
    nblk = s // tb
    nsub = cc // LANES
    ntile = tb // SCAN_ROWS
    lanes = [slice(j * LANES, (j + 1) * LANES) for j in range(nsub)]
    units = [(bi, j, bi * nsub + j) for bi in range(nseq) for j in range(nsub)]

    def gate_ab(half_pre_r, half_pre_i, half_sp, half_xc):
        z = jnp.tanh(half_pre_r) * half_sp + half_sp
        ixc = jnp.tanh(half_pre_i) * half_xc + half_xc
        a = jnp.exp2(z * (-LOG2_E))
        u = (1.0 + a * a) * jnp.tanh(z)
        bx = u * lax.rsqrt(jnp.maximum(u, SQRT_FLOOR)) * ixc
        return a, bx

    for bi, j, slab in units:
        xpad[slab, 0:SUBLANES, :] = jnp.zeros((SUBLANES, LANES), F32)
        xpad[slab, s + SUBLANES:s + 2 * SUBLANES, :] = jnp.zeros((SUBLANES, LANES), F32)
        xpad[slab, SUBLANES:s + SUBLANES, :] = xr_ref[bi, :, lanes[j]]
    params = []
    for j, ls in enumerate(lanes):
        params.append(dict(
            sp_f=(0.5 * RG_C) * _softplus(-lam_ref[0:1, ls]),
            sp_b=(0.5 * RG_C) * _softplus(-lam_ref[1:2, ls]),
            wg=wg_ref[cbase + j],
            bg=bg_ref[:, ls], cw=cw_ref[:, ls], cb=cb_ref[:, ls]))

    def fwd_lanes(unit, t0, carry):
        bi, j, slab = unit
        p = params[j]
        cw, bg = p["cw"], p["bg"]
        xcs = []
        for q in range(ntile):
            xs = {m: xpad[slab, _tile_rows(t0 + SUBLANES, q, m), :]
                  for m in range(-CONV_PAD_LEFT, SUBLANES + CONV_W - 1 - CONV_PAD_LEFT)}
            for sv in range(SUBLANES):
                acc = p["cb"]
                for tap in range(CONV_W):
                    acc = acc + xs[sv + tap - CONV_PAD_LEFT] * cw[tap:tap + 1]
                xcs.append(acc)
        xc = jnp.concatenate(xcs, axis=0)
        pre = jnp.dot(xc.astype(BF16), p["wg"], preferred_element_type=F32)
        hxc = 0.5 * xc
        a_f, b_f = gate_ab(pre[:, 0:128] + bg[0:1], pre[:, 128:256] + bg[1:2], p["sp_f"], hxc)
        a_b, b_b = gate_ab(pre[:, 256:384] + bg[2:3], pre[:, 384:512] + bg[3:4], p["sp_b"], hxc)
        ab[slab, pl.ds(t0, tb), :] = a_b
        bb[slab, pl.ds(t0, tb), :] = b_b

        def vreg_of(x):
            return lambda q, sv: x[q * SCAN_ROWS + sv * SUBLANES:
                                   q * SCAN_ROWS + (sv + 1) * SUBLANES]

        return _scan_block(vreg_of(a_f), vreg_of(b_f), hf, slab, t0, tb, carry, False)

    def bwd_lanes(unit, t0, carry):
        bi, j, slab = unit
        ls = lanes[j]

        def vreg_of(ref):
            return lambda q, sv: ref[slab, pl.ds(t0 + q * SCAN_ROWS + sv * SUBLANES, SUBLANES), :]

        carry = _scan_block(vreg_of(ab), vreg_of(bb), bb, slab, t0, tb, carry, True)
        y = ((hf[slab, pl.ds(t0, tb), :] + bb[slab, pl.ds(t0, tb), :])
             * _gelu_tanh(xg_ref[bi, pl.ds(t0, tb), ls]))
        y_ref[bi, pl.ds(t0, tb), ls] = y.astype(y_ref.dtype)
        return carry

    def fwd_block(t, carries):
        t0 = 0 if nblk == 1 else pl.multiple_of(t * tb, tb)
        return tuple(fwd_lanes(u, t0, c) for u, c in zip(units, carries))

    def bwd_block(tt, carries):
        t0 = 0 if nblk == 1 else pl.multiple_of((nblk - 1 - tt) * tb, tb)
        return tuple(bwd_lanes(u, t0, c) for u, c in zip(units, carries))

    for row, block in ((0, fwd_block), (1, bwd_block)):
        init = tuple(h0_ref[bi, row:row + 1, lanes[j]] for bi, j, _ in units)
        ends = block(0, init) if nblk == 1 else lax.fori_loop(0, nblk, block, init)
        for (bi, j, _), end in zip(units, ends):
            hl_ref[bi, row:row + 1, lanes[j]] = end


def _rglru_kernel(*refs, s, cc, tb):
    _rglru_seqs(*refs, s=s, cc=cc, tb=tb, cbase=pl.program_id(1) * (cc // LANES), nseq=1)


def _ctx_mixers_kernel(q_ref, k_ref, v_ref, dl_ref, sg_ref, *rglru_refs, tq, s, cc, tb, nseq):
    o_ref = rglru_refs[8]
    _attn_small_kernel(q_ref, k_ref, v_ref, dl_ref, sg_ref, o_ref, tq=tq)
    _rglru_seqs(*rglru_refs[:8], *rglru_refs[9:], s=s, cc=cc, tb=tb, cbase=0, nseq=nseq)


def _rglru_call(xr, xg, h0, cw, cb, wg, bg, lam, *, cc):
    b, s, _ = xr.shape
    nch = D_RG // cc
    tb = 512
    seq = pl.BlockSpec((1, s, cc), lambda bi, ci: (bi, 0, ci))
    st = pl.BlockSpec((1, 2, cc), lambda bi, ci: (bi, 0, ci))
    return pl.pallas_call(
        functools.partial(_rglru_kernel, s=s, cc=cc, tb=tb),
        grid=(b, nch),
        in_specs=[seq, seq, st,
                  pl.BlockSpec((CONV_W, cc), lambda bi, ci: (0, ci)),
                  pl.BlockSpec((1, cc), lambda bi, ci: (0, ci)),
                  pl.BlockSpec(wg.shape, lambda bi, ci: (0, 0, 0)),
                  pl.BlockSpec((4, cc), lambda bi, ci: (0, ci)),
                  pl.BlockSpec((2, cc), lambda bi, ci: (0, ci))],
        out_specs=[seq, st],
        out_shape=[jax.ShapeDtypeStruct((b, s, D_RG), BF16),
                   jax.ShapeDtypeStruct((b, 2, D_RG), F32)],
        scratch_shapes=[pltpu.VMEM((cc // LANES, s + 2 * SUBLANES, LANES), F32)]
        + [pltpu.VMEM((cc // LANES, s, LANES), F32)] * 3,
        compiler_params=pltpu.CompilerParams(
            dimension_semantics=("arbitrary", "arbitrary"), vmem_limit_bytes=VMEM_LIMIT),
        name="rglru",
    )(xr, xg, h0, cw, cb, wg, bg, lam)


def _ctx_mixers_call(q, k, v, diff_lambda, subln_g, xr, xg, h0, cw, cb, wg, bg, lam):
    b, s, _ = q.shape
    nseq = SEQS_PER_STEP
    assert b % nseq == 0 and s % SCAN_ROWS == 0
    seq = pl.BlockSpec((nseq, s, D_ATTN), lambda i: (i, 0, 0))
    kv = pl.BlockSpec((nseq * (k.shape[0] // b), V_DIM), lambda i: (i, 0))
    st = pl.BlockSpec((nseq, 2, D_RG), lambda i: (i, 0, 0))
    whole = lambda a: pl.BlockSpec(a.shape, lambda i: (0,) * a.ndim)
    slabs = nseq * (D_RG // LANES)
    return pl.pallas_call(
        functools.partial(_ctx_mixers_kernel, tq=s, s=s, cc=D_RG, tb=s, nseq=nseq),
        grid=(b // nseq,),
        in_specs=[seq, kv, kv, whole(diff_lambda), whole(subln_g), seq, seq, st,
                  whole(cw), whole(cb), whole(wg), whole(bg), whole(lam)],
        out_specs=[seq, seq, st],
        out_shape=[jax.ShapeDtypeStruct((b, s, D_ATTN), BF16),
                   jax.ShapeDtypeStruct((b, s, D_RG), BF16),
                   jax.ShapeDtypeStruct((b, 2, D_RG), F32)],
        scratch_shapes=[pltpu.VMEM((slabs, s + 2 * SUBLANES, LANES), F32)]
        + [pltpu.VMEM((slabs, s, LANES), F32)] * 3,
        compiler_params=pltpu.CompilerParams(
            dimension_semantics=("arbitrary",), vmem_limit_bytes=VMEM_LIMIT),
        name="ctx_mixers",
    )(q, k, v, diff_lambda, subln_g, xr, xg, h0, cw, cb, wg, bg, lam)


def _rope_tables(s):
    pos = np.arange(s)
    row = (pos // GRID_W).astype(np.float32)
    col = (pos % GRID_W).astype(np.float32)
    n_freq = HEAD_DIM // 4
    inv_freq = (ROPE_BASE ** (-np.arange(n_freq, dtype=np.float32) / n_freq)).astype(np.float32)
    p = np.arange(LANES) % HEAD_DIM
    freq = inv_freq[p % n_freq]
    ang = (np.where((p < HEAD_DIM // 2)[None, :], row[:, None], col[:, None])
           * freq[None, :]).astype(np.float32)
    sign = np.where((p % (2 * n_freq)) < n_freq, -1.0, 1.0)
    cos = np.cos(ang.astype(np.float64)).astype(np.float32)
    sin = (np.sin(ang.astype(np.float64)) * sign[None, :]).astype(np.float32)
    return jnp.asarray(cos), jnp.asarray(sin)


def _gate_weights(w_r, w_i):
    def bd(w):
        w = w.reshape(4, 2, RG_BLOCK_W, RG_BLOCK_W)
        z = jnp.zeros_like(w[:, 0])
        top = jnp.concatenate([w[:, 0], z], axis=2)
        bot = jnp.concatenate([z, w[:, 1]], axis=2)
        return jnp.concatenate([top, bot], axis=1)
    w = jnp.concatenate([bd(w_r[0]), bd(w_i[0]), bd(w_r[1]), bd(w_i[1])], axis=2)
    return (0.5 * w).astype(BF16)


def kernel(x_prompt, x_sample, cache_attn_k, cache_attn_v, state_rglru, c, c_ctx, norm_g, w_mod, b_mod, ffn_w_gate, ffn_w_up, ffn_w_down, w_in, w_out, diff_lambda, subln_g, conv_w, conv_b, rg_w_r, rg_b_r, rg_w_i, rg_b_i, rg_lambda, final_g):
    l = 0
    bsz, seq, _ = x_prompt.shape
    dbsz, dseq, _ = x_sample.shape
    past = cache_attn_k.shape[2]

    ng = norm_g[l]
    ffn0_w = [w[l, 0].astype(BF16) for w in (ffn_w_gate, ffn_w_up, ffn_w_down)]
    win = w_in[l].astype(BF16)
    wg = _gate_weights(rg_w_r[l], rg_w_i[l])
    bg = 0.5 * jnp.stack([rg_b_r[l, 0], rg_b_i[l, 0], rg_b_r[l, 1], rg_b_i[l, 1]])
    cw = conv_w[l]
    cb = conv_b[l][None, :]
    lam = rg_lambda[l]
    dl = diff_lambda[l]
    sg = subln_g[l][None, :]
    fg = final_g[None, :]
    tm = 1024

    c8 = jnp.concatenate([c_ctx[None, :], c, jnp.zeros((SUBLANES - 1 - dbsz, D_MODEL), F32)], axis=0)
    mod3 = _mod_call(c8, w_mod[l], b_mod[l][None, :]).reshape(SUBLANES, N_MOD, D_MODEL)

    def mixers(x, *, rows_per_mod, mod_base, nb, s, h0, casts, k_ctx=None, v_ctx=None,
               rope_tabs=None):
        x1, cast = _ffn_call(x, mod3, ng, *ffn0_w, sub=0, rows_per_mod=rows_per_mod,
                             mod_base=mod_base, tm=tm, casts=casts)
        q, k, v, xr, xg = _proj_call(x1, mod3, ng, win, rows_per_mod=rows_per_mod,
                                     mod_base=mod_base, tm=tm, rope_tabs=rope_tabs)
        q3 = q.reshape(nb, s, D_ATTN)
        xr3 = xr.reshape(nb, s, D_RG)
        xg3 = xg.reshape(nb, s, D_RG)
        if k_ctx is None:
            o, rg, h_last = _ctx_mixers_call(q3, k, v, dl, sg, xr3, xg3, h0, cw, cb, wg, bg, lam)
        else:
            o = _attn_call(q3, k.reshape(nb, s, D_ATTN), v.reshape(nb, s, D_ATTN), k_ctx, v_ctx,
                           dl, sg, tq=LATENT_TQ)
            rg, h_last = _rglru_call(xr3, xg3, h0, cw, cb, wg, bg, lam, cc=LATENT_CC)
        return x1, o.reshape(nb * s, D_ATTN), rg.reshape(nb * s, D_RG), k, v, h_last, cast

    x1p, op, rgp, k_new, v_new, h_new, (wgate1, wup1) = mixers(
        x_prompt.reshape(bsz * seq, D_MODEL), rows_per_mod=bsz * seq, mod_base=0, nb=bsz, s=seq,
        h0=jnp.zeros((bsz, 2, D_RG), F32), casts=[(ffn_w_gate, (l, 1)), (ffn_w_up, (l, 1))])
    x1s, os_, rgs, _, _, _, (wdown1, wout) = mixers(
        x_sample.reshape(dbsz * dseq, D_MODEL), rows_per_mod=dseq, mod_base=1, nb=dbsz, s=dseq,
        k_ctx=cache_attn_k[:, l].reshape(dbsz, past, D_ATTN).astype(BF16),
        v_ctx=cache_attn_v[:, l].reshape(dbsz, past, D_ATTN).astype(BF16),
        h0=state_rglru[:, l], rope_tabs=_rope_tables(dseq),
        casts=[(ffn_w_down, (l, 1)), (w_out, (l,))])

    def second_ffn(x1, o, rg, *, rows_per_mod, mod_base):
        y, _ = _ffn_call(x1, mod3, ng, wgate1, wup1, wdown1, sub=2, rows_per_mod=rows_per_mod,
                         mod_base=mod_base, tm=tm, mix=(o, rg, wout), final_g=fg)
        return y

    yp = second_ffn(x1p, op, rgp, rows_per_mod=bsz * seq, mod_base=0)
    ys = second_ffn(x1s, os_, rgs, rows_per_mod=dseq, mod_base=1)

    return (yp.reshape(bsz, seq, D_MODEL),
            ys.reshape(dbsz, dseq, D_MODEL),
            k_new.reshape(bsz, 1, seq, N_HEADS, V_DIM),
            v_new.reshape(bsz, 1, seq, N_HEADS, V_DIM),
            h_new.reshape(bsz, 1, 2, D_RG))
```

```python
import functools
import math

import jax
import jax.numpy as jnp
import numpy as np
from jax import lax
from jax.experimental import pallas as pl
from jax.experimental.pallas import tpu as pltpu

F32 = jnp.float32
BF16 = jnp.bfloat16

D_MODEL = 1024
N_HEADS = 4
HEAD_DIM = 64
V_DIM = 2 * HEAD_DIM
D_ATTN = N_HEADS * V_DIM
D_RG = 512
RG_BLOCK_W = 64
RG_C = 8.0
CONV_W = 4
CONV_PAD_LEFT = 2
D_FF = 2816
N_MOD = 9
GRID_W = 64
ROPE_BASE = 10000.0
EPS = 1e-6
LAM_INIT = 0.8 - 0.6 * math.exp(-0.3 * 0)
LOG2_E = math.log2(math.e)

LANES = 128
SUBLANES = 8
MXU_N = 256
LATENT_TQ = 256
LATENT_CC = 2 * LANES
S_BUFS = 2
SEQS_PER_STEP = 4
SQRT_FLOOR = 1e-30
SCAN_ROWS = SUBLANES * SUBLANES
EXP_ROWS = 64
FF_CHUNK = MXU_N
N_FF_CHUNKS = D_FF // FF_CHUNK
VMEM_LIMIT = 56 * 1024 * 1024


def _sigmoid(x):
    return 1.0 / (1.0 + jnp.exp(-x))


def _gelu_tanh(x):
    return 0.5 * x * (1.0 + jnp.tanh(math.sqrt(2.0 / math.pi) * (x + 0.044715 * (x * x * x))))


def _rms(x, g):
    ms = jnp.mean(x * x, axis=-1, keepdims=True)
    return x * lax.rsqrt(ms + EPS) * g


def _mod_kernel(c_ref, w_ref, b_ref, o_ref):
    c = c_ref[...]
    s = (c * _sigmoid(c)).astype(BF16)
    o_ref[...] = jnp.dot(s, w_ref[...].astype(BF16), preferred_element_type=F32) + b_ref[...]


def _mod_call(c8, w_mod, b_mod):
    n = w_mod.shape[1]
    tn = 2304
    return pl.pallas_call(
        _mod_kernel,
        grid=(n // tn,),
        in_specs=[
            pl.BlockSpec((SUBLANES, D_MODEL), lambda j: (0, 0)),
            pl.BlockSpec((D_MODEL, tn), lambda j: (0, j)),
            pl.BlockSpec((1, tn), lambda j: (0, j)),
        ],
        out_specs=pl.BlockSpec((SUBLANES, tn), lambda j: (0, j)),
        out_shape=jax.ShapeDtypeStruct((SUBLANES, n), F32),
        compiler_params=pltpu.CompilerParams(
            dimension_semantics=("arbitrary",), vmem_limit_bytes=VMEM_LIMIT),
        name="mod",
    )(c8, w_mod, b_mod)


def _ffn_kernel(*refs, sub, fuse_mix, final_norm, n_casts):
    it = iter(refs)
    x_ref = next(it)
    if fuse_mix:
        o_ref_in = next(it)
        rg_ref = next(it)
        wout_ref = next(it)
    mod_ref = next(it)
    ng_ref = next(it)
    wg_hbm = next(it)
    wu_hbm = next(it)
    wd_hbm = next(it)
    fg_ref = next(it) if final_norm else None
    cast_in = [next(it) for _ in range(n_casts)]
    out_ref = next(it)
    for src in cast_in:
        dst = next(it)
        dst[...] = src[...].astype(dst.dtype)
    acc_ref = out_ref
    h_ref = next(it)
    a0_ref = next(it)
    a1_ref = next(it)
    wg_ref = next(it)
    wu_ref = next(it)
    wd_ref = next(it)
    w_sem = next(it)

    head = 2 * FF_CHUNK

    def weight_copies(part):
        cols = slice(0, head) if part == 0 else slice(head, D_FF)
        return [pltpu.make_async_copy(wg_hbm.at[:, cols], wg_ref.at[:, cols], w_sem.at[part, 0]),
                pltpu.make_async_copy(wu_hbm.at[:, cols], wu_ref.at[:, cols], w_sem.at[part, 1]),
                pltpu.make_async_copy(wd_hbm.at[cols, :], wd_ref.at[cols, :], w_sem.at[part, 2])]

    first_step = pl.program_id(0) == 0

    @pl.when(first_step)
    def _():
        for part in (0, 1):
            for cp in weight_copies(part):
                cp.start()
        for cp in weight_copies(0):
            cp.wait()

    x = x_ref[...]
    if fuse_mix:
        mix = jnp.dot(o_ref_in[...], wout_ref[0:D_ATTN, :], preferred_element_type=F32)
        mix = mix + jnp.dot(rg_ref[...], wout_ref[D_ATTN:, :], preferred_element_type=F32)
        x = x + mod_ref[0, 5:6, :] * mix
    sh = mod_ref[0, 3 * sub:3 * sub + 1, :]
    sc = mod_ref[0, 3 * sub + 1:3 * sub + 2, :]
    gate = mod_ref[0, 3 * sub + 2:3 * sub + 3, :]
    h_ref[...] = (_rms(x, ng_ref[sub:sub + 1, :]) * (1.0 + sc) + sh).astype(BF16)

    def chunk(j):
        return pl.ds(pl.multiple_of(j * FF_CHUNK, FF_CHUNK), FF_CHUNK)

    def gate_up(j):
        h = h_ref[...]
        g = jnp.dot(h, wg_ref[:, chunk(j)], preferred_element_type=F32)
        u = jnp.dot(h, wu_ref[:, chunk(j)], preferred_element_type=F32)
        return (g * _sigmoid(g) * u).astype(BF16)

    def down(j, a):
        return jnp.dot(a, wd_ref[chunk(j), :], preferred_element_type=F32)

    a0_ref[...] = gate_up(0)
    a1_ref[...] = gate_up(1)
    acc_ref[...] = down(0, a0_ref[...])

    @pl.when(first_step)
    def _():
        for cp in weight_copies(1):
            cp.wait()

    def body(i, carry):
        a0_ref[...] = gate_up(2 * i + 2)
        acc_ref[...] += down(2 * i + 1, a1_ref[...])
        a1_ref[...] = gate_up(2 * i + 3)
        acc_ref[...] += down(2 * i + 2, a0_ref[...])
        return carry

    assert N_FF_CHUNKS % 2 == 1 and N_FF_CHUNKS >= 5
    lax.fori_loop(0, (N_FF_CHUNKS - 3) // 2, body, 0, unroll=2)
    a0_ref[...] = gate_up(N_FF_CHUNKS - 1)
    acc_ref[...] += down(N_FF_CHUNKS - 2, a1_ref[...])
    y = x + (0.5 * gate) * (acc_ref[...] + down(N_FF_CHUNKS - 1, a0_ref[...]))
    if final_norm:
        y = _rms(y, fg_ref[...])
    out_ref[...] = y


def _ffn_call(x, mod3, ng, wg, wu, wd, *, sub, rows_per_mod, mod_base, tm,
              mix=None, final_g=None, casts=()):
    t = x.shape[0]
    steps = t // tm
    fuse_mix = mix is not None
    final_norm = final_g is not None
    tiles_per_mod = rows_per_mod // tm

    def row_map(i):
        return (i, 0)

    def mod_map(i):
        return (mod_base + i // tiles_per_mod, 0, 0)

    const2 = lambda i: (0, 0)
    in_specs = [pl.BlockSpec((tm, D_MODEL), row_map)]
    args = [x]
    if fuse_mix:
        o, rg, wout = mix
        in_specs += [pl.BlockSpec((tm, D_ATTN), row_map),
                     pl.BlockSpec((tm, D_RG), row_map),
                     pl.BlockSpec(wout.shape, const2, pipeline_mode=pl.Buffered(1))]
        args += [o, rg, wout]
    in_specs += [pl.BlockSpec((1, N_MOD, D_MODEL), mod_map),
                 pl.BlockSpec(ng.shape, const2),
                 pl.BlockSpec(memory_space=pl.ANY),
                 pl.BlockSpec(memory_space=pl.ANY),
                 pl.BlockSpec(memory_space=pl.ANY)]
    args += [mod3, ng, wg, wu, wd]
    if final_norm:
        in_specs.append(pl.BlockSpec((1, D_MODEL), const2))
        args.append(final_g)
    out_specs = [pl.BlockSpec((tm, D_MODEL), row_map)]
    out_shape = [jax.ShapeDtypeStruct((t, D_MODEL), F32)]
    for arr, lead in casts:
        rows, cols = arr.shape[-2:]
        rb = rows // steps
        assert rows % steps == 0 and rb % (2 * SUBLANES) == 0
        in_specs.append(pl.BlockSpec((None,) * len(lead) + (rb, cols),
                                     lambda i, lead=lead: lead + (i, 0)))
        args.append(arr)
        out_specs.append(pl.BlockSpec((rb, cols), row_map))
        out_shape.append(jax.ShapeDtypeStruct((rows, cols), BF16))
    outs = pl.pallas_call(
        functools.partial(_ffn_kernel, sub=sub, fuse_mix=fuse_mix, final_norm=final_norm,
                          n_casts=len(casts)),
        grid=(steps,),
        in_specs=in_specs,
        out_specs=out_specs,
        out_shape=out_shape,
        scratch_shapes=[pltpu.VMEM((tm, D_MODEL), BF16),
                        pltpu.VMEM((tm, FF_CHUNK), BF16), pltpu.VMEM((tm, FF_CHUNK), BF16),
                        pltpu.VMEM(wg.shape, BF16), pltpu.VMEM(wu.shape, BF16),
                        pltpu.VMEM(wd.shape, BF16), pltpu.SemaphoreType.DMA((2, 3))],
        compiler_params=pltpu.CompilerParams(
            dimension_semantics=("arbitrary",), vmem_limit_bytes=VMEM_LIMIT),
        name="ffn%d" % sub,
    )(*args)
    return outs[0], list(outs[1:])


def _rope(x, cos, sin_signed, first_half):
    outs = []
    for cblk in range(x.shape[1] // LANES):
        xs = x[:, cblk * LANES:(cblk + 1) * LANES]
        partner = jnp.where(first_half, pltpu.roll(xs, LANES - 16, axis=1),
                            pltpu.roll(xs, 16, axis=1))
        outs.append(xs * cos + partner * sin_signed)
    return jnp.concatenate(outs, axis=1)


def _proj_kernel(*refs, rope):
    it = iter(refs)
    x_ref = next(it)
    mod_ref = next(it)
    ng_ref = next(it)
    win_ref = next(it)
    if rope:
        cos_ref = next(it)
        sin_ref = next(it)
    q_ref, k_ref, v_ref, xr_ref, xg_ref = it

    x = x_ref[...]
    sh = mod_ref[0, 3:4, :]
    sc = mod_ref[0, 4:5, :]
    h = (_rms(x, ng_ref[1:2, :]) * (1.0 + sc) + sh).astype(BF16)

    def col(j):
        return jnp.dot(h, win_ref[:, j * D_ATTN:(j + 1) * D_ATTN], preferred_element_type=F32)

    q = col(0)
    k = col(1)
    if rope:
        cos = cos_ref[...]
        sin = sin_ref[...]
        lane = lax.broadcasted_iota(jnp.int32, (1, LANES), 1)
        first_half = (lane % 32) < 16
        q = _rope(q, cos, sin, first_half)
        k = _rope(k, cos, sin, first_half)
    q_ref[...] = (q * (HEAD_DIM ** -0.5 * LOG2_E)).astype(q_ref.dtype)
    v = col(2)
    if rope:
        k_ref[...] = k.astype(k_ref.dtype)
        v_ref[...] = v.astype(v_ref.dtype)
    else:
        tm = x.shape[0]
        for hd in range(N_HEADS):
            k_ref[pl.ds(hd, tm, stride=N_HEADS), :] = k[:, hd * V_DIM:(hd + 1) * V_DIM]
            v_ref[pl.ds(hd, tm, stride=N_HEADS), :] = v[:, hd * V_DIM:(hd + 1) * V_DIM]
    xr_ref[...] = col(3)
    xg_ref[...] = col(4)


def _proj_call(x, mod3, ng, win, *, rows_per_mod, mod_base, tm, rope_tabs=None):
    t = x.shape[0]
    rope = rope_tabs is not None
    tiles_per_mod = rows_per_mod // tm
    row_map = lambda i: (i, 0)
    const2 = lambda i: (0, 0)
    in_specs = [pl.BlockSpec((tm, D_MODEL), row_map),
                pl.BlockSpec((1, N_MOD, D_MODEL), lambda i: (mod_base + i // tiles_per_mod, 0, 0)),
                pl.BlockSpec(ng.shape, const2),
                pl.BlockSpec(win.shape, const2, pipeline_mode=pl.Buffered(1))]
    args = [x, mod3, ng, win]
    if rope:
        cos, sin = rope_tabs
        tiles_per_seq = cos.shape[0] // tm
        tab_map = lambda i: (i % tiles_per_seq, 0)
        in_specs += [pl.BlockSpec((tm, LANES), tab_map), pl.BlockSpec((tm, LANES), tab_map)]
        args += [cos, sin]
    half = pl.BlockSpec((tm, D_ATTN), row_map)
    if rope:
        kv_spec, kv_shape = half, jax.ShapeDtypeStruct((t, D_ATTN), BF16)
    else:
        kv_spec = pl.BlockSpec((tm * N_HEADS, V_DIM), row_map)
        kv_shape = jax.ShapeDtypeStruct((t * N_HEADS, V_DIM), F32)
    return pl.pallas_call(
        functools.partial(_proj_kernel, rope=rope),
        grid=(t // tm,),
        in_specs=in_specs,
        out_specs=[half, kv_spec, kv_spec, half, half],
        out_shape=[jax.ShapeDtypeStruct((t, D_ATTN), BF16),
                   kv_shape,
                   kv_shape,
                   jax.ShapeDtypeStruct((t, D_RG), F32),
                   jax.ShapeDtypeStruct((t, D_RG), F32)],
        compiler_params=pltpu.CompilerParams(
            dimension_semantics=("arbitrary",), vmem_limit_bytes=VMEM_LIMIT),
        name="proj",
    )(*args)


def _diff_lambda(dl_ref):
    dl = dl_ref[...]
    return (jnp.exp(jnp.sum(dl[0:1] * dl[1:2], axis=-1, keepdims=True))
            - jnp.exp(jnp.sum(dl[2:3] * dl[3:4], axis=-1, keepdims=True)) + LAM_INIT)


def _stack_maps(q):
    map0 = lax.broadcasted_iota(jnp.int32, (1, V_DIM), 1) < HEAD_DIM
    zero = jnp.zeros_like(q)
    return jnp.concatenate([jnp.where(map0, q, zero), jnp.where(map0, zero, q)], axis=0)


def _finish_head(o2, l, lam, sg, tq):
    o = o2[:tq] * (1.0 / l[:tq]) - o2[tq:] * (lam / l[tq:])
    return _rms(o, sg) * (1.0 - LAM_INIT)


def _attn_small_kernel(q_ref, k_ref, v_ref, dl_ref, sg_ref, o_ref, *, tq):
    lam = _diff_lambda(dl_ref)
    nseq = q_ref.shape[0]
    sk = k_ref.shape[0] // (N_HEADS * nseq)
    ones = jnp.ones((sk, V_DIM), BF16)
    for bi in range(nseq):
        for hd in range(N_HEADS):
            cols = slice(hd * V_DIM, (hd + 1) * V_DIM)
            head_rows = pl.ds(bi * sk * N_HEADS + hd, sk, stride=N_HEADS)
            q2 = _stack_maps(q_ref[bi, :, cols])
            k = k_ref[head_rows, :].astype(BF16)
            s = lax.dot_general(q2, k, (((1,), (1,)), ((), ())), preferred_element_type=F32)
            e = jnp.exp2(s - jnp.max(s, axis=-1, keepdims=True))
            v1 = jnp.concatenate([v_ref[head_rows, :].astype(BF16), ones], axis=1)
            o2 = jnp.dot(e.astype(BF16), v1, preferred_element_type=F32)
            o = _finish_head(o2[:, :V_DIM], o2[:, V_DIM:], lam, sg_ref[...], tq)
            o_ref[bi, :, cols] = o.astype(o_ref.dtype)


def _attn_pipe_kernel(q_ref, k_ref, v_ref, kc_ref, vc_ref, dl_ref, sg_ref, o_ref, *scratch, tq, kt):
    lam = _diff_lambda(dl_ref)
    s_bufs, scratch = scratch[:S_BUFS], scratch[S_BUFS:]
    sc_bufs, scratch = scratch[:S_BUFS], scratch[S_BUFS:]
    p_bufs, pc_bufs = scratch[0:2], scratch[2:4]
    q2_scr, mv_scr, mb_scr, oacc_scr = scratch[4:]
    n_steps = k_ref.shape[1] // kt
    for t in range(N_HEADS + 2):
        ha, hb, hc = t, t - 1, t - 2
        do_a, do_b, do_c = 0 <= ha < N_HEADS, 0 <= hb < N_HEADS, 0 <= hc < N_HEADS
        cols_a = slice(ha * V_DIM, (ha + 1) * V_DIM)
        cols_c = slice(hc * V_DIM, (hc + 1) * V_DIM)
        if do_a:
            q2_scr[...] = _stack_maps(q_ref[0, :, cols_a])
            mv_scr[...] = jnp.full(mv_scr.shape, -jnp.inf, F32)
        if do_c:
            oacc_scr[...] = jnp.zeros(oacc_scr.shape, F32)

        def key_block(k_blk, v_blk, s_a, s_b, p_b, p_c):
            whole = (slice(None), slice(None))
            if do_a:
                s_ref, s_idx = s_a
                kb = k_blk()
                for c0 in range(0, kb.shape[0], MXU_N):
                    s = lax.dot_general(q2_scr[...], kb[c0:c0 + MXU_N], (((1,), (1,)), ((), ())),
                                        preferred_element_type=F32)
                    s_ref[s_idx + (slice(None), slice(c0, c0 + MXU_N))] = s
                    mv_scr[...] = jnp.maximum(mv_scr[...],
                                              jnp.maximum(s[:, :LANES], s[:, LANES:]))
            if do_b:
                s_ref, s_idx = s_b
                p_ref, p_idx = p_b
                n_keys = s_ref.shape[-1]
                for r0 in range(0, 2 * tq, EXP_ROWS):
                    rows = slice(r0, r0 + EXP_ROWS)
                    mb = mb_scr[rows, :]
                    for c0 in range(0, n_keys, LANES):
                        at = (rows, slice(c0, c0 + LANES))
                        p_ref[p_idx + at] = jnp.exp2(s_ref[s_idx + at] - mb).astype(BF16)
            if do_c:
                v = v_blk()
                v1 = jnp.concatenate([v, jnp.ones(v.shape, BF16)], axis=1)
                oacc_scr[...] += jnp.dot(p_c[0][p_c[1] + whole], v1, preferred_element_type=F32)

        def new_keys_step(j, carry):
            keys = pl.ds(pl.multiple_of(j * kt, kt), kt)
            key_block(lambda: k_ref[0, keys, cols_a], lambda: v_ref[0, keys, cols_c],
                      (s_bufs[ha % S_BUFS], (j,)), (s_bufs[hb % S_BUFS], (j,)),
                      (p_bufs[hb % 2], (j,)), (p_bufs[hc % 2], (j,)))
            return carry

        lax.fori_loop(0, n_steps, new_keys_step, 0, unroll=True)
        key_block(lambda: kc_ref[0, :, cols_a], lambda: vc_ref[0, :, cols_c],
                  (sc_bufs[ha % S_BUFS], ()), (sc_bufs[hb % S_BUFS], ()),
                  (pc_bufs[hb % 2], ()), (pc_bufs[hc % 2], ()))
        if do_a:
            m = jnp.max(mv_scr[...], axis=-1, keepdims=True)
            mb_scr[...] = jnp.broadcast_to(m, mb_scr.shape)
        if do_c:
            o = _finish_head(oacc_scr[:, :V_DIM], oacc_scr[:, V_DIM:], lam, sg_ref[...], tq)
            o_ref[0, :, cols_c] = o.astype(o_ref.dtype)


def _attn_call(q, k, v, kc, vc, diff_lambda, subln_g, *, tq):
    b, sq, _ = q.shape
    sk = k.shape[1]
    past = kc.shape[1]
    kt = 1024
    rows = 2 * tq
    qmap = lambda bi, qi: (bi, qi, 0)
    seqmap = lambda bi, qi: (bi, 0, 0)
    const2 = lambda bi, qi: (0, 0)
    kv_spec = pl.BlockSpec((1, sk, D_ATTN), seqmap, pipeline_mode=pl.Buffered(1))
    cache_spec = pl.BlockSpec((1, past, D_ATTN), seqmap)
    scratch = ([pltpu.VMEM((sk // kt, rows, kt), F32)] * S_BUFS
               + [pltpu.VMEM((rows, past), F32)] * S_BUFS
               + [pltpu.VMEM((sk // kt, rows, kt), BF16)] * 2 + [pltpu.VMEM((rows, past), BF16)] * 2
               + [pltpu.VMEM((rows, V_DIM), BF16)] + [pltpu.VMEM((rows, LANES), F32)] * 2
               + [pltpu.VMEM((rows, 2 * V_DIM), F32)])
    return pl.pallas_call(
        functools.partial(_attn_pipe_kernel, tq=tq, kt=kt),
        grid=(b, sq // tq),
        in_specs=[pl.BlockSpec((1, tq, D_ATTN), qmap), kv_spec, kv_spec, cache_spec, cache_spec,
                  pl.BlockSpec(diff_lambda.shape, const2), pl.BlockSpec(subln_g.shape, const2)],
        out_specs=pl.BlockSpec((1, tq, D_ATTN), qmap),
        out_shape=jax.ShapeDtypeStruct((b, sq, D_ATTN), BF16),
        scratch_shapes=scratch,
        compiler_params=pltpu.CompilerParams(
            dimension_semantics=("arbitrary", "arbitrary"), vmem_limit_bytes=VMEM_LIMIT),
        name="attn",
    )(q, k, v, kc, vc, diff_lambda, subln_g)


def _scan8(a, b, reverse):
    sub = lax.broadcasted_iota(jnp.int32, a.shape, 1)
    for d in (1, 2, 4):
        if reverse:
            shift, m = SUBLANES - d, sub < SUBLANES - d
        else:
            shift, m = d, sub >= d
        a_s = pltpu.roll(a, shift, axis=1)
        b_s = pltpu.roll(b, shift, axis=1)
        b = jnp.where(m, a * b_s + b, b)
        a = jnp.where(m, a * a_s, a)
    return a, b


def _scan_tile(load_a, load_b, store_h, carry, reverse):
    order = range(SUBLANES - 1, -1, -1) if reverse else range(SUBLANES)
    hs, ps = {}, {}
    h = p = None
    for s in order:
        a, b = load_a(s), load_b(s)
        h = b if h is None else a * h + b
        p = a if p is None else a * p
        hs[s], ps[s] = h, p
    pp, hh = _scan8(p[None], h[None], reverse)
    after = hh[0] + pp[0] * carry
    sub = lax.broadcasted_iota(jnp.int32, after.shape, 0)
    if reverse:
        before = jnp.where(sub == SUBLANES - 1, carry, pltpu.roll(after, SUBLANES - 1, axis=0))
        carry = after[0:1]
    else:
        before = jnp.where(sub == 0, carry, pltpu.roll(after, 1, axis=0))
        carry = after[SUBLANES - 1:SUBLANES]
    for s in order:
        store_h(s, hs[s] + ps[s] * before)
    return carry


def _tile_rows(t0, q, s):
    return pl.ds(t0 + q * SCAN_ROWS + s, SUBLANES, stride=SUBLANES)


def _scan_block(load_a, load_b, h_ref, slab, t0, tb, carry, reverse):
    tiles = range(tb // SCAN_ROWS)
    for q in (reversed(tiles) if reverse else tiles):
        def store_h(s, h, q=q):
            h_ref[slab, _tile_rows(t0, q, s), :] = h

        carry = _scan_tile(functools.partial(load_a, q), functools.partial(load_b, q),
                           store_h, carry, reverse)
    return carry


def _softplus(x):
    return jnp.maximum(x, 0.0) + jnp.log(1.0 + jnp.exp(-jnp.abs(x)))


def _rglru_seqs(xr_ref, xg_ref, h0_ref, cw_ref, cb_ref, wg_ref, bg_ref, lam_ref,
                y_ref, hl_ref, xpad, ab, bb, hf, *, s, cc, tb, cbase, nseq):
    nblk = s // tb
    nsub = cc // LANES
    ntile = tb // SCAN_ROWS
    lanes = [slice(j * LANES, (j + 1) * LANES) for j in range(nsub)]
    units = [(bi, j, bi * nsub + j) for bi in range(nseq) for j in range(nsub)]

    def gate_ab(half_pre_r, half_pre_i, half_sp, half_xc):
        z = jnp.tanh(half_pre_r) * half_sp + half_sp
        ixc = jnp.tanh(half_pre_i) * half_xc + half_xc
        a = jnp.exp2(z * (-LOG2_E))
        u = (1.0 + a * a) * jnp.tanh(z)
        bx = u * lax.rsqrt(jnp.maximum(u, SQRT_FLOOR)) * ixc
        return a, bx

    for bi, j, slab in units:
        xpad[slab, 0:SUBLANES, :] = jnp.zeros((SUBLANES, LANES), F32)
        xpad[slab, s + SUBLANES:s + 2 * SUBLANES, :] = jnp.zeros((SUBLANES, LANES), F32)
        xpad[slab, SUBLANES:s + SUBLANES, :] = xr_ref[bi, :, lanes[j]]
    params = []
    for j, ls in enumerate(lanes):
        params.append(dict(
            sp_f=(0.5 * RG_C) * _softplus(-lam_ref[0:1, ls]),
            sp_b=(0.5 * RG_C) * _softplus(-lam_ref[1:2, ls]),
            wg=wg_ref[cbase + j],
            bg=bg_ref[:, ls], cw=cw_ref[:, ls], cb=cb_ref[:, ls]))

    def fwd_lanes(unit, t0, carry):
        bi, j, slab = unit
        p = params[j]
        cw, bg = p["cw"], p["bg"]
        xcs = []
        for q in range(ntile):
            xs = {m: xpad[slab, _tile_rows(t0 + SUBLANES, q, m), :]
                  for m in range(-CONV_PAD_LEFT, SUBLANES + CONV_W - 1 - CONV_PAD_LEFT)}
            for sv in range(SUBLANES):
                acc = p["cb"]
                for tap in range(CONV_W):
                    acc = acc + xs[sv + tap - CONV_PAD_LEFT] * cw[tap:tap + 1]
                xcs.append(acc)
        xc = jnp.concatenate(xcs, axis=0)
        pre = jnp.dot(xc.astype(BF16), p["wg"], preferred_element_type=F32)
        hxc = 0.5 * xc
        a_f, b_f = gate_ab(pre[:, 0:128] + bg[0:1], pre[:, 128:256] + bg[1:2], p["sp_f"], hxc)
        a_b, b_b = gate_ab(pre[:, 256:384] + bg[2:3], pre[:, 384:512] + bg[3:4], p["sp_b"], hxc)
        ab[slab, pl.ds(t0, tb), :] = a_b
        bb[slab, pl.ds(t0, tb), :] = b_b

        def vreg_of(x):
            return lambda q, sv: x[q * SCAN_ROWS + sv * SUBLANES:
                                   q * SCAN_ROWS + (sv + 1) * SUBLANES]

        return _scan_block(vreg_of(a_f), vreg_of(b_f), hf, slab, t0, tb, carry, False)

    def bwd_lanes(unit, t0, carry):
        bi, j, slab = unit
        ls = lanes[j]

        def vreg_of(ref):
            return lambda q, sv: ref[slab, pl.ds(t0 + q * SCAN_ROWS + sv * SUBLANES, SUBLANES), :]

        carry = _scan_block(vreg_of(ab), vreg_of(bb), bb, slab, t0, tb, carry, True)
        y = ((hf[slab, pl.ds(t0, tb), :] + bb[slab, pl.ds(t0, tb), :])
             * _gelu_tanh(xg_ref[bi, pl.ds(t0, tb), ls]))
        y_ref[bi, pl.ds(t0, tb), ls] = y.astype(y_ref.dtype)
        return carry

    def fwd_block(t, carries):
        t0 = 0 if nblk == 1 else pl.multiple_of(t * tb, tb)
        return tuple(fwd_lanes(u, t0, c) for u, c in zip(units, carries))

    def bwd_block(tt, carries):
        t0 = 0 if nblk == 1 else pl.multiple_of((nblk - 1 - tt) * tb, tb)
        return tuple(bwd_lanes(u, t0, c) for u, c in zip(units, carries))

    for row, block in ((0, fwd_block), (1, bwd_block)):
        init = tuple(h0_ref[bi, row:row + 1, lanes[j]] for bi, j, _ in units)
        ends = block(0, init) if nblk == 1 else lax.fori_loop(0, nblk, block, init)
        for (bi, j, _), end in zip(units, ends):
            hl_ref[bi, row:row + 1, lanes[j]] = end


def _rglru_kernel(*refs, s, cc, tb):
    _rglru_seqs(*refs, s=s, cc=cc, tb=tb, cbase=pl.program_id(1) * (cc // LANES), nseq=1)


def _ctx_mixers_kernel(q_ref, k_ref, v_ref, dl_ref, sg_ref, *rglru_refs, tq, s, cc, tb, nseq):
    o_ref = rglru_refs[8]
    _attn_small_kernel(q_ref, k_ref, v_ref, dl_ref, sg_ref, o_ref, tq=tq)
    _rglru_seqs(*rglru_refs[:8], *rglru_refs[9:], s=s, cc=cc, tb=tb, cbase=0, nseq=nseq)


def _rglru_call(xr, xg, h0, cw, cb, wg, bg, lam, *, cc):
    b, s, _ = xr.shape
    nch = D_RG // cc
    tb = 512
    seq = pl.BlockSpec((1, s, cc), lambda bi, ci: (bi, 0, ci))
    st = pl.BlockSpec((1, 2, cc), lambda bi, ci: (bi, 0, ci))
    return pl.pallas_call(
        functools.partial(_rglru_kernel, s=s, cc=cc, tb=tb),
        grid=(b, nch),
        in_specs=[seq, seq, st,
                  pl.BlockSpec((CONV_W, cc), lambda bi, ci: (0, ci)),
                  pl.BlockSpec((1, cc), lambda bi, ci: (0, ci)),
                  pl.BlockSpec(wg.shape, lambda bi, ci: (0, 0, 0)),
                  pl.BlockSpec((4, cc), lambda bi, ci: (0, ci)),
                  pl.BlockSpec((2, cc), lambda bi, ci: (0, ci))],
        out_specs=[seq, st],
        out_shape=[jax.ShapeDtypeStruct((b, s, D_RG), BF16),
                   jax.ShapeDtypeStruct((b, 2, D_RG), F32)],
        scratch_shapes=[pltpu.VMEM((cc // LANES, s + 2 * SUBLANES, LANES), F32)]
        + [pltpu.VMEM((cc // LANES, s, LANES), F32)] * 3,
        compiler_params=pltpu.CompilerParams(
            dimension_semantics=("arbitrary", "arbitrary"), vmem_limit_bytes=VMEM_LIMIT),
        name="rglru",
    )(xr, xg, h0, cw, cb, wg, bg, lam)


def _ctx_mixers_call(q, k, v, diff_lambda, subln_g, xr, xg, h0, cw, cb, wg, bg, lam):
    b, s, _ = q.shape
    nseq = SEQS_PER_STEP
    assert b % nseq == 0 and s % SCAN_ROWS == 0
    seq = pl.BlockSpec((nseq, s, D_ATTN), lambda i: (i, 0, 0))
    kv = pl.BlockSpec((nseq * (k.shape[0] // b), V_DIM), lambda i: (i, 0))
    st = pl.BlockSpec((nseq, 2, D_RG), lambda i: (i, 0, 0))
    whole = lambda a: pl.BlockSpec(a.shape, lambda i: (0,) * a.ndim)
    slabs = nseq * (D_RG // LANES)
    return pl.pallas_call(
        functools.partial(_ctx_mixers_kernel, tq=s, s=s, cc=D_RG, tb=s, nseq=nseq),
        grid=(b // nseq,),
        in_specs=[seq, kv, kv, whole(diff_lambda), whole(subln_g), seq, seq, st,
                  whole(cw), whole(cb), whole(wg), whole(bg), whole(lam)],
        out_specs=[seq, seq, st],
        out_shape=[jax.ShapeDtypeStruct((b, s, D_ATTN), BF16),
                   jax.ShapeDtypeStruct((b, s, D_RG), BF16),
                   jax.ShapeDtypeStruct((b, 2, D_RG), F32)],
        scratch_shapes=[pltpu.VMEM((slabs, s + 2 * SUBLANES, LANES), F32)]
        + [pltpu.VMEM((slabs, s, LANES), F32)] * 3,
        compiler_params=pltpu.CompilerParams(
            dimension_semantics=("arbitrary",), vmem_limit_bytes=VMEM_LIMIT),
        name="ctx_mixers",
    )(q, k, v, diff_lambda, subln_g, xr, xg, h0, cw, cb, wg, bg, lam)


def _rope_tables(s):
    pos = np.arange(s)
    row = (pos // GRID_W).astype(np.float32)
    col = (pos % GRID_W).astype(np.float32)
    n_freq = HEAD_DIM // 4
    inv_freq = (ROPE_BASE ** (-np.arange(n_freq, dtype=np.float32) / n_freq)).astype(np.float32)
    p = np.arange(LANES) % HEAD_DIM
    freq = inv_freq[p % n_freq]
    ang = (np.where((p < HEAD_DIM // 2)[None, :], row[:, None], col[:, None])
           * freq[None, :]).astype(np.float32)
    sign = np.where((p % (2 * n_freq)) < n_freq, -1.0, 1.0)
    cos = np.cos(ang.astype(np.float64)).astype(np.float32)
    sin = (np.sin(ang.astype(np.float64)) * sign[None, :]).astype(np.float32)
    return jnp.asarray(cos), jnp.asarray(sin)


def _gate_weights(w_r, w_i):
    def bd(w):
        w = w.reshape(4, 2, RG_BLOCK_W, RG_BLOCK_W)
        z = jnp.zeros_like(w[:, 0])
        top = jnp.concatenate([w[:, 0], z], axis=2)
        bot = jnp.concatenate([z, w[:, 1]], axis=2)
        return jnp.concatenate([top, bot], axis=1)
    w = jnp.concatenate([bd(w_r[0]), bd(w_i[0]), bd(w_r[1]), bd(w_i[1])], axis=2)
    return (0.5 * w).astype(BF16)


def kernel(x_prompt, x_sample, cache_attn_k, cache_attn_v, state_rglru, c, c_ctx, norm_g, w_mod, b_mod, ffn_w_gate, ffn_w_up, ffn_w_down, w_in, w_out, diff_lambda, subln_g, conv_w, conv_b, rg_w_r, rg_b_r, rg_w_i, rg_b_i, rg_lambda, final_g):
    l = 0
    bsz, seq, _ = x_prompt.shape
    dbsz, dseq, _ = x_sample.shape
    past = cache_attn_k.shape[2]

    ng = norm_g[l]
    ffn0_w = [w[l, 0].astype(BF16) for w in (ffn_w_gate, ffn_w_up, ffn_w_down)]
    win = w_in[l].astype(BF16)
    wg = _gate_weights(rg_w_r[l], rg_w_i[l])
    bg = 0.5 * jnp.stack([rg_b_r[l, 0], rg_b_i[l, 0], rg_b_r[l, 1], rg_b_i[l, 1]])
    cw = conv_w[l]
    cb = conv_b[l][None, :]
    lam = rg_lambda[l]
    dl = diff_lambda[l]
    sg = subln_g[l][None, :]
    fg = final_g[None, :]
    tm = 1024

    c8 = jnp.concatenate([c_ctx[None, :], c, jnp.zeros((SUBLANES - 1 - dbsz, D_MODEL), F32)], axis=0)
    mod3 = _mod_call(c8, w_mod[l], b_mod[l][None, :]).reshape(SUBLANES, N_MOD, D_MODEL)

    def mixers(x, *, rows_per_mod, mod_base, nb, s, h0, casts, k_ctx=None, v_ctx=None,
               rope_tabs=None):
        x1, cast = _ffn_call(x, mod3, ng, *ffn0_w, sub=0, rows_per_mod=rows_per_mod,
                             mod_base=mod_base, tm=tm, casts=casts)
        q, k, v, xr, xg = _proj_call(x1, mod3, ng, win, rows_per_mod=rows_per_mod,
                                     mod_base=mod_base, tm=tm, rope_tabs=rope_tabs)
        q3 = q.reshape(nb, s, D_ATTN)
        xr3 = xr.reshape(nb, s, D_RG)
        xg3 = xg.reshape(nb, s, D_RG)
        if k_ctx is None:
            o, rg, h_last = _ctx_mixers_call(q3, k, v, dl, sg, xr3, xg3, h0, cw, cb, wg, bg, lam)
        else:
            o = _attn_call(q3, k.reshape(nb, s, D_ATTN), v.reshape(nb, s, D_ATTN), k_ctx, v_ctx,
                           dl, sg, tq=LATENT_TQ)
            rg, h_last = _rglru_call(xr3, xg3, h0, cw, cb, wg, bg, lam, cc=LATENT_CC)
        return x1, o.reshape(nb * s, D_ATTN), rg.reshape(nb * s, D_RG), k, v, h_last, cast

    x1p, op, rgp, k_new, v_new, h_new, (wgate1, wup1) = mixers(
        x_prompt.reshape(bsz * seq, D_MODEL), rows_per_mod=bsz * seq, mod_base=0, nb=bsz, s=seq,
        h0=jnp.zeros((bsz, 2, D_RG), F32), casts=[(ffn_w_gate, (l, 1)), (ffn_w_up, (l, 1))])
    x1s, os_, rgs, _, _, _, (wdown1, wout) = mixers(
        x_sample.reshape(dbsz * dseq, D_MODEL), rows_per_mod=dseq, mod_base=1, nb=dbsz, s=dseq,
        k_ctx=cache_attn_k[:, l].reshape(dbsz, past, D_ATTN).astype(BF16),
        v_ctx=cache_attn_v[:, l].reshape(dbsz, past, D_ATTN).astype(BF16),
        h0=state_rglru[:, l], rope_tabs=_rope_tables(dseq),
        casts=[(ffn_w_down, (l, 1)), (w_out, (l,))])

    def second_ffn(x1, o, rg, *, rows_per_mod, mod_base):
        y, _ = _ffn_call(x1, mod3, ng, wgate1, wup1, wdown1, sub=2, rows_per_mod=rows_per_mod,
                         mod_base=mod_base, tm=tm, mix=(o, rg, wout), final_g=fg)
        return y

    yp = second_ffn(x1p, op, rgp, rows_per_mod=bsz * seq, mod_base=0)
    ys = second_ffn(x1s, os_, rgs, rows_per_mod=dseq, mod_base=1)

    return (yp.reshape(bsz, seq, D_MODEL),
            ys.reshape(dbsz, dseq, D_MODEL),
            k_new.reshape(bsz, 1, seq, N_HEADS, V_DIM),
            v_new.reshape(bsz, 1, seq, N_HEADS, V_DIM),
            h_new.reshape(bsz, 1, 2, D_RG))
```

```python
import functools
import math

import jax
import jax.numpy as jnp
import numpy as np
from jax import lax
from jax.experimental import pallas as pl
from jax.experimental.pallas import tpu as pltpu

F32 = jnp.float32
BF16 = jnp.bfloat16

D_MODEL = 1024
N_HEADS = 4
HEAD_DIM = 64
V_DIM = 2 * HEAD_DIM
D_ATTN = N_HEADS * V_DIM
D_RG = 512
RG_BLOCK_W = 64
RG_C = 8.0
CONV_W = 4
CONV_PAD_LEFT = 2
D_FF = 2816
N_MOD = 9
GRID_W = 64
ROPE_BASE = 10000.0
EPS = 1e-6
LAM_INIT = 0.8 - 0.6 * math.exp(-0.3 * 0)
LOG2_E = math.log2(math.e)

LANES = 128
SUBLANES = 8
MXU_N = 256
LATENT_TQ = 256
LATENT_CC = 2 * LANES
S_BUFS = 2
SEQS_PER_STEP = 4
SQRT_FLOOR = 1e-30
SCAN_ROWS = SUBLANES * SUBLANES
EXP_ROWS = 64
FF_CHUNK = MXU_N
N_FF_CHUNKS = D_FF // FF_CHUNK
VMEM_LIMIT = 56 * 1024 * 1024


def _sigmoid(x):
    return 1.0 / (1.0 + jnp.exp(-x))


def _gelu_tanh(x):
    return 0.5 * x * (1.0 + jnp.tanh(math.sqrt(2.0 / math.pi) * (x + 0.044715 * (x * x * x))))


def _rms(x, g):
    ms = jnp.mean(x * x, axis=-1, keepdims=True)
    return x * lax.rsqrt(ms + EPS) * g


def _mod_kernel(c_ref, w_ref, b_ref, o_ref):
    c = c_ref[...]
    s = (c * _sigmoid(c)).astype(BF16)
    o_ref[...] = jnp.dot(s, w_ref[...].astype(BF16), preferred_element_type=F32) + b_ref[...]


def _mod_call(c8, w_mod, b_mod):
    n = w_mod.shape[1]
    tn = 2304
    return pl.pallas_call(
        _mod_kernel,
        grid=(n // tn,),
        in_specs=[
            pl.BlockSpec((SUBLANES, D_MODEL), lambda j: (0, 0)),
            pl.BlockSpec((D_MODEL, tn), lambda j: (0, j)),
            pl.BlockSpec((1, tn), lambda j: (0, j)),
        ],
        out_specs=pl.BlockSpec((SUBLANES, tn), lambda j: (0, j)),
        out_shape=jax.ShapeDtypeStruct((SUBLANES, n), F32),
        compiler_params=pltpu.CompilerParams(
            dimension_semantics=("arbitrary",), vmem_limit_bytes=VMEM_LIMIT),
        name="mod",
    )(c8, w_mod, b_mod)


def _ffn_kernel(*refs, sub, fuse_mix, final_norm, n_casts):
    it = iter(refs)
    x_ref = next(it)
    if fuse_mix:
        o_ref_in = next(it)
        rg_ref = next(it)
        wout_ref = next(it)
    mod_ref = next(it)
    ng_ref = next(it)
    wg_hbm = next(it)
    wu_hbm = next(it)
    wd_hbm = next(it)
    fg_ref = next(it) if final_norm else None
    cast_in = [next(it) for _ in range(n_casts)]
    out_ref = next(it)
    for src in cast_in:
        dst = next(it)
        dst[...] = src[...].astype(dst.dtype)
    acc_ref = out_ref
    h_ref = next(it)
    a0_ref = next(it)
    a1_ref = next(it)
    wg_ref = next(it)
    wu_ref = next(it)
    wd_ref = next(it)
    w_sem = next(it)

    head = 2 * FF_CHUNK

    def weight_copies(part):
        cols = slice(0, head) if part == 0 else slice(head, D_FF)
        return [pltpu.make_async_copy(wg_hbm.at[:, cols], wg_ref.at[:, cols], w_sem.at[part, 0]),
                pltpu.make_async_copy(wu_hbm.at[:, cols], wu_ref.at[:, cols], w_sem.at[part, 1]),
                pltpu.make_async_copy(wd_hbm.at[cols, :], wd_ref.at[cols, :], w_sem.at[part, 2])]

    first_step = pl.program_id(0) == 0

    @pl.when(first_step)
    def _():
        for part in (0, 1):
            for cp in weight_copies(part):
                cp.start()
        for cp in weight_copies(0):
            cp.wait()

    x = x_ref[...]
    if fuse_mix:
        mix = jnp.dot(o_ref_in[...], wout_ref[0:D_ATTN, :], preferred_element_type=F32)
        mix = mix + jnp.dot(rg_ref[...], wout_ref[D_ATTN:, :], preferred_element_type=F32)
        x = x + mod_ref[0, 5:6, :] * mix
    sh = mod_ref[0, 3 * sub:3 * sub + 1, :]
    sc = mod_ref[0, 3 * sub + 1:3 * sub + 2, :]
    gate = mod_ref[0, 3 * sub + 2:3 * sub + 3, :]
    h_ref[...] = (_rms(x, ng_ref[sub:sub + 1, :]) * (1.0 + sc) + sh).astype(BF16)

    def chunk(j):
        return pl.ds(pl.multiple_of(j * FF_CHUNK, FF_CHUNK), FF_CHUNK)

    def gate_up(j):
        h = h_ref[...]
        g = jnp.dot(h, wg_ref[:, chunk(j)], preferred_element_type=F32)
        u = jnp.dot(h, wu_ref[:, chunk(j)], preferred_element_type=F32)
        return (g * _sigmoid(g) * u).astype(BF16)

    def down(j, a):
        return jnp.dot(a, wd_ref[chunk(j), :], preferred_element_type=F32)

    a0_ref[...] = gate_up(0)
    a1_ref[...] = gate_up(1)
    acc_ref[...] = down(0, a0_ref[...])

    @pl.when(first_step)
    def _():
        for cp in weight_copies(1):
            cp.wait()

    def body(i, carry):
        a0_ref[...] = gate_up(2 * i + 2)
        acc_ref[...] += down(2 * i + 1, a1_ref[...])
        a1_ref[...] = gate_up(2 * i + 3)
        acc_ref[...] += down(2 * i + 2, a0_ref[...])
        return carry

    assert N_FF_CHUNKS % 2 == 1 and N_FF_CHUNKS >= 5
    lax.fori_loop(0, (N_FF_CHUNKS - 3) // 2, body, 0, unroll=2)
    a0_ref[...] = gate_up(N_FF_CHUNKS - 1)
    acc_ref[...] += down(N_FF_CHUNKS - 2, a1_ref[...])
    y = x + (0.5 * gate) * (acc_ref[...] + down(N_FF_CHUNKS - 1, a0_ref[...]))
    if final_norm:
        y = _rms(y, fg_ref[...])
    out_ref[...] = y


def _ffn_call(x, mod3, ng, wg, wu, wd, *, sub, rows_per_mod, mod_base, tm,
              mix=None, final_g=None, casts=()):
    t = x.shape[0]
    steps = t // tm
    fuse_mix = mix is not None
    final_norm = final_g is not None
    tiles_per_mod = rows_per_mod // tm

    def row_map(i):
        return (i, 0)

    def mod_map(i):
        return (mod_base + i // tiles_per_mod, 0, 0)

    const2 = lambda i: (0, 0)
    in_specs = [pl.BlockSpec((tm, D_MODEL), row_map)]
    args = [x]
    if fuse_mix:
        o, rg, wout = mix
        in_specs += [pl.BlockSpec((tm, D_ATTN), row_map),
                     pl.BlockSpec((tm, D_RG), row_map),
                     pl.BlockSpec(wout.shape, const2, pipeline_mode=pl.Buffered(1))]
        args += [o, rg, wout]
    in_specs += [pl.BlockSpec((1, N_MOD, D_MODEL), mod_map),
                 pl.BlockSpec(ng.shape, const2),
                 pl.BlockSpec(memory_space=pl.ANY),
                 pl.BlockSpec(memory_space=pl.ANY),
                 pl.BlockSpec(memory_space=pl.ANY)]
    args += [mod3, ng, wg, wu, wd]
    if final_norm:
        in_specs.append(pl.BlockSpec((1, D_MODEL), const2))
        args.append(final_g)
    out_specs = [pl.BlockSpec((tm, D_MODEL), row_map)]
    out_shape = [jax.ShapeDtypeStruct((t, D_MODEL), F32)]
    for arr, lead in casts:
        rows, cols = arr.shape[-2:]
        rb = rows // steps
        assert rows % steps == 0 and rb % (2 * SUBLANES) == 0
        in_specs.append(pl.BlockSpec((None,) * len(lead) + (rb, cols),
                                     lambda i, lead=lead: lead + (i, 0)))
        args.append(arr)
        out_specs.append(pl.BlockSpec((rb, cols), row_map))
        out_shape.append(jax.ShapeDtypeStruct((rows, cols), BF16))
    outs = pl.pallas_call(
        functools.partial(_ffn_kernel, sub=sub, fuse_mix=fuse_mix, final_norm=final_norm,
                          n_casts=len(casts)),
        grid=(steps,),
        in_specs=in_specs,
        out_specs=out_specs,
        out_shape=out_shape,
        scratch_shapes=[pltpu.VMEM((tm, D_MODEL), BF16),
                        pltpu.VMEM((tm, FF_CHUNK), BF16), pltpu.VMEM((tm, FF_CHUNK), BF16),
                        pltpu.VMEM(wg.shape, BF16), pltpu.VMEM(wu.shape, BF16),
                        pltpu.VMEM(wd.shape, BF16), pltpu.SemaphoreType.DMA((2, 3))],
        compiler_params=pltpu.CompilerParams(
            dimension_semantics=("arbitrary",), vmem_limit_bytes=VMEM_LIMIT),
        name="ffn%d" % sub,
    )(*args)
    return outs[0], list(outs[1:])


def _rope(x, cos, sin_signed, first_half):
    outs = []
    for cblk in range(x.shape[1] // LANES):
        xs = x[:, cblk * LANES:(cblk + 1) * LANES]
        partner = jnp.where(first_half, pltpu.roll(xs, LANES - 16, axis=1),
                            pltpu.roll(xs, 16, axis=1))
        outs.append(xs * cos + partner * sin_signed)
    return jnp.concatenate(outs, axis=1)


def _proj_kernel(*refs, rope):
    it = iter(refs)
    x_ref = next(it)
    mod_ref = next(it)
    ng_ref = next(it)
    win_ref = next(it)
    if rope:
        cos_ref = next(it)
        sin_ref = next(it)
    q_ref, k_ref, v_ref, xr_ref, xg_ref = it

    x = x_ref[...]
    sh = mod_ref[0, 3:4, :]
    sc = mod_ref[0, 4:5, :]
    h = (_rms(x, ng_ref[1:2, :]) * (1.0 + sc) + sh).astype(BF16)

    def col(j):
        return jnp.dot(h, win_ref[:, j * D_ATTN:(j + 1) * D_ATTN], preferred_element_type=F32)

    q = col(0)
    k = col(1)
    if rope:
        cos = cos_ref[...]
        sin = sin_ref[...]
        lane = lax.broadcasted_iota(jnp.int32, (1, LANES), 1)
        first_half = (lane % 32) < 16
        q = _rope(q, cos, sin, first_half)
        k = _rope(k, cos, sin, first_half)
    q_ref[...] = (q * (HEAD_DIM ** -0.5 * LOG2_E)).astype(q_ref.dtype)
    v = col(2)
    if rope:
        k_ref[...] = k.astype(k_ref.dtype)
        v_ref[...] = v.astype(v_ref.dtype)
    else:
        tm = x.shape[0]
        for hd in range(N_HEADS):
            k_ref[pl.ds(hd, tm, stride=N_HEADS), :] = k[:, hd * V_DIM:(hd + 1) * V_DIM]
            v_ref[pl.ds(hd, tm, stride=N_HEADS), :] = v[:, hd * V_DIM:(hd + 1) * V_DIM]
    xr_ref[...] = col(3)
    xg_ref[...] = col(4)


def _proj_call(x, mod3, ng, win, *, rows_per_mod, mod_base, tm, rope_tabs=None):
    t = x.shape[0]
    rope = rope_tabs is not None
    tiles_per_mod = rows_per_mod // tm
    row_map = lambda i: (i, 0)
    const2 = lambda i: (0, 0)
    in_specs = [pl.BlockSpec((tm, D_MODEL), row_map),
                pl.BlockSpec((1, N_MOD, D_MODEL), lambda i: (mod_base + i // tiles_per_mod, 0, 0)),
                pl.BlockSpec(ng.shape, const2),
                pl.BlockSpec(win.shape, const2, pipeline_mode=pl.Buffered(1))]
    args = [x, mod3, ng, win]
    if rope:
        cos, sin = rope_tabs
        tiles_per_seq = cos.shape[0] // tm
        tab_map = lambda i: (i % tiles_per_seq, 0)
        in_specs += [pl.BlockSpec((tm, LANES), tab_map), pl.BlockSpec((tm, LANES), tab_map)]
        args += [cos, sin]
    half = pl.BlockSpec((tm, D_ATTN), row_map)
    if rope:
        kv_spec, kv_shape = half, jax.ShapeDtypeStruct((t, D_ATTN), BF16)
    else:
        kv_spec = pl.BlockSpec((tm * N_HEADS, V_DIM), row_map)
        kv_shape = jax.ShapeDtypeStruct((t * N_HEADS, V_DIM), F32)
    return pl.pallas_call(
        functools.partial(_proj_kernel, rope=rope),
        grid=(t // tm,),
        in_specs=in_specs,
        out_specs=[half, kv_spec, kv_spec, half, half],
        out_shape=[jax.ShapeDtypeStruct((t, D_ATTN), BF16),
                   kv_shape,
                   kv_shape,
                   jax.ShapeDtypeStruct((t, D_RG), F32),
                   jax.ShapeDtypeStruct((t, D_RG), F32)],
        compiler_params=pltpu.CompilerParams(
            dimension_semantics=("arbitrary",), vmem_limit_bytes=VMEM_LIMIT),
        name="proj",
    )(*args)


def _diff_lambda(dl_ref):
    dl = dl_ref[...]
    return (jnp.exp(jnp.sum(dl[0:1] * dl[1:2], axis=-1, keepdims=True))
            - jnp.exp(jnp.sum(dl[2:3] * dl[3:4], axis=-1, keepdims=True)) + LAM_INIT)


def _stack_maps(q):
    map0 = lax.broadcasted_iota(jnp.int32, (1, V_DIM), 1) < HEAD_DIM
    zero = jnp.zeros_like(q)
    return jnp.concatenate([jnp.where(map0, q, zero), jnp.where(map0, zero, q)], axis=0)


def _finish_head(o2, l, lam, sg, tq):
    o = o2[:tq] * (1.0 / l[:tq]) - o2[tq:] * (lam / l[tq:])
    return _rms(o, sg) * (1.0 - LAM_INIT)


def _attn_small_kernel(q_ref, k_ref, v_ref, dl_ref, sg_ref, o_ref, *, tq):
    lam = _diff_lambda(dl_ref)
    nseq = q_ref.shape[0]
    sk = k_ref.shape[0] // (N_HEADS * nseq)
    ones = jnp.ones((sk, V_DIM), BF16)
    for bi in range(nseq):
        for hd in range(N_HEADS):
            cols = slice(hd * V_DIM, (hd + 1) * V_DIM)
            head_rows = pl.ds(bi * sk * N_HEADS + hd, sk, stride=N_HEADS)
            q2 = _stack_maps(q_ref[bi, :, cols])
            k = k_ref[head_rows, :].astype(BF16)
            s = lax.dot_general(q2, k, (((1,), (1,)), ((), ())), preferred_element_type=F32)
            e = jnp.exp2(s - jnp.max(s, axis=-1, keepdims=True))
            v1 = jnp.concatenate([v_ref[head_rows, :].astype(BF16), ones], axis=1)
            o2 = jnp.dot(e.astype(BF16), v1, preferred_element_type=F32)
            o = _finish_head(o2[:, :V_DIM], o2[:, V_DIM:], lam, sg_ref[...], tq)
            o_ref[bi, :, cols] = o.astype(o_ref.dtype)


def _attn_pipe_kernel(q_ref, k_ref, v_ref, kc_ref, vc_ref, dl_ref, sg_ref, o_ref, *scratch, tq, kt):
    lam = _diff_lambda(dl_ref)
    s_bufs, scratch = scratch[:S_BUFS], scratch[S_BUFS:]
    sc_bufs, scratch = scratch[:S_BUFS], scratch[S_BUFS:]
    p_bufs, pc_bufs = scratch[0:2], scratch[2:4]
    q2_scr, mv_scr, mb_scr, oacc_scr = scratch[4:]
    n_steps = k_ref.shape[1] // kt
    for t in range(N_HEADS + 2):
        ha, hb, hc = t, t - 1, t - 2
        do_a, do_b, do_c = 0 <= ha < N_HEADS, 0 <= hb < N_HEADS, 0 <= hc < N_HEADS
        cols_a = slice(ha * V_DIM, (ha + 1) * V_DIM)
        cols_c = slice(hc * V_DIM, (hc + 1) * V_DIM)
        if do_a:
            q2_scr[...] = _stack_maps(q_ref[0, :, cols_a])
            mv_scr[...] = jnp.full(mv_scr.shape, -jnp.inf, F32)
        if do_c:
            oacc_scr[...] = jnp.zeros(oacc_scr.shape, F32)

        def key_block(k_blk, v_blk, s_a, s_b, p_b, p_c):
            whole = (slice(None), slice(None))
            if do_a:
                s_ref, s_idx = s_a
                kb = k_blk()
                for c0 in range(0, kb.shape[0], MXU_N):
                    s = lax.dot_general(q2_scr[...], kb[c0:c0 + MXU_N], (((1,), (1,)), ((), ())),
                                        preferred_element_type=F32)
                    s_ref[s_idx + (slice(None), slice(c0, c0 + MXU_N))] = s
                    mv_scr[...] = jnp.maximum(mv_scr[...],
                                              jnp.maximum(s[:, :LANES], s[:, LANES:]))
            if do_b:
                s_ref, s_idx = s_b
                p_ref, p_idx = p_b
                n_keys = s_ref.shape[-1]
                for r0 in range(0, 2 * tq, EXP_ROWS):
                    rows = slice(r0, r0 + EXP_ROWS)
                    mb = mb_scr[rows, :]
                    for c0 in range(0, n_keys, LANES):
                        at = (rows, slice(c0, c0 + LANES))
                        p_ref[p_idx + at] = jnp.exp2(s_ref[s_idx + at] - mb).astype(BF16)
            if do_c:
                v = v_blk()
                v1 = jnp.concatenate([v, jnp.ones(v.shape, BF16)], axis=1)
                oacc_scr[...] += jnp.dot(p_c[0][p_c[1] + whole], v1, preferred_element_type=F32)

        def new_keys_step(j, carry):
            keys = pl.ds(pl.multiple_of(j * kt, kt), kt)
            key_block(lambda: k_ref[0, keys, cols_a], lambda: v_ref[0, keys, cols_c],
                      (s_bufs[ha % S_BUFS], (j,)), (s_bufs[hb % S_BUFS], (j,)),
                      (p_bufs[hb % 2], (j,)), (p_bufs[hc % 2], (j,)))
            return carry

        lax.fori_loop(0, n_steps, new_keys_step, 0, unroll=True)
        key_block(lambda: kc_ref[0, :, cols_a], lambda: vc_ref[0, :, cols_c],
                  (sc_bufs[ha % S_BUFS], ()), (sc_bufs[hb % S_BUFS], ()),
                  (pc_bufs[hb % 2], ()), (pc_bufs[hc % 2], ()))
        if do_a:
            m = jnp.max(mv_scr[...], axis=-1, keepdims=True)
            mb_scr[...] = jnp.broadcast_to(m, mb_scr.shape)
        if do_c:
            o = _finish_head(oacc_scr[:, :V_DIM], oacc_scr[:, V_DIM:], lam, sg_ref[...], tq)
            o_ref[0, :, cols_c] = o.astype(o_ref.dtype)


def _attn_call(q, k, v, kc, vc, diff_lambda, subln_g, *, tq):
    b, sq, _ = q.shape
    sk = k.shape[1]
    past = kc.shape[1]
    kt = 1024
    rows = 2 * tq
    qmap = lambda bi, qi: (bi, qi, 0)
    seqmap = lambda bi, qi: (bi, 0, 0)
    const2 = lambda bi, qi: (0, 0)
    kv_spec = pl.BlockSpec((1, sk, D_ATTN), seqmap)
    cache_spec = pl.BlockSpec((1, past, D_ATTN), seqmap)
    scratch = ([pltpu.VMEM((sk // kt, rows, kt), F32)] * S_BUFS
               + [pltpu.VMEM((rows, past), F32)] * S_BUFS
               + [pltpu.VMEM((sk // kt, rows, kt), BF16)] * 2 + [pltpu.VMEM((rows, past), BF16)] * 2
               + [pltpu.VMEM((rows, V_DIM), BF16)] + [pltpu.VMEM((rows, LANES), F32)] * 2
               + [pltpu.VMEM((rows, 2 * V_DIM), F32)])
    return pl.pallas_call(
        functools.partial(_attn_pipe_kernel, tq=tq, kt=kt),
        grid=(b, sq // tq),
        in_specs=[pl.BlockSpec((1, tq, D_ATTN), qmap), kv_spec, kv_spec, cache_spec, cache_spec,
                  pl.BlockSpec(diff_lambda.shape, const2), pl.BlockSpec(subln_g.shape, const2)],
        out_specs=pl.BlockSpec((1, tq, D_ATTN), qmap),
        out_shape=jax.ShapeDtypeStruct((b, sq, D_ATTN), BF16),
        scratch_shapes=scratch,
        compiler_params=pltpu.CompilerParams(
            dimension_semantics=("arbitrary", "arbitrary"), vmem_limit_bytes=VMEM_LIMIT),
        name="attn",
    )(q, k, v, kc, vc, diff_lambda, subln_g)


def _scan8(a, b, reverse):
    sub = lax.broadcasted_iota(jnp.int32, a.shape, 1)
    for d in (1, 2, 4):
        if reverse:
            shift, m = SUBLANES - d, sub < SUBLANES - d
        else:
            shift, m = d, sub >= d
        a_s = pltpu.roll(a, shift, axis=1)
        b_s = pltpu.roll(b, shift, axis=1)
        b = jnp.where(m, a * b_s + b, b)
        a = jnp.where(m, a * a_s, a)
    return a, b


def _scan_tile(load_a, load_b, store_h, carry, reverse):
    order = range(SUBLANES - 1, -1, -1) if reverse else range(SUBLANES)
    hs, ps = {}, {}
    h = p = None
    for s in order:
        a, b = load_a(s), load_b(s)
        h = b if h is None else a * h + b
        p = a if p is None else a * p
        hs[s], ps[s] = h, p
    pp, hh = _scan8(p[None], h[None], reverse)
    after = hh[0] + pp[0] * carry
    sub = lax.broadcasted_iota(jnp.int32, after.shape, 0)
    if reverse:
        before = jnp.where(sub == SUBLANES - 1, carry, pltpu.roll(after, SUBLANES - 1, axis=0))
        carry = after[0:1]
    else:
        before = jnp.where(sub == 0, carry, pltpu.roll(after, 1, axis=0))
        carry = after[SUBLANES - 1:SUBLANES]
    for s in order:
        store_h(s, hs[s] + ps[s] * before)
    return carry


def _tile_rows(t0, q, s):
    return pl.ds(t0 + q * SCAN_ROWS + s, SUBLANES, stride=SUBLANES)


def _scan_block(load_a, load_b, h_ref, slab, t0, tb, carry, reverse):
    tiles = range(tb // SCAN_ROWS)
    for q in (reversed(tiles) if reverse else tiles):
        def store_h(s, h, q=q):
            h_ref[slab, _tile_rows(t0, q, s), :] = h

        carry = _scan_tile(functools.partial(load_a, q), functools.partial(load_b, q),
                           store_h, carry, reverse)
    return carry


def _softplus(x):
    return jnp.maximum(x, 0.0) + jnp.log(1.0 + jnp.exp(-jnp.abs(x)))


def _rglru_seqs(xr_ref, xg_ref, h0_ref, cw_ref, cb_ref, wg_ref, bg_ref, lam_ref,
                y_ref, hl_ref, xpad, ab, bb, hf, *, s, cc, tb, cbase, nseq):
    nblk = s // tb
    nsub = cc // LANES
    ntile = tb // SCAN_ROWS
    lanes = [slice(j * LANES, (j + 1) * LANES) for j in range(nsub)]
    units = [(bi, j, bi * nsub + j) for bi in range(nseq) for j in range(nsub)]

    def gate_ab(half_pre_r, half_pre_i, half_sp, half_xc):
        z = jnp.tanh(half_pre_r) * half_sp + half_sp
        ixc = jnp.tanh(half_pre_i) * half_xc + half_xc
        a = jnp.exp2(z * (-LOG2_E))
        u = (1.0 + a * a) * jnp.tanh(z)
        bx = u * lax.rsqrt(jnp.maximum(u, SQRT_FLOOR)) * ixc
        return a, bx

    for bi, j, slab in units:
        xpad[slab, 0:SUBLANES, :] = jnp.zeros((SUBLANES, LANES), F32)
        xpad[slab, s + SUBLANES:s + 2 * SUBLANES, :] = jnp.zeros((SUBLANES, LANES), F32)
        xpad[slab, SUBLANES:s + SUBLANES, :] = xr_ref[bi, :, lanes[j]]
    params = []
    for j, ls in enumerate(lanes):
        params.append(dict(
            sp_f=(0.5 * RG_C) * _softplus(-lam_ref[0:1, ls]),
            sp_b=(0.5 * RG_C) * _softplus(-lam_ref[1:2, ls]),
            wg=wg_ref[cbase + j],
            bg=bg_ref[:, ls], cw=cw_ref[:, ls], cb=cb_ref[:, ls]))

    def fwd_lanes(unit, t0, carry):
        bi, j, slab = unit
        p = params[j]
        cw, bg = p["cw"], p["bg"]
        xcs = []
        for q in range(ntile):
            xs = {m: xpad[slab, _tile_rows(t0 + SUBLANES, q, m), :]
                  for m in range(-CONV_PAD_LEFT, SUBLANES + CONV_W - 1 - CONV_PAD_LEFT)}
            for sv in range(SUBLANES):
                acc = p["cb"]
                for tap in range(CONV_W):
                    acc = acc + xs[sv + tap - CONV_PAD_LEFT] * cw[tap:tap + 1]
                xcs.append(acc)
        xc = jnp.concatenate(xcs, axis=0)
        pre = jnp.dot(xc.astype(BF16), p["wg"], preferred_element_type=F32)
        hxc = 0.5 * xc
        a_f, b_f = gate_ab(pre[:, 0:128] + bg[0:1], pre[:, 128:256] + bg[1:2], p["sp_f"], hxc)
        a_b, b_b = gate_ab(pre[:, 256:384] + bg[2:3], pre[:, 384:512] + bg[3:4], p["sp_b"], hxc)
        ab[slab, pl.ds(t0, tb), :] = a_b
        bb[slab, pl.ds(t0, tb), :] = b_b

        def vreg_of(x):
            return lambda q, sv: x[q * SCAN_ROWS + sv * SUBLANES:
                                   q * SCAN_ROWS + (sv + 1) * SUBLANES]

        return _scan_block(vreg_of(a_f), vreg_of(b_f), hf, slab, t0, tb, carry, False)

    def bwd_lanes(unit, t0, carry):
        bi, j, slab = unit
        ls = lanes[j]

        def vreg_of(ref):
            return lambda q, sv: ref[slab, pl.ds(t0 + q * SCAN_ROWS + sv * SUBLANES, SUBLANES), :]

        carry = _scan_block(vreg_of(ab), vreg_of(bb), bb, slab, t0, tb, carry, True)
        y = ((hf[slab, pl.ds(t0, tb), :] + bb[slab, pl.ds(t0, tb), :])
             * _gelu_tanh(xg_ref[bi, pl.ds(t0, tb), ls]))
        y_ref[bi, pl.ds(t0, tb), ls] = y.astype(y_ref.dtype)
        return carry

    def fwd_block(t, carries):
        t0 = 0 if nblk == 1 else pl.multiple_of(t * tb, tb)
        return tuple(fwd_lanes(u, t0, c) for u, c in zip(units, carries))

    def bwd_block(tt, carries):
        t0 = 0 if nblk == 1 else pl.multiple_of((nblk - 1 - tt) * tb, tb)
        return tuple(bwd_lanes(u, t0, c) for u, c in zip(units, carries))

    for row, block in ((0, fwd_block), (1, bwd_block)):
        init = tuple(h0_ref[bi, row:row + 1, lanes[j]] for bi, j, _ in units)
        ends = block(0, init) if nblk == 1 else lax.fori_loop(0, nblk, block, init)
        for (bi, j, _), end in zip(units, ends):
            hl_ref[bi, row:row + 1, lanes[j]] = end


def _rglru_kernel(*refs, s, cc, tb):
    _rglru_seqs(*refs, s=s, cc=cc, tb=tb, cbase=pl.program_id(1) * (cc // LANES), nseq=1)


def _ctx_mixers_kernel(q_ref, k_ref, v_ref, dl_ref, sg_ref, *rglru_refs, tq, s, cc, tb, nseq):
    o_ref = rglru_refs[8]
    _attn_small_kernel(q_ref, k_ref, v_ref, dl_ref, sg_ref, o_ref, tq=tq)
    _rglru_seqs(*rglru_refs[:8], *rglru_refs[9:], s=s, cc=cc, tb=tb, cbase=0, nseq=nseq)


def _rglru_call(xr, xg, h0, cw, cb, wg, bg, lam, *, cc):
    b, s, _ = xr.shape
    nch = D_RG // cc
    tb = 1024
    seq = pl.BlockSpec((1, s, cc), lambda bi, ci: (bi, 0, ci))
    st = pl.BlockSpec((1, 2, cc), lambda bi, ci: (bi, 0, ci))
    return pl.pallas_call(
        functools.partial(_rglru_kernel, s=s, cc=cc, tb=tb),
        grid=(b, nch),
        in_specs=[seq, seq, st,
                  pl.BlockSpec((CONV_W, cc), lambda bi, ci: (0, ci)),
                  pl.BlockSpec((1, cc), lambda bi, ci: (0, ci)),
                  pl.BlockSpec(wg.shape, lambda bi, ci: (0, 0, 0)),
                  pl.BlockSpec((4, cc), lambda bi, ci: (0, ci)),
                  pl.BlockSpec((2, cc), lambda bi, ci: (0, ci))],
        out_specs=[seq, st],
        out_shape=[jax.ShapeDtypeStruct((b, s, D_RG), BF16),
                   jax.ShapeDtypeStruct((b, 2, D_RG), F32)],
        scratch_shapes=[pltpu.VMEM((cc // LANES, s + 2 * SUBLANES, LANES), F32)]
        + [pltpu.VMEM((cc // LANES, s, LANES), F32)] * 3,
        compiler_params=pltpu.CompilerParams(
            dimension_semantics=("arbitrary", "arbitrary"), vmem_limit_bytes=VMEM_LIMIT),
        name="rglru",
    )(xr, xg, h0, cw, cb, wg, bg, lam)


def _ctx_mixers_call(q, k, v, diff_lambda, subln_g, xr, xg, h0, cw, cb, wg, bg, lam):
    b, s, _ = q.shape
    nseq = SEQS_PER_STEP
    assert b % nseq == 0 and s % SCAN_ROWS == 0
    seq = pl.BlockSpec((nseq, s, D_ATTN), lambda i: (i, 0, 0))
    kv = pl.BlockSpec((nseq * (k.shape[0] // b), V_DIM), lambda i: (i, 0))
    st = pl.BlockSpec((nseq, 2, D_RG), lambda i: (i, 0, 0))
    whole = lambda a: pl.BlockSpec(a.shape, lambda i: (0,) * a.ndim)
    slabs = nseq * (D_RG // LANES)
    return pl.pallas_call(
        functools.partial(_ctx_mixers_kernel, tq=s, s=s, cc=D_RG, tb=s, nseq=nseq),
        grid=(b // nseq,),
        in_specs=[seq, kv, kv, whole(diff_lambda), whole(subln_g), seq, seq, st,
                  whole(cw), whole(cb), whole(wg), whole(bg), whole(lam)],
        out_specs=[seq, seq, st],
        out_shape=[jax.ShapeDtypeStruct((b, s, D_ATTN), BF16),
                   jax.ShapeDtypeStruct((b, s, D_RG), BF16),
                   jax.ShapeDtypeStruct((b, 2, D_RG), F32)],
        scratch_shapes=[pltpu.VMEM((slabs, s + 2 * SUBLANES, LANES), F32)]
        + [pltpu.VMEM((slabs, s, LANES), F32)] * 3,
        compiler_params=pltpu.CompilerParams(
            dimension_semantics=("arbitrary",), vmem_limit_bytes=VMEM_LIMIT),
        name="ctx_mixers",
    )(q, k, v, diff_lambda, subln_g, xr, xg, h0, cw, cb, wg, bg, lam)


def _rope_tables(s):
    pos = np.arange(s)
    row = (pos // GRID_W).astype(np.float32)
    col = (pos % GRID_W).astype(np.float32)
    n_freq = HEAD_DIM // 4
    inv_freq = (ROPE_BASE ** (-np.arange(n_freq, dtype=np.float32) / n_freq)).astype(np.float32)
    p = np.arange(LANES) % HEAD_DIM
    freq = inv_freq[p % n_freq]
    ang = (np.where((p < HEAD_DIM // 2)[None, :], row[:, None], col[:, None])
           * freq[None, :]).astype(np.float32)
    sign = np.where((p % (2 * n_freq)) < n_freq, -1.0, 1.0)
    cos = np.cos(ang.astype(np.float64)).astype(np.float32)
    sin = (np.sin(ang.astype(np.float64)) * sign[None, :]).astype(np.float32)
    return jnp.asarray(cos), jnp.asarray(sin)


def _gate_weights(w_r, w_i):
    def bd(w):
        w = w.reshape(4, 2, RG_BLOCK_W, RG_BLOCK_W)
        z = jnp.zeros_like(w[:, 0])
        top = jnp.concatenate([w[:, 0], z], axis=2)
        bot = jnp.concatenate([z, w[:, 1]], axis=2)
        return jnp.concatenate([top, bot], axis=1)
    w = jnp.concatenate([bd(w_r[0]), bd(w_i[0]), bd(w_r[1]), bd(w_i[1])], axis=2)
    return (0.5 * w).astype(BF16)


def kernel(x_prompt, x_sample, cache_attn_k, cache_attn_v, state_rglru, c, c_ctx, norm_g, w_mod, b_mod, ffn_w_gate, ffn_w_up, ffn_w_down, w_in, w_out, diff_lambda, subln_g, conv_w, conv_b, rg_w_r, rg_b_r, rg_w_i, rg_b_i, rg_lambda, final_g):
    l = 0
    bsz, seq, _ = x_prompt.shape
    dbsz, dseq, _ = x_sample.shape
    past = cache_attn_k.shape[2]

    ng = norm_g[l]
    ffn0_w = [w[l, 0].astype(BF16) for w in (ffn_w_gate, ffn_w_up, ffn_w_down)]
    win = w_in[l].astype(BF16)
    wg = _gate_weights(rg_w_r[l], rg_w_i[l])
    bg = 0.5 * jnp.stack([rg_b_r[l, 0], rg_b_i[l, 0], rg_b_r[l, 1], rg_b_i[l, 1]])
    cw = conv_w[l]
    cb = conv_b[l][None, :]
    lam = rg_lambda[l]
    dl = diff_lambda[l]
    sg = subln_g[l][None, :]
    fg = final_g[None, :]
    tm = 1024

    c8 = jnp.concatenate([c_ctx[None, :], c, jnp.zeros((SUBLANES - 1 - dbsz, D_MODEL), F32)], axis=0)
    mod3 = _mod_call(c8, w_mod[l], b_mod[l][None, :]).reshape(SUBLANES, N_MOD, D_MODEL)

    def mixers(x, *, rows_per_mod, mod_base, nb, s, h0, casts, k_ctx=None, v_ctx=None,
               rope_tabs=None):
        x1, cast = _ffn_call(x, mod3, ng, *ffn0_w, sub=0, rows_per_mod=rows_per_mod,
                             mod_base=mod_base, tm=tm, casts=casts)
        q, k, v, xr, xg = _proj_call(x1, mod3, ng, win, rows_per_mod=rows_per_mod,
                                     mod_base=mod_base, tm=tm, rope_tabs=rope_tabs)
        q3 = q.reshape(nb, s, D_ATTN)
        xr3 = xr.reshape(nb, s, D_RG)
        xg3 = xg.reshape(nb, s, D_RG)
        if k_ctx is None:
            o, rg, h_last = _ctx_mixers_call(q3, k, v, dl, sg, xr3, xg3, h0, cw, cb, wg, bg, lam)
        else:
            o = _attn_call(q3, k.reshape(nb, s, D_ATTN), v.reshape(nb, s, D_ATTN), k_ctx, v_ctx,
                           dl, sg, tq=LATENT_TQ)
            rg, h_last = _rglru_call(xr3, xg3, h0, cw, cb, wg, bg, lam, cc=LATENT_CC)
        return x1, o.reshape(nb * s, D_ATTN), rg.reshape(nb * s, D_RG), k, v, h_last, cast

    x1p, op, rgp, k_new, v_new, h_new, (wgate1, wup1) = mixers(
        x_prompt.reshape(bsz * seq, D_MODEL), rows_per_mod=bsz * seq, mod_base=0, nb=bsz, s=seq,
        h0=jnp.zeros((bsz, 2, D_RG), F32), casts=[(ffn_w_gate, (l, 1)), (ffn_w_up, (l, 1))])
    x1s, os_, rgs, _, _, _, (wdown1, wout) = mixers(
        x_sample.reshape(dbsz * dseq, D_MODEL), rows_per_mod=dseq, mod_base=1, nb=dbsz, s=dseq,
        k_ctx=cache_attn_k[:, l].reshape(dbsz, past, D_ATTN).astype(BF16),
        v_ctx=cache_attn_v[:, l].reshape(dbsz, past, D_ATTN).astype(BF16),
        h0=state_rglru[:, l], rope_tabs=_rope_tables(dseq),
        casts=[(ffn_w_down, (l, 1)), (w_out, (l,))])

    def second_ffn(x1, o, rg, *, rows_per_mod, mod_base):
        y, _ = _ffn_call(x1, mod3, ng, wgate1, wup1, wdown1, sub=2, rows_per_mod=rows_per_mod,
                         mod_base=mod_base, tm=tm, mix=(o, rg, wout), final_g=fg)
        return y

    yp = second_ffn(x1p, op, rgp, rows_per_mod=bsz * seq, mod_base=0)
    ys = second_ffn(x1s, os_, rgs, rows_per_mod=dseq, mod_base=1)

    return (yp.reshape(bsz, seq, D_MODEL),
            ys.reshape(dbsz, dseq, D_MODEL),
            k_new.reshape(bsz, 1, seq, N_HEADS, V_DIM),
            v_new.reshape(bsz, 1, seq, N_HEADS, V_DIM),
            h_new.reshape(bsz, 1, 2, D_RG))
```
